```python
import math
import jax, jax.numpy as jnp
from jax import lax
import numpy as np

D_MODEL = 1024
BATCH = 16
SEQ = 2048
DEPTH = 1
DEC_BATCH = 8
DEC_SEQ = 64
PAST_LEN = 2048

CHUNK = 64
N_MEM = 256
EPS = 1e-6
HG_HEADS = 4
HG_DK = 128
HG_DV = 128
HG_WIDTH = HG_HEADS * HG_DV
HG_BLOCK = 16
MLA_HEADS = 8
Q_LORA = 384
KV_LORA = 256
QK_NOPE = 64
QK_ROPE = 32
V_HEAD = 64
MLA_WIDTH = MLA_HEADS * V_HEAD
ROPE_THETA = 10000.0
Q_BLOCK = 128
X_HEADS = 4
X_HEAD_DIM = 128
X_WIDTH = X_HEADS * X_HEAD_DIM
N_EXPERTS = 32
TOP_K = 4
D_FF = D_MODEL
SWIGLU_LIMIT = 7.0
SWIGLU_ALPHA = 1.702
MOE_BLOCK = 256
D_IN = 2 * HG_HEADS * HG_DK + 2 * HG_WIDTH + Q_LORA + KV_LORA + QK_ROPE + 2 * D_MODEL
IN_SPLIT_POINTS = (
    HG_HEADS * HG_DK,
    2 * HG_HEADS * HG_DK,
    2 * HG_HEADS * HG_DK + HG_WIDTH,
    2 * HG_HEADS * HG_DK + 2 * HG_WIDTH,
    2 * HG_HEADS * HG_DK + 2 * HG_WIDTH + Q_LORA,
    2 * HG_HEADS * HG_DK + 2 * HG_WIDTH + Q_LORA + KV_LORA,
    2 * HG_HEADS * HG_DK + 2 * HG_WIDTH + Q_LORA + KV_LORA + QK_ROPE,
    2 * HG_HEADS * HG_DK + 2 * HG_WIDTH + Q_LORA + KV_LORA + QK_ROPE + D_MODEL,
)

kernel_name = 'hybrid_hgrn2_mla_moe_stream_step'


def rmsnorm(x, g):
    xf = x.astype(jnp.float32)
    xf = xf * lax.rsqrt(jnp.mean(xf * xf, axis=-1, keepdims=True) + EPS)
    return (xf * g.astype(jnp.float32)).astype(x.dtype)


def rope(x, pos):
    half = x.shape[-1] // 2
    inv_freq = jnp.exp(-math.log(ROPE_THETA) * jnp.arange(half, dtype=jnp.float32) / half)
    ang = pos.astype(jnp.float32)[:, None] * inv_freq[None, :]
    shape = (ang.shape[0],) + (1,) * (x.ndim - 3) + (half,)
    cos, sin = jnp.cos(ang).reshape(shape), jnp.sin(ang).reshape(shape)
    xf = x.astype(jnp.float32)
    x1, x2 = xf[..., :half], xf[..., half:]
    return jnp.concatenate([x1 * cos - x2 * sin, x2 * cos + x1 * sin], axis=-1).astype(x.dtype)


def hgrn_block(S, blk):
    q, k, lf, v = blk
    L = q.shape[1]
    b = jnp.cumsum(lf, axis=1)
    o = jnp.einsum('blhk,bhkv->blhv', q * jnp.exp(b), S)
    causal = jnp.tril(jnp.ones((L, L), dtype=bool))[None, :, :, None, None]
    diff = b[:, :, None] - b[:, None, :]
    decay = jnp.exp(jnp.where(causal, diff, -jnp.inf))
    a = jnp.einsum('bthk,bshk,btshk->bhts', q, k, decay)
    o = o + jnp.einsum('bhts,bshv->bthv', a, v)
    b_last = b[:, -1]
    S = jnp.exp(b_last)[..., None] * S + jnp.einsum('blhk,blhv->bhkv', k * jnp.exp(b_last[:, None] - b), v)
    return S, o


def hgrn_recurrence(S0, q, k, lf, v):
    B, L, H, _ = q.shape
    if L > HG_BLOCK and L % HG_BLOCK == 0:
        n = L // HG_BLOCK
        def split(t):
            return jnp.moveaxis(t.reshape(B, n, HG_BLOCK, H, t.shape[-1]), 1, 0)
        S, o = lax.scan(hgrn_block, S0, (split(q), split(k), split(lf), split(v)))
        o = jnp.moveaxis(o, 0, 1).reshape(B, L, H, HG_DV)
    else:
        S, o = hgrn_block(S0, (q, k, lf, v))
    return S, o


def mla_attend(q_nope, q_pe, k_nope, k_pe, v, q_pos, k_pos):
    scale = (QK_NOPE + QK_ROPE) ** -0.5
    def block(args):
        qn, qp, qpos = args
        s = (jnp.einsum('bqhd,bkhd->bhqk', qn, k_nope) + jnp.einsum('bqhr,bkr->bhqk', qp, k_pe)).astype(jnp.float32) * scale
        visible = k_pos[None, :] < (qpos[:, None] // CHUNK + 1) * CHUNK
        s = jnp.where(visible[None, None], s, -jnp.inf)
        p = jax.nn.softmax(s, axis=-1).astype(v.dtype)
        return jnp.einsum('bhqk,bkhd->bqhd', p, v)
    B, Lq = q_nope.shape[0], q_nope.shape[1]
    if Lq > Q_BLOCK and Lq % Q_BLOCK == 0:
        n = Lq // Q_BLOCK
        def split(t):
            return jnp.moveaxis(t.reshape((B, n, Q_BLOCK) + t.shape[2:]), 1, 0)
        out = lax.map(block, (split(q_nope), split(q_pe), q_pos.reshape(n, Q_BLOCK)))
        return jnp.moveaxis(out, 0, 1).reshape(B, Lq, MLA_HEADS, V_HEAD)
    return block((q_nope, q_pe, q_pos))


def mem_kv(mem, g_mem, w_mk, w_mv):
    B = mem.shape[0]
    m = rmsnorm(mem, g_mem)
    k = (m @ w_mk).reshape(B, -1, X_HEADS, X_HEAD_DIM)
    v = (m @ w_mv).reshape(B, -1, X_HEADS, X_HEAD_DIM)
    return k, v


def mem_attend(u, mem_k, mem_v, w_xq, w_xo):
    B, L, _ = u.shape
    q = (u @ w_xq).reshape(B, L, X_HEADS, X_HEAD_DIM)
    s = jnp.einsum('blhd,bmhd->bhlm', q, mem_k.astype(u.dtype)).astype(jnp.float32) * X_HEAD_DIM ** -0.5
    p = jax.nn.softmax(s, axis=-1).astype(u.dtype)
    o = jnp.einsum('bhlm,bmhd->blhd', p, mem_v.astype(u.dtype)).reshape(B, L, X_WIDTH)
    return o @ w_xo


def moe(x, w_router, b_router, w_gu, b_gu, w_down, b_down):
    T, D = x.shape
    logits = (x @ w_router + b_router).astype(jnp.float32)
    top_val, top_idx = lax.top_k(logits, TOP_K)
    gate = jax.nn.softmax(top_val, axis=-1)
    A = T * TOP_K
    e_flat = top_idx.reshape(-1)
    tok_flat = jnp.arange(A, dtype=jnp.int32) // TOP_K
    order = jnp.argsort(e_flat)
    e_sorted = e_flat[order]
    counts = jnp.bincount(e_flat, length=N_EXPERTS)
    padded = (counts + MOE_BLOCK - 1) // MOE_BLOCK * MOE_BLOCK
    start = jnp.cumsum(counts) - counts
    pend = jnp.cumsum(padded)
    pstart = pend - padded
    dest = pstart[e_sorted] + jnp.arange(A, dtype=jnp.int32) - start[e_sorted]
    n_blocks = (A + N_EXPERTS * (MOE_BLOCK - 1) + MOE_BLOCK - 1) // MOE_BLOCK
    P = n_blocks * MOE_BLOCK
    row_tok = jnp.full((P,), T, dtype=jnp.int32).at[dest].set(tok_flat[order])
    row_gate = jnp.zeros((P,), jnp.float32).at[dest].set(gate.reshape(-1)[order])
    block_expert = jnp.minimum(jnp.searchsorted(pend, jnp.arange(n_blocks) * MOE_BLOCK, side='right'), N_EXPERTS - 1)
    x_pad = jnp.concatenate([x, jnp.zeros((1, D), x.dtype)], axis=0)
    def expert_block(args):
        rt, e = args
        gu = x_pad[rt] @ w_gu[e] + b_gu[e]
        g, up = gu[:, :D_FF], gu[:, D_FF:]
        g = jnp.minimum(g, SWIGLU_LIMIT)
        up = jnp.clip(up, -SWIGLU_LIMIT, SWIGLU_LIMIT)
        act = (up + 1.0) * g * jax.nn.sigmoid(SWIGLU_ALPHA * g)
        return act @ w_down[e] + b_down[e]
    yb = lax.map(expert_block, (row_tok.reshape(n_blocks, MOE_BLOCK), block_expert)).reshape(P, D)
    y = jnp.zeros((T + 1, D), x.dtype).at[row_tok].add(yb * row_gate[:, None].astype(x.dtype))
    return y[:T]


def trunk_layer(x, pos, mem_k, mem_v, past_ckv, past_kpe, hg_state, lb, g_mix, w_in, g_qa, w_uq, g_kva, w_ukv,
                hg_norm, w_pa, w_pb, w_out, g_x, w_xq, w_xo, g_ffn, w_router, b_router, w_gu, b_gu, w_down, b_down):
    B, L, _ = x.shape
    u = rmsnorm(x, g_mix)
    pq, pf, pi, pg, pqa, pkv, pkr, ga, gb = jnp.split(u @ w_in, IN_SPLIT_POINTS, axis=-1)
    f = lb + (1.0 - lb) * jax.nn.sigmoid(pf.astype(jnp.float32))
    def heads(t, d):
        return t.reshape(B, L, HG_HEADS, d)
    S_new, o = hgrn_recurrence(hg_state.astype(jnp.float32), heads(pq.astype(jnp.float32), HG_DK),
                               heads(1.0 - f, HG_DK), heads(jnp.log(f), HG_DK), heads(pi.astype(jnp.float32), HG_DV))
    o = rmsnorm(o, hg_norm.reshape(HG_HEADS, HG_DV)).astype(x.dtype)
    branch_a = (o * jax.nn.silu(heads(pg, HG_DV))).reshape(B, L, HG_WIDTH)
    qh = (rmsnorm(pqa, g_qa) @ w_uq).reshape(B, L, MLA_HEADS, QK_NOPE + QK_ROPE)
    q_nope, q_pe = qh[..., :QK_NOPE], rope(qh[..., QK_NOPE:], pos)
    ckv_new = rmsnorm(pkv, g_kva)
    kpe_new = rope(pkr, pos)
    if past_ckv is None:
        ckv_all, kpe_all, k_pos = ckv_new, kpe_new, pos
    else:
        past_len = past_ckv.shape[1]
        ckv_all = jnp.concatenate([past_ckv.astype(x.dtype), ckv_new], axis=1)
        kpe_all = jnp.concatenate([past_kpe.astype(x.dtype), kpe_new], axis=1)
        k_pos = jnp.concatenate([jnp.arange(past_len, dtype=jnp.int32), pos])
    kv = (ckv_all @ w_ukv).reshape(B, ckv_all.shape[1], MLA_HEADS, QK_NOPE + V_HEAD)
    branch_b = mla_attend(q_nope, q_pe, kv[..., :QK_NOPE], kpe_all, kv[..., QK_NOPE:], pos, k_pos).reshape(B, L, MLA_WIDTH)
    mixed = (jax.nn.sigmoid(ga) * (branch_a @ w_pa) + jax.nn.sigmoid(gb) * (branch_b @ w_pb)) @ w_out
    h = x + mixed
    h = h + mem_attend(rmsnorm(h, g_x), mem_k, mem_v, w_xq, w_xo)
    h = h + moe(rmsnorm(h, g_ffn).reshape(B * L, D_MODEL), w_router, b_router, w_gu, b_gu, w_down, b_down).reshape(B, L, D_MODEL)
    return h, ckv_new, kpe_new, S_new.astype(x.dtype)


def setup_inputs(seed: int = 0) -> dict:
    key = jax.random.key(seed)
    ks = iter(jax.random.split(key, 48))
    def nrm(shape, scale):
        return scale * jax.random.normal(next(ks), shape, jnp.float32)
    def gain(shape):
        return 1.0 + nrm(shape, 0.02)
    return {
        'x_prompt': nrm((BATCH, SEQ, D_MODEL), 1.0),
        'x_sample': nrm((DEC_BATCH, DEC_SEQ, D_MODEL), 1.0),
        'cache_mla_ckv': nrm((DEPTH, DEC_BATCH, PAST_LEN, KV_LORA), 1.0),
        'cache_mla_kpe': nrm((DEPTH, DEC_BATCH, PAST_LEN, QK_ROPE), 1.0),
        'state_hgrn': nrm((DEPTH, DEC_BATCH, HG_HEADS, HG_DK, HG_DV), 0.3),
        'cache_mem_k': nrm((DEPTH, DEC_BATCH, N_MEM, X_HEADS, X_HEAD_DIM), 1.0),
        'cache_mem_v': nrm((DEPTH, DEC_BATCH, N_MEM, X_HEADS, X_HEAD_DIM), 1.0),
        'mem_prompt': nrm((BATCH, N_MEM, D_MODEL), 1.0),
        'hg_lb_logits': nrm((DEPTH + 1, HG_HEADS * HG_DK), 0.5),
        'g_mix': gain((DEPTH, D_MODEL)),
        'w_in': nrm((DEPTH, D_MODEL, D_IN), D_MODEL ** -0.5),
        'g_qa': gain((DEPTH, Q_LORA)),
        'w_uq': nrm((DEPTH, Q_LORA, MLA_HEADS * (QK_NOPE + QK_ROPE)), Q_LORA ** -0.5),
        'g_kva': gain((DEPTH, KV_LORA)),
        'w_ukv': nrm((DEPTH, KV_LORA, MLA_HEADS * (QK_NOPE + V_HEAD)), KV_LORA ** -0.5),
        'hg_norm': gain((DEPTH, HG_WIDTH)),
        'w_pa': nrm((DEPTH, HG_WIDTH, D_MODEL), HG_WIDTH ** -0.5),
        'w_pb': nrm((DEPTH, MLA_WIDTH, D_MODEL), MLA_WIDTH ** -0.5),
        'w_out': nrm((DEPTH, D_MODEL, D_MODEL), D_MODEL ** -0.5),
        'g_mem': gain((DEPTH, D_MODEL)),
        'w_mk': nrm((DEPTH, D_MODEL, X_WIDTH), D_MODEL ** -0.5),
        'w_mv': nrm((DEPTH, D_MODEL, X_WIDTH), D_MODEL ** -0.5),
        'g_x': gain((DEPTH, D_MODEL)),
        'w_xq': nrm((DEPTH, D_MODEL, X_WIDTH), D_MODEL ** -0.5),
        'w_xo': nrm((DEPTH, X_WIDTH, D_MODEL), X_WIDTH ** -0.5),
        'g_ffn': gain((DEPTH, D_MODEL)),
        'w_router': nrm((DEPTH, D_MODEL, N_EXPERTS), D_MODEL ** -0.5),
        'b_router': nrm((DEPTH, N_EXPERTS), 0.01),
        'w_gu': nrm((DEPTH, N_EXPERTS, D_MODEL, 2 * D_FF), D_MODEL ** -0.5),
        'b_gu': nrm((DEPTH, N_EXPERTS, 2 * D_FF), 0.01),
        'w_down': nrm((DEPTH, N_EXPERTS, D_FF, D_MODEL), D_FF ** -0.5),
        'b_down': nrm((DEPTH, N_EXPERTS, D_MODEL), 0.01),
        'g_final': gain((D_MODEL,)),
    }


def reference(x_prompt, x_sample, cache_mla_ckv, cache_mla_kpe, state_hgrn, cache_mem_k, cache_mem_v, mem_prompt,
              hg_lb_logits, g_mix, w_in, g_qa, w_uq, g_kva, w_ukv, hg_norm, w_pa, w_pb, w_out, g_mem, w_mk, w_mv,
              g_x, w_xq, w_xo, g_ffn, w_router, b_router, w_gu, b_gu, w_down, b_down, g_final):
    Bp, Lp, _ = x_prompt.shape
    Ls = x_sample.shape[1]
    past_len = cache_mla_ckv.shape[2]
    pos_p = jnp.arange(Lp, dtype=jnp.int32)
    pos_s = past_len + jnp.arange(Ls, dtype=jnp.int32)
    lb_all = jnp.cumsum(jax.nn.softmax(hg_lb_logits.astype(jnp.float32), axis=0), axis=0)
    hp, hs = x_prompt, x_sample
    ckv_p, kpe_p, hg_p, mk_p, mv_p, ckv_s, kpe_s, hg_s = [], [], [], [], [], [], [], []
    for l in range(DEPTH):
        lw = (lb_all[l], g_mix[l], w_in[l], g_qa[l], w_uq[l], g_kva[l], w_ukv[l], hg_norm[l], w_pa[l], w_pb[l],
              w_out[l], g_x[l], w_xq[l], w_xo[l], g_ffn[l], w_router[l], b_router[l], w_gu[l], b_gu[l],
              w_down[l], b_down[l])
        mk, mv = mem_kv(mem_prompt, g_mem[l], w_mk[l], w_mv[l])
        s0 = jnp.zeros((Bp, HG_HEADS, HG_DK, HG_DV), jnp.float32)
        hp, c1, k1, s1 = trunk_layer(hp, pos_p, mk, mv, None, None, s0, *lw)
        hs, c2, k2, s2 = trunk_layer(hs, pos_s, cache_mem_k[l], cache_mem_v[l], cache_mla_ckv[l], cache_mla_kpe[l],
                                     state_hgrn[l], *lw)
        ckv_p.append(c1); kpe_p.append(k1); hg_p.append(s1); mk_p.append(mk); mv_p.append(mv)
        ckv_s.append(c2); kpe_s.append(k2); hg_s.append(s2)
    y_prompt = rmsnorm(hp, g_final)
    y_sample = rmsnorm(hs, g_final)
    return (y_prompt, y_sample, jnp.stack(ckv_p), jnp.stack(kpe_p), jnp.stack(hg_p), jnp.stack(mk_p), jnp.stack(mv_p),
            jnp.stack(ckv_s), jnp.stack(kpe_s), jnp.stack(hg_s))
```

```python
import functools
import math

import jax
import jax.numpy as jnp
from jax import lax
from jax.experimental import pallas as pl
from jax.experimental.pallas import tpu as pltpu

F32 = jnp.float32
BF16 = jnp.bfloat16
I32 = jnp.int32

D_MODEL = 1024
CHUNK = 64
EPS = 1e-6
HG_HEADS = 4
HG_DK = 128
HG_DV = 128
HG_WIDTH = HG_HEADS * HG_DV
HG_SUB = 16
MLA_HEADS = 8
Q_LORA = 384
KV_LORA = 256
QK_NOPE = 64
QK_ROPE = 32
V_HEAD = 64
MLA_WIDTH = MLA_HEADS * V_HEAD
HEAD_PAD = 128
ROPE_THETA = 10000.0
X_HEADS = 4
X_HEAD_DIM = 128
X_WIDTH = X_HEADS * X_HEAD_DIM
N_EXPERTS = 32
TOP_K = 4
D_FF = D_MODEL
SWIGLU_LIMIT = 7.0
SWIGLU_ALPHA = 1.702
LANES = 128
NEG = -1e30

VMEM_LIMIT = 56 * 1024 * 1024
TOKEN_TILE = 512
ATTN_TILE = 256
MOE_ROWS = 256


def _dot(a, b):
    return jnp.dot(a, b, preferred_element_type=F32)


def _dot_nt(a, b):
    return lax.dot_general(a, b, (((1,), (1,)), ((), ())), preferred_element_type=F32)


def _dot_tn(a, b):
    return lax.dot_general(a, b, (((0,), (0,)), ((), ())), preferred_element_type=F32)


def _rms(x, g):
    return x * lax.rsqrt(jnp.mean(x * x, axis=-1, keepdims=True) + EPS) * g


def _sigmoid(x):
    return 1.0 / (1.0 + jnp.exp(-x))


def _const_spec(shape):
    zeros = (0,) * len(shape)
    return pl.BlockSpec(shape, lambda *_: zeros, pipeline_mode=pl.Buffered(1))


def _params(n_axes):
    return pltpu.CompilerParams(dimension_semantics=("arbitrary",) * n_axes, vmem_limit_bytes=VMEM_LIMIT)


def _inproj_kernel(x_ref, gmix_ref, whg_ref, wg_ref, wqa_ref, wkv_ref, wkr_ref, gqa_ref, wq_ref, wqr_ref, gkva_ref,
                   cosq_ref, sinq_ref, cosk_ref, sink_ref,
                   hg_ref, sg_ref, q_ref, ckv_ref, kpe_ref):
    u = _rms(x_ref[...], gmix_ref[...]).astype(BF16)
    hg_ref[...] = _dot(u, whg_ref[...])
    sg_ref[...] = _sigmoid(_dot(u, wg_ref[...])).astype(BF16)
    qn = _rms(_dot(u, wqa_ref[...]), gqa_ref[...]).astype(BF16)
    cosq = jnp.concatenate([cosq_ref[...]] * MLA_HEADS, axis=-1)
    sinq = jnp.concatenate([sinq_ref[...]] * MLA_HEADS, axis=-1)
    scale = (QK_NOPE + QK_ROPE) ** -0.5
    q = (_dot(qn, wq_ref[...]) * cosq + _dot(qn, wqr_ref[...]) * sinq) * scale
    q_ref[...] = q.astype(BF16)
    ckv_ref[...] = _rms(_dot(u, wkv_ref[...]), gkva_ref[...])
    pk = _dot(u, wkr_ref[...])
    kpe_ref[...] = pk[:, :LANES] * cosk_ref[...] + pk[:, LANES:] * sink_ref[...]


def _inproj(x, lw, tabs, tm):
    t = x.shape[0]
    nt = tabs["cosq"].shape[0] // tm
    tok = lambda i: (i, 0)
    tab = lambda i: (i % nt, 0)
    consts = [lw["g_mix"], lw["w_hg"], lw["w_g"], lw["w_qa"], lw["w_kv"], lw["w_kr"], lw["g_qa"], lw["w_q"], lw["w_qr"],
              lw["g_kva"]]
    in_specs = ([pl.BlockSpec((tm, D_MODEL), tok)] + [_const_spec(c.shape) for c in consts]
                + [pl.BlockSpec((tm, LANES), tab)] * 4)
    out_shape = (jax.ShapeDtypeStruct((t, 4 * HG_WIDTH), F32), jax.ShapeDtypeStruct((t, 2 * D_MODEL), BF16),
                 jax.ShapeDtypeStruct((t, MLA_HEADS * HEAD_PAD), BF16), jax.ShapeDtypeStruct((t, KV_LORA), F32),
                 jax.ShapeDtypeStruct((t, LANES), F32))
    out_specs = tuple(pl.BlockSpec((tm, s.shape[1]), tok) for s in out_shape)
    return pl.pallas_call(
        _inproj_kernel, grid=(t // tm,), in_specs=in_specs, out_specs=out_specs, out_shape=out_shape,
        compiler_params=_params(1), name="inproj",
    )(x, *consts, tabs["cosq"], tabs["sinq"], tabs["cosk"], tabs["sink"])


def _kvbuild_kernel(ckv_ref, kpe_ref, wkn_ref, wv_ref, e_ref, k_ref, v_ref):
    c = ckv_ref[...].astype(BF16)
    k_ref[...] = (_dot(c, wkn_ref[...]) + _dot(kpe_ref[...].astype(BF16), e_ref[...])).astype(BF16)
    v_ref[...] = _dot(c, wv_ref[...]).astype(BF16)


def _kvbuild(ckv, kpe, lw, tm):
    n = ckv.shape[0]
    tok = lambda i: (i, 0)
    consts = [lw["w_kn"], lw["w_v"], lw["e_kpe"]]
    out_shape = (jax.ShapeDtypeStruct((n, MLA_HEADS * HEAD_PAD), BF16), jax.ShapeDtypeStruct((n, MLA_WIDTH), BF16))
    return pl.pallas_call(
        _kvbuild_kernel, grid=(n // tm,),
        in_specs=[pl.BlockSpec((tm, KV_LORA), tok), pl.BlockSpec((tm, LANES), tok)] + [_const_spec(c.shape) for c in consts],
        out_specs=tuple(pl.BlockSpec((tm, s.shape[1]), tok) for s in out_shape), out_shape=out_shape,
        compiler_params=_params(1), name="kvbuild",
    )(ckv, kpe, *consts)


def _attn_kernel(q_ref, k_ref, v_ref, o_ref, m_sc, l_sc, acc_sc, *, tq, tk, lk_true, q_off):
    q0 = q_off + pl.program_id(2) * tq
    lim_first = (q0 // CHUNK + 1) * CHUNK
    lim_last = ((q0 + tq - 1) // CHUNK + 1) * CHUNK
    n_tiles = (jnp.minimum(lim_last, lk_true) + tk - 1) // tk
    n_full = jnp.minimum(lim_first, lk_true) // tk
    m_sc[...] = jnp.full(m_sc.shape, NEG, F32)
    l_sc[...] = jnp.zeros(l_sc.shape, F32)
    acc_sc[...] = jnp.zeros(acc_sc.shape, F32)

    def step(j, masked):
        k0 = pl.multiple_of(j * tk, tk)
        kt = k_ref[pl.ds(k0, tk), :]
        vt = v_ref[pl.ds(k0, tk), :]
        for h in range(2):
            s = _dot_nt(q_ref[:, h * HEAD_PAD:(h + 1) * HEAD_PAD], kt[:, h * HEAD_PAD:(h + 1) * HEAD_PAD])
            if masked:
                kpos = k0 + lax.broadcasted_iota(I32, (tq, tk), 1)
                qpos = q0 + lax.broadcasted_iota(I32, (tq, tk), 0)
                lim = jnp.minimum((qpos // CHUNK + 1) * CHUNK, lk_true)
                s = jnp.where(kpos < lim, s, NEG)
            m_old = m_sc[h]
            m_new = jnp.maximum(m_old, jnp.max(s, axis=-1, keepdims=True))
            alpha = jnp.exp(m_old - m_new)
            p = jnp.exp(s - m_new)
            l_sc[h] = alpha * l_sc[h] + jnp.sum(p, axis=-1, keepdims=True)
            acc_sc[h] = alpha * acc_sc[h] + _dot(p.astype(BF16), vt)
            m_sc[h] = m_new

    def full_step(j, c):
        step(j, False)
        return c

    def masked_step(j, c):
        step(j, True)
        return c

    lax.fori_loop(0, n_full, full_step, 0)
    lax.fori_loop(n_full, n_tiles, masked_step, 0)
    o0 = acc_sc[0] / l_sc[0]
    o1 = acc_sc[1] / l_sc[1]
    lane = lax.broadcasted_iota(I32, o0.shape, 1)
    o_ref[...] = jnp.where(lane < V_HEAD, o0, o1).astype(BF16)


def _attention(q, k, v, lk_true, q_off, tq, tk):
    b, lq, _ = q.shape
    lk = k.shape[1]
    kern = functools.partial(_attn_kernel, tq=tq, tk=tk, lk_true=lk_true, q_off=q_off)
    return pl.pallas_call(
        kern, grid=(b, MLA_HEADS // 2, lq // tq),
        in_specs=[pl.BlockSpec((None, tq, 2 * HEAD_PAD), lambda bi, hp, i: (bi, i, hp)),
                  pl.BlockSpec((None, lk, 2 * HEAD_PAD), lambda bi, hp, i: (bi, 0, hp)),
                  pl.BlockSpec((None, lk, 2 * V_HEAD), lambda bi, hp, i: (bi, 0, hp))],
        out_specs=pl.BlockSpec((None, tq, 2 * V_HEAD), lambda bi, hp, i: (bi, i, hp)),
        out_shape=jax.ShapeDtypeStruct((b, lq, MLA_WIDTH), BF16),
        scratch_shapes=[pltpu.VMEM((2, tq, 1), F32), pltpu.VMEM((2, tq, 1), F32), pltpu.VMEM((2, tq, 2 * V_HEAD), F32)],
        compiler_params=_params(3), name="attn",
    )(q, k, v)


def _hgrn_kernel(q_ref, f_ref, i_ref, g_ref, lb_ref, norm_ref, s0_ref, a_ref, sout_ref, st_sc, *, chunk, n_chunks):
    t = pl.program_id(2)

    @pl.when(t == 0)
    def _():
        st_sc[...] = s0_ref[...].T

    lb = lb_ref[...]
    row = lax.broadcasted_iota(I32, (chunk, chunk), 0)
    col = lax.broadcasted_iota(I32, (chunk, chunk), 1)
    tri = jnp.where(row >= col, 1.0, 0.0).astype(BF16)
    nsub = chunk // HG_SUB
    sub_row = lax.broadcasted_iota(I32, (nsub, HG_SUB, 1), 1)

    def body(c, carry):
        r0 = pl.multiple_of(c * chunk, chunk)
        q = q_ref[pl.ds(r0, chunk), :]
        v = i_ref[pl.ds(r0, chunk), :]
        f = lb + (1.0 - lb) * _sigmoid(f_ref[pl.ds(r0, chunk), :])
        lf = jnp.log(f)
        k = 1.0 - f
        hi = lf.astype(BF16)
        r1 = lf - hi.astype(F32)
        mid = r1.astype(BF16)
        lo = (r1 - mid.astype(F32)).astype(BF16)
        b = _dot(tri, hi) + _dot(tri, mid) + _dot(tri, lo)
        st = st_sc[...]
        o = _dot_nt((q * jnp.exp(b)).astype(BF16), st.astype(BF16))
        a_cross = jnp.zeros((chunk, chunk), F32)
        m = chunk // 2
        while m >= HG_SUB:
            blk = 2 * m
            shp = (chunk // blk, blk, HG_DK)
            b3 = b.reshape(shp)
            ref = b3[:, m - 1:m, :]
            first = lax.broadcasted_iota(I32, (chunk // blk, blk, 1), 1) < m
            ql = jnp.where(first, 0.0, q.reshape(shp) * jnp.exp(jnp.minimum(b3 - ref, 0.0)))
            kl = jnp.where(first, k.reshape(shp) * jnp.exp(jnp.minimum(ref - b3, 0.0)), 0.0)
            a_l = _dot_nt(ql.reshape(chunk, HG_DK).astype(BF16), kl.reshape(chunk, HG_DK).astype(BF16))
            a_cross = a_cross + jnp.where(row // blk == col // blk, a_l, 0.0)
            m //= 2
        o = o + _dot(a_cross.astype(BF16), v.astype(BF16))
        shp = (nsub, HG_SUB, HG_DK)
        q3, k3, b3, v3 = q.reshape(shp), k.reshape(shp), b.reshape(shp), v.reshape(shp)
        od = jnp.zeros(shp, F32)
        for s in range(HG_SUB):
            w = q3 * jnp.exp(jnp.minimum(b3 - b3[:, s:s + 1, :], 0.0)) * k3[:, s:s + 1, :]
            a_col = jnp.where(sub_row >= s, jnp.sum(w, axis=-1, keepdims=True), 0.0)
            od = od + a_col * v3[:, s:s + 1, :]
        o = o + od.reshape(chunk, HG_DV)
        b_last = b[chunk - 1:chunk, :]
        kd = k * jnp.exp(b_last - b)
        st_sc[...] = st * jnp.exp(b_last) + _dot_tn(v.astype(BF16), kd.astype(BF16))
        g = g_ref[pl.ds(r0, chunk), :]
        a_ref[pl.ds(r0, chunk), :] = (_rms(o, norm_ref[...]) * (g * _sigmoid(g))).astype(BF16)
        return carry

    lax.fori_loop(0, n_chunks, body, 0)

    @pl.when(t == pl.num_programs(2) - 1)
    def _():
        sout_ref[...] = st_sc[...].T


def _hgrn(hg, lb, norm, s0, lt, chunk):
    b, l, _ = hg.shape
    kern = functools.partial(_hgrn_kernel, chunk=chunk, n_chunks=lt // chunk)
    seg = lambda j: pl.BlockSpec((None, lt, HG_DK), lambda bi, h, t: (bi, t, j * HG_HEADS + h))
    head_vec = pl.BlockSpec((1, HG_DK), lambda bi, h, t: (0, h))
    state = pl.BlockSpec((None, None, HG_DK, HG_DV), lambda bi, h, t: (bi, h, 0, 0))
    return pl.pallas_call(
        kern, grid=(b, HG_HEADS, l // lt),
        in_specs=[seg(0), seg(1), seg(2), seg(3), head_vec, head_vec, state],
        out_specs=(pl.BlockSpec((None, lt, HG_DV), lambda bi, h, t: (bi, t, h)), state),
        out_shape=(jax.ShapeDtypeStruct((b, l, HG_WIDTH), BF16), jax.ShapeDtypeStruct(s0.shape, F32)),
        scratch_shapes=[pltpu.VMEM((HG_DV, HG_DK), F32)],
        compiler_params=_params(3), name="hgrn",
    )(hg, hg, hg, hg, lb, norm, s0)


def _memkv_kernel(m_ref, g_ref, wk_ref, wv_ref, k_ref, v_ref):
    m = _rms(m_ref[...], g_ref[...]).astype(BF16)
    k_ref[...] = _dot(m, wk_ref[...])
    v_ref[...] = _dot(m, wv_ref[...])


def _memkv(mem, lw, tm):
    n = mem.shape[0]
    tok = lambda i: (i, 0)
    consts = [lw["g_mem"], lw["w_mk"], lw["w_mv"]]
    out_shape = (jax.ShapeDtypeStruct((n, X_WIDTH), F32),) * 2
    return pl.pallas_call(
        _memkv_kernel, grid=(n // tm,),
        in_specs=[pl.BlockSpec((tm, D_MODEL), tok)] + [_const_spec(c.shape) for c in consts],
        out_specs=(pl.BlockSpec((tm, X_WIDTH), tok),) * 2, out_shape=out_shape,
        compiler_params=_params(1), name="memkv",
    )(mem, *consts)


def _merge_kernel(x_ref, a_ref, bo_ref, sg_ref, mk_ref, mv_ref, wpa_ref, wpb_ref, wout_ref, gx_ref, wxq_ref, wxo_ref,
                  gffn_ref, wr_ref, br_ref, h_ref, u_ref, lg_ref, *, n_seg, seg):
    sg = sg_ref[...]
    mix = (sg[:, :D_MODEL].astype(F32) * _dot(a_ref[...], wpa_ref[...])
           + sg[:, D_MODEL:].astype(F32) * _dot(bo_ref[...], wpb_ref[...]))
    h1 = x_ref[...] + _dot(mix.astype(BF16), wout_ref[...])
    u2 = _rms(h1, gx_ref[...]).astype(BF16)
    qx = (_dot(u2, wxq_ref[...]) * X_HEAD_DIM ** -0.5).astype(BF16)
    segs = []
    for j in range(n_seg):
        kj = mk_ref[j].astype(BF16)
        vj = mv_ref[j].astype(BF16)
        heads = []
        for h in range(X_HEADS):
            hs = slice(h * X_HEAD_DIM, (h + 1) * X_HEAD_DIM)
            s = _dot_nt(qx[j * seg:(j + 1) * seg, hs], kj[:, hs])
            p = jnp.exp(s - jnp.max(s, axis=-1, keepdims=True))
            heads.append(_dot(p.astype(BF16), vj[:, hs]) / jnp.sum(p, axis=-1, keepdims=True))
        segs.append(jnp.concatenate(heads, axis=-1))
    ox = segs[0] if n_seg == 1 else jnp.concatenate(segs, axis=0)
    h2 = h1 + _dot(ox.astype(BF16), wxo_ref[...])
    h_ref[...] = h2
    u3 = _rms(h2, gffn_ref[...])
    u_ref[...] = u3
    lg_ref[...] = _dot(u3.astype(BF16), wr_ref[...]) + br_ref[...]


def _merge(x, a, bo, sg, mk, mv, lw, tm, seq_len):
    t = x.shape[0]
    n_mem = mk.shape[1]
    n_seg = max(1, tm // seq_len)
    seg = tm // n_seg
    tiles_per_seq = max(1, seq_len // tm)
    tok = lambda i: (i, 0)
    mem = pl.BlockSpec((n_seg, n_mem, X_WIDTH), lambda i: (i // tiles_per_seq, 0, 0))
    consts = [lw["w_pa"], lw["w_pb"], lw["w_out"], lw["g_x"], lw["w_xq"], lw["w_xo"], lw["g_ffn"], lw["w_router"],
              lw["b_router"]]
    out_shape = (jax.ShapeDtypeStruct((t, D_MODEL), F32), jax.ShapeDtypeStruct((t, D_MODEL), F32),
                 jax.ShapeDtypeStruct((t, LANES), F32))
    kern = functools.partial(_merge_kernel, n_seg=n_seg, seg=seg)
    return pl.pallas_call(
        kern, grid=(t // tm,),
        in_specs=[pl.BlockSpec((tm, D_MODEL), tok), pl.BlockSpec((tm, HG_WIDTH), tok), pl.BlockSpec((tm, MLA_WIDTH), tok),
                  pl.BlockSpec((tm, 2 * D_MODEL), tok), mem, mem] + [_const_spec(c.shape) for c in consts],
        out_specs=tuple(pl.BlockSpec((tm, s.shape[1]), tok) for s in out_shape), out_shape=out_shape,
        compiler_params=_params(1), name="merge",
    )(x, a, bo, sg, mk, mv, *consts)


def _route_kernel(lg_ref, idx_ref, gate_ref, rank_ref, cnt_ref, carry_sc, *, tr):
    @pl.when(pl.program_id(0) == 0)
    def _():
        carry_sc[...] = jnp.zeros(carry_sc.shape, F32)

    lane = lax.broadcasted_iota(I32, (tr, LANES), 1).astype(F32)
    logit = jnp.where(lane < N_EXPERTS, lg_ref[...], -jnp.inf)
    vals, idxs, hots = [], [], []
    for _ in range(TOP_K):
        top = jnp.max(logit, axis=-1, keepdims=True)
        first = jnp.min(jnp.where(logit == top, lane, float(LANES)), axis=-1, keepdims=True)
        hot = lane == first
        vals.append(top)
        idxs.append(first)
        hots.append(hot)
        logit = jnp.where(hot, -jnp.inf, logit)
    exps = [jnp.exp(v - vals[0]) for v in vals]
    denom = exps[0] + exps[1] + exps[2] + exps[3]
    hot_all = sum(jnp.where(h, 1.0, 0.0) for h in hots)
    row = lax.broadcasted_iota(I32, (tr, tr), 0)
    col = lax.broadcasted_iota(I32, (tr, tr), 1)
    before = jnp.where(row > col, 1.0, 0.0).astype(BF16)
    prefix = _dot(before, hot_all.astype(BF16)) + carry_sc[...]
    lane4 = lax.broadcasted_iota(I32, (tr, TOP_K), 1)
    idx = jnp.zeros((tr, TOP_K), F32)
    gate = jnp.zeros((tr, TOP_K), F32)
    rank = jnp.zeros((tr, TOP_K), F32)
    for k in range(TOP_K):
        rk = jnp.sum(jnp.where(hots[k], prefix, 0.0), axis=-1, keepdims=True)
        idx = jnp.where(lane4 == k, idxs[k], idx)
        gate = jnp.where(lane4 == k, exps[k] / denom, gate)
        rank = jnp.where(lane4 == k, rk, rank)
    idx_ref[...] = idx.astype(I32)
    gate_ref[...] = gate
    rank_ref[...] = rank.astype(I32)
    carry_sc[...] += jnp.sum(hot_all, axis=0, keepdims=True)
    cnt_ref[...] = carry_sc[...]


def _route(logits, tr):
    t = logits.shape[0]
    tok = lambda i: (i, 0)
    out_shape = (jax.ShapeDtypeStruct((t, TOP_K), I32), jax.ShapeDtypeStruct((t, TOP_K), F32),
                 jax.ShapeDtypeStruct((t, TOP_K), I32), jax.ShapeDtypeStruct((1, LANES), F32))
    return pl.pallas_call(
        functools.partial(_route_kernel, tr=tr), grid=(t // tr,),
        in_specs=[pl.BlockSpec((tr, LANES), tok)],
        out_specs=(pl.BlockSpec((tr, TOP_K), tok),) * 3 + (pl.BlockSpec((1, LANES), lambda i: (0, 0)),),
        out_shape=out_shape, scratch_shapes=[pltpu.VMEM((1, LANES), F32)],
        compiler_params=_params(1), name="route",
    )(logits)


def _moe_kernel(be_ref, rt_ref, rd_ref, x_hbm, wgu_ref, bgu_ref, wd_ref, bd_ref, y_hbm, xs_sc, ys_sc, gsem, ssem, *, bm):
    del be_ref
    b = pl.program_id(0)

    def gather(r):
        return pltpu.make_async_copy(x_hbm.at[pl.ds(rt_ref[0, 0, r], 1), :], xs_sc.at[pl.ds(r, 1), :], gsem)

    def scatter(r):
        return pltpu.make_async_copy(ys_sc.at[pl.ds(r, 1), :], y_hbm.at[pl.ds(rd_ref[0, 0, r], 1), :], ssem)

    def start_gather(r, c):
        gather(r).start()
        return c

    def start_scatter(r, c):
        scatter(r).start()
        return c

    def wait_scatters():
        pltpu.make_async_copy(ys_sc, y_hbm.at[pl.ds(0, bm), :], ssem).wait()

    lax.fori_loop(0, bm, start_gather, 0, unroll=8)
    pltpu.make_async_copy(x_hbm.at[pl.ds(0, bm), :], xs_sc, gsem).wait()
    gu = _dot(xs_sc[...].astype(BF16), wgu_ref[...]) + bgu_ref[...]
    g = jnp.minimum(gu[:, :D_FF], SWIGLU_LIMIT)
    up = jnp.clip(gu[:, D_FF:], -SWIGLU_LIMIT, SWIGLU_LIMIT)
    act = (up + 1.0) * g * _sigmoid(SWIGLU_ALPHA * g)
    y = _dot(act.astype(BF16), wd_ref[...]) + bd_ref[...]

    @pl.when(b > 0)
    def _():
        wait_scatters()

    ys_sc[...] = y
    lax.fori_loop(0, bm, start_scatter, 0, unroll=8)

    @pl.when(b == pl.num_programs(0) - 1)
    def _():
        wait_scatters()


def _moe(x, block_expert, row_tok, row_dst, lw, bm, n_out_rows):
    n_blocks = block_expert.shape[0]
    rows = pl.BlockSpec((1, 1, bm), lambda b, be: (b, 0, 0), memory_space=pltpu.SMEM)
    grid_spec = pltpu.PrefetchScalarGridSpec(
        num_scalar_prefetch=1, grid=(n_blocks,),
        in_specs=[rows, rows, pl.BlockSpec(memory_space=pl.ANY),
                  pl.BlockSpec((None, D_MODEL, 2 * D_FF), lambda b, be: (be[b], 0, 0)),
                  pl.BlockSpec((None, 1, 2 * D_FF), lambda b, be: (be[b], 0, 0)),
                  pl.BlockSpec((None, D_FF, D_MODEL), lambda b, be: (be[b], 0, 0)),
                  pl.BlockSpec((None, 1, D_MODEL), lambda b, be: (be[b], 0, 0))],
        out_specs=pl.BlockSpec(memory_space=pl.ANY),
        scratch_shapes=[pltpu.VMEM((bm, D_MODEL), F32), pltpu.VMEM((bm, D_MODEL), F32),
                        pltpu.SemaphoreType.DMA, pltpu.SemaphoreType.DMA])
    return pl.pallas_call(
        functools.partial(_moe_kernel, bm=bm), grid_spec=grid_spec,
        out_shape=jax.ShapeDtypeStruct((n_out_rows, D_MODEL), F32),
        compiler_params=_params(1), name="moe",
    )(block_expert, row_tok.reshape(n_blocks, 1, bm), row_dst.reshape(n_blocks, 1, bm), x,
      lw["w_gu"], lw["b_gu"], lw["w_down"], lw["b_down"])


def _final_kernel(h_ref, gate_ref, y0_ref, y1_ref, y2_ref, y3_ref, g_ref, o_ref):
    gate = gate_ref[...]
    h = h_ref[...]
    for k, y_ref in enumerate((y0_ref, y1_ref, y2_ref, y3_ref)):
        h = h + gate[:, k:k + 1] * y_ref[...]
    o_ref[...] = _rms(h, g_ref[...])


def _final(h, gate, y4, g_final, tm, tok_off, t_all):
    t = h.shape[0]
    tok = lambda i: (i, 0)
    slot = lambda k: pl.BlockSpec((tm, D_MODEL), lambda i: ((k * t_all + tok_off) // tm + i, 0))
    return pl.pallas_call(
        _final_kernel, grid=(t // tm,),
        in_specs=[pl.BlockSpec((tm, D_MODEL), tok), pl.BlockSpec((tm, TOP_K), tok), slot(0), slot(1), slot(2), slot(3),
                  _const_spec(g_final.shape)],
        out_specs=pl.BlockSpec((tm, D_MODEL), tok), out_shape=jax.ShapeDtypeStruct((t, D_MODEL), F32),
        compiler_params=_params(1), name="final",
    )(h, gate, y4, y4, y4, y4, g_final)


def _rope_tables(pos, reps):
    half = QK_ROPE // 2
    inv_freq = jnp.exp(-math.log(ROPE_THETA) * jnp.arange(half, dtype=F32) / half)
    ang = pos.astype(F32)[:, None] * inv_freq[None, :]
    cos = jnp.concatenate([jnp.cos(ang)] * 2, axis=-1)
    sin = jnp.concatenate([jnp.sin(ang)] * 2, axis=-1)
    n = pos.shape[0]
    ones = jnp.ones((n, QK_NOPE), F32)
    z = lambda w: jnp.zeros((n, w), F32)
    tabs = {"cosq": jnp.concatenate([ones, cos, z(HEAD_PAD - QK_NOPE - QK_ROPE)], axis=-1),
            "sinq": jnp.concatenate([z(QK_NOPE), sin, z(HEAD_PAD - QK_NOPE - QK_ROPE)], axis=-1),
            "cosk": jnp.concatenate([cos, z(LANES - QK_ROPE)], axis=-1),
            "sink": jnp.concatenate([sin, z(LANES - QK_ROPE)], axis=-1)}
    return {k: jnp.tile(v, (reps, 1)) for k, v in tabs.items()}


def _rot_cols(w):
    half = QK_ROPE // 2
    return jnp.concatenate([-w[..., half:], w[..., :half]], axis=-1)


def _layer_weights(l, lb, g_mix, w_in, g_qa, w_uq, g_kva, w_ukv, hg_norm, w_pa, w_pb, w_out, g_mem, w_mk, w_mv, g_x, w_xq,
                   w_xo, g_ffn, w_router, b_router, w_gu, b_gu, w_down, b_down):
    row = lambda v: v.reshape(1, -1).astype(F32)
    wi = w_in[l]
    o_qa = 4 * HG_WIDTH
    o_kv = o_qa + Q_LORA
    o_kr = o_kv + KV_LORA
    o_g = o_kr + QK_ROPE
    w_kr = wi[:, o_kr:o_g]
    zk = jnp.zeros((D_MODEL, LANES - QK_ROPE), F32)
    uq = w_uq[l].reshape(Q_LORA, MLA_HEADS, QK_NOPE + QK_ROPE)
    zq = lambda w: jnp.zeros((Q_LORA, MLA_HEADS, w), F32)
    pad = HEAD_PAD - QK_NOPE - QK_ROPE
    w_q = jnp.concatenate([uq, zq(pad)], axis=-1).reshape(Q_LORA, MLA_HEADS * HEAD_PAD)
    w_qr = jnp.concatenate([zq(QK_NOPE), _rot_cols(uq[..., QK_NOPE:]), zq(pad)], axis=-1).reshape(Q_LORA, MLA_HEADS * HEAD_PAD)
    ukv = w_ukv[l].reshape(KV_LORA, MLA_HEADS, QK_NOPE + V_HEAD)
    w_kn = jnp.concatenate([ukv[..., :QK_NOPE], jnp.zeros((KV_LORA, MLA_HEADS, HEAD_PAD - QK_NOPE), F32)], axis=-1)
    e_head = jnp.concatenate([jnp.zeros((QK_ROPE, QK_NOPE), F32), jnp.eye(QK_ROPE, dtype=F32),
                              jnp.zeros((QK_ROPE, pad), F32)], axis=-1)
    e_kpe = jnp.concatenate([jnp.tile(e_head, (1, MLA_HEADS)), jnp.zeros((LANES - QK_ROPE, MLA_HEADS * HEAD_PAD), F32)], axis=0)
    bf = lambda w: w.astype(BF16)
    return {
        "lb": row(lb[l]), "g_mix": row(g_mix[l]), "w_hg": bf(wi[:, :o_qa]), "w_g": bf(wi[:, o_g:]),
        "w_qa": bf(wi[:, o_qa:o_kv]), "w_kv": bf(wi[:, o_kv:o_kr]),
        "w_kr": bf(jnp.concatenate([w_kr, zk, _rot_cols(w_kr), zk], axis=-1)),
        "g_qa": row(g_qa[l]), "w_q": bf(w_q), "w_qr": bf(w_qr), "g_kva": row(g_kva[l]),
        "w_kn": bf(w_kn.reshape(KV_LORA, MLA_HEADS * HEAD_PAD)), "w_v": bf(ukv[..., QK_NOPE:].reshape(KV_LORA, MLA_WIDTH)),
        "e_kpe": bf(e_kpe), "hg_norm": row(hg_norm[l]), "w_pa": bf(w_pa[l]), "w_pb": bf(w_pb[l]), "w_out": bf(w_out[l]),
        "g_mem": row(g_mem[l]), "w_mk": bf(w_mk[l]), "w_mv": bf(w_mv[l]), "g_x": row(g_x[l]), "w_xq": bf(w_xq[l]),
        "w_xo": bf(w_xo[l]), "g_ffn": row(g_ffn[l]),
        "w_router": bf(jnp.pad(w_router[l], ((0, 0), (0, LANES - N_EXPERTS)))),
        "b_router": jnp.pad(row(b_router[l]), ((0, 0), (0, LANES - N_EXPERTS))),
        "w_gu": bf(w_gu[l]), "b_gu": b_gu[l].reshape(N_EXPERTS, 1, 2 * D_FF).astype(F32),
        "w_down": bf(w_down[l]), "b_down": b_down[l].reshape(N_EXPERTS, 1, D_MODEL).astype(F32),
    }


def _tile(n, want):
    t = min(n, want)
    while n % t:
        t -= 8
    assert t > 0 and n % t == 0, (n, want)
    return t


def _group_front(x, lw, tabs, seq_len, hg_state, past):
    b, l, _ = x.shape
    t = b * l
    tm = _tile(t, TOKEN_TILE)
    hg, sg, q, ckv, kpe = _inproj(x.reshape(t, D_MODEL), lw, tabs, tm)
    lt = _tile(l, 512)
    a, s_new = _hgrn(hg.reshape(b, l, 4 * HG_WIDTH), lw["lb"], lw["hg_norm"], hg_state, lt, _tile(lt, 128))
    tq = _tile(l, ATTN_TILE)
    if past is None:
        lk_true, q_off = l, 0
        ckv_all, kpe_all = ckv, kpe
        lk = l
    else:
        past_ckv, past_kpe = past
        q_off = past_ckv.shape[1]
        lk_true = q_off + l
        lk = -(-lk_true // ATTN_TILE) * ATTN_TILE
        padr = ((0, 0), (0, lk - lk_true), (0, 0))
        ckv_all = jnp.pad(jnp.concatenate([past_ckv, ckv.reshape(b, l, KV_LORA)], axis=1), padr).reshape(b * lk, KV_LORA)
        past_kpe = jnp.pad(past_kpe, ((0, 0), (0, 0), (0, LANES - QK_ROPE)))
        kpe_all = jnp.pad(jnp.concatenate([past_kpe, kpe.reshape(b, l, LANES)], axis=1), padr).reshape(b * lk, LANES)
    k_arr, v_arr = _kvbuild(ckv_all, kpe_all, lw, _tile(b * lk, TOKEN_TILE))
    bo = _attention(q.reshape(b, l, MLA_HEADS * HEAD_PAD), k_arr.reshape(b, lk, MLA_HEADS * HEAD_PAD),
                    v_arr.reshape(b, lk, MLA_WIDTH), lk_true, q_off, tq, _tile(lk, ATTN_TILE))
    return a.reshape(t, HG_WIDTH), bo.reshape(t, MLA_WIDTH), sg, ckv.reshape(b, l, KV_LORA), kpe[:, :QK_ROPE].reshape(b, l, QK_ROPE), s_new


def kernel(x_prompt, x_sample, cache_mla_ckv, cache_mla_kpe, state_hgrn, cache_mem_k, cache_mem_v, mem_prompt, hg_lb_logits, g_mix, w_in, g_qa, w_uq, g_kva, w_ukv, hg_norm, w_pa, w_pb, w_out, g_mem, w_mk, w_mv, g_x, w_xq, w_xo, g_ffn, w_router, b_router, w_gu, b_gu, w_down, b_down, g_final):
    bp, lp, _ = x_prompt.shape
    bs, ls, _ = x_sample.shape
    depth = w_in.shape[0]
    past_len = cache_mla_ckv.shape[2]
    n_mem = mem_prompt.shape[1]
    tp, ts = bp * lp, bs * ls
    t_all = tp + ts
    tm_p, tm_s = _tile(tp, TOKEN_TILE), _tile(ts, TOKEN_TILE)
    assert tp % tm_s == 0
    lb_all = jnp.cumsum(jax.nn.softmax(hg_lb_logits.astype(F32), axis=0), axis=0)
    tabs_p = _rope_tables(jnp.arange(lp, dtype=I32), max(1, tm_p // lp))
    tabs_s = _rope_tables(past_len + jnp.arange(ls, dtype=I32), max(1, tm_s // ls))
    g_fin = g_final.reshape(1, D_MODEL).astype(F32)

    n_asg = t_all * TOP_K
    bm = MOE_ROWS
    n_blocks = (n_asg + N_EXPERTS * (bm - 1) + bm - 1) // bm
    n_rows = n_blocks * bm

    hp, hs = x_prompt.reshape(tp, D_MODEL), x_sample.reshape(ts, D_MODEL)
    outs = [[] for _ in range(8)]
    for l in range(depth):
        lw = _layer_weights(l, lb_all, g_mix, w_in, g_qa, w_uq, g_kva, w_ukv, hg_norm, w_pa, w_pb, w_out, g_mem, w_mk, w_mv,
                            g_x, w_xq, w_xo, g_ffn, w_router, b_router, w_gu, b_gu, w_down, b_down)
        mk, mv = _memkv(mem_prompt.reshape(bp * n_mem, D_MODEL), lw, _tile(bp * n_mem, TOKEN_TILE))
        mk, mv = mk.reshape(bp, n_mem, X_WIDTH), mv.reshape(bp, n_mem, X_WIDTH)
        zero_state = jnp.zeros((bp, HG_HEADS, HG_DK, HG_DV), F32)
        a_p, bo_p, sg_p, ckv_p, kpe_p, st_p = _group_front(hp.reshape(bp, lp, D_MODEL), lw, tabs_p, lp, zero_state, None)
        a_s, bo_s, sg_s, ckv_s, kpe_s, st_s = _group_front(hs.reshape(bs, ls, D_MODEL), lw, tabs_s, ls, state_hgrn[l],
                                                           (cache_mla_ckv[l], cache_mla_kpe[l]))
        h2_p, u3_p, lg_p = _merge(hp, a_p, bo_p, sg_p, mk, mv, lw, tm_p, lp)
        h2_s, u3_s, lg_s = _merge(hs, a_s, bo_s, sg_s, cache_mem_k[l].reshape(bs, n_mem, X_WIDTH),
                                  cache_mem_v[l].reshape(bs, n_mem, X_WIDTH), lw, tm_s, ls)
        u3 = jnp.concatenate([u3_p, u3_s], axis=0)
        idx, gate, rank, cnt = _route(jnp.concatenate([lg_p, lg_s], axis=0), _tile(t_all, TOKEN_TILE))
        counts = cnt[0, :N_EXPERTS].astype(I32)
        padded = (counts + bm - 1) // bm * bm
        pend = jnp.cumsum(padded)
        dest = ((pend - padded)[idx] + rank).reshape(-1)
        block_start = jnp.arange(n_blocks, dtype=I32) * bm
        block_expert = jnp.minimum(jnp.sum((pend[None, :] <= block_start[:, None]).astype(I32), axis=1), N_EXPERTS - 1)
        tok = jnp.broadcast_to(jnp.arange(t_all, dtype=I32)[:, None], (t_all, TOP_K)).reshape(-1)
        slot_row = (jnp.arange(TOP_K, dtype=I32)[None, :] * t_all + jnp.arange(t_all, dtype=I32)[:, None]).reshape(-1)
        row_tok = jnp.zeros((n_rows,), I32).at[dest].set(tok)
        row_dst = (n_asg + jnp.arange(n_rows, dtype=I32) % bm).at[dest].set(slot_row)
        y4 = _moe(u3, block_expert, row_tok, row_dst, lw, bm, n_asg + bm)
        last = l == depth - 1
        gf = g_fin if last else jnp.ones_like(g_fin)
        yp = _final(h2_p, gate[:tp], y4, gf, tm_p, 0, t_all)
        ys = _final(h2_s, gate[tp:], y4, gf, tm_s, tp, t_all)
        assert last, "multi-layer stacking needs the un-normalised residual stream"
        hp, hs = yp, ys
        for lst, v in zip(outs, (ckv_p, kpe_p, st_p, mk.reshape(bp, n_mem, X_HEADS, X_HEAD_DIM),
                                 mv.reshape(bp, n_mem, X_HEADS, X_HEAD_DIM), ckv_s, kpe_s, st_s)):
            lst.append(v)
    stk = [jnp.stack(o) for o in outs]
    return (hp.reshape(bp, lp, D_MODEL), hs.reshape(bs, ls, D_MODEL), stk[0], stk[1], stk[2], stk[3], stk[4], stk[5], stk[6], stk[7])
```

```python
import functools
import math

import jax
import jax.numpy as jnp
from jax import lax
from jax.experimental import pallas as pl
from jax.experimental.pallas import tpu as pltpu

F32 = jnp.float32
BF16 = jnp.bfloat16
I32 = jnp.int32

D_MODEL = 1024
CHUNK = 64
EPS = 1e-6
HG_HEADS = 4
HG_DK = 128
HG_DV = 128
HG_WIDTH = HG_HEADS * HG_DV
HG_SUB = 16
MLA_HEADS = 8
Q_LORA = 384
KV_LORA = 256
QK_NOPE = 64
QK_ROPE = 32
V_HEAD = 64
MLA_WIDTH = MLA_HEADS * V_HEAD
HEAD_PAD = 128
ROPE_THETA = 10000.0
X_HEADS = 4
X_HEAD_DIM = 128
X_WIDTH = X_HEADS * X_HEAD_DIM
N_EXPERTS = 32
TOP_K = 4
D_FF = D_MODEL
SWIGLU_LIMIT = 7.0
SWIGLU_ALPHA = 1.702
LANES = 128
ROW_TILE = D_MODEL // LANES
NEG = -1e30

VMEM_LIMIT = 56 * 1024 * 1024
TOKEN_TILE = 512
ATTN_TILE = 256
MOE_ROWS = 256


def _dot(a, b):
    return jnp.dot(a, b, preferred_element_type=F32)


def _dot_nt(a, b):
    return lax.dot_general(a, b, (((1,), (1,)), ((), ())), preferred_element_type=F32)


def _dot_tn(a, b):
    return lax.dot_general(a, b, (((0,), (0,)), ((), ())), preferred_element_type=F32)


def _rms(x, g):
    return x * lax.rsqrt(jnp.mean(x * x, axis=-1, keepdims=True) + EPS) * g


def _sigmoid(x):
    return 1.0 / (1.0 + jnp.exp(-x))


def _const_spec(shape):
    zeros = (0,) * len(shape)
    return pl.BlockSpec(shape, lambda *_: zeros, pipeline_mode=pl.Buffered(1))


def _params(n_axes):
    return pltpu.CompilerParams(dimension_semantics=("arbitrary",) * n_axes, vmem_limit_bytes=VMEM_LIMIT)


def _inproj_kernel(x_ref, gmix_ref, whg_ref, wg_ref, wqa_ref, wkv_ref, wkr_ref, gqa_ref, wq_ref, wqr_ref, gkva_ref,
                   cosq_ref, sinq_ref, cosk_ref, sink_ref,
                   hg_ref, sg_ref, q_ref, ckv_ref, kpe_ref):
    u = _rms(x_ref[...], gmix_ref[...]).astype(BF16)
    hg_ref[...] = _dot(u, whg_ref[...])
    sg_ref[...] = _sigmoid(_dot(u, wg_ref[...])).astype(BF16)
    qn = _rms(_dot(u, wqa_ref[...]), gqa_ref[...]).astype(BF16)
    cosq = jnp.concatenate([cosq_ref[...]] * MLA_HEADS, axis=-1)
    sinq = jnp.concatenate([sinq_ref[...]] * MLA_HEADS, axis=-1)
    scale = (QK_NOPE + QK_ROPE) ** -0.5
    q = (_dot(qn, wq_ref[...]) * cosq + _dot(qn, wqr_ref[...]) * sinq) * scale
    q_ref[...] = q.astype(BF16)
    ckv_ref[...] = _rms(_dot(u, wkv_ref[...]), gkva_ref[...])
    pk = _dot(u, wkr_ref[...])
    kpe_ref[...] = pk[:, :LANES] * cosk_ref[...] + pk[:, LANES:] * sink_ref[...]


def _inproj(x, lw, tabs, tm):
    t = x.shape[0]
    nt = tabs["cosq"].shape[0] // tm
    tok = lambda i: (i, 0)
    tab = lambda i: (i % nt, 0)
    consts = [lw["g_mix"], lw["w_hg"], lw["w_g"], lw["w_qa"], lw["w_kv"], lw["w_kr"], lw["g_qa"], lw["w_q"], lw["w_qr"],
              lw["g_kva"]]
    in_specs = ([pl.BlockSpec((tm, D_MODEL), tok)] + [_const_spec(c.shape) for c in consts]
                + [pl.BlockSpec((tm, LANES), tab)] * 4)
    out_shape = (jax.ShapeDtypeStruct((t, 4 * HG_WIDTH), F32), jax.ShapeDtypeStruct((t, 2 * D_MODEL), BF16),
                 jax.ShapeDtypeStruct((t, MLA_HEADS * HEAD_PAD), BF16), jax.ShapeDtypeStruct((t, KV_LORA), F32),
                 jax.ShapeDtypeStruct((t, LANES), F32))
    out_specs = tuple(pl.BlockSpec((tm, s.shape[1]), tok) for s in out_shape)
    return pl.pallas_call(
        _inproj_kernel, grid=(t // tm,), in_specs=in_specs, out_specs=out_specs, out_shape=out_shape,
        compiler_params=_params(1), name="inproj",
    )(x, *consts, tabs["cosq"], tabs["sinq"], tabs["cosk"], tabs["sink"])


def _kvbuild_kernel(ckv_ref, kpe_ref, wkn_ref, wv_ref, e_ref, k_ref, v_ref):
    c = ckv_ref[...].astype(BF16)
    k_ref[...] = (_dot(c, wkn_ref[...]) + _dot(kpe_ref[...].astype(BF16), e_ref[...])).astype(BF16)
    v_ref[...] = _dot(c, wv_ref[...]).astype(BF16)


def _kvbuild(ckv, kpe, lw, tm):
    n = ckv.shape[0]
    tok = lambda i: (i, 0)
    consts = [lw["w_kn"], lw["w_v"], lw["e_kpe"]]
    out_shape = (jax.ShapeDtypeStruct((n, MLA_HEADS * HEAD_PAD), BF16), jax.ShapeDtypeStruct((n, MLA_WIDTH), BF16))
    return pl.pallas_call(
        _kvbuild_kernel, grid=(n // tm,),
        in_specs=[pl.BlockSpec((tm, KV_LORA), tok), pl.BlockSpec((tm, LANES), tok)] + [_const_spec(c.shape) for c in consts],
        out_specs=tuple(pl.BlockSpec((tm, s.shape[1]), tok) for s in out_shape), out_shape=out_shape,
        compiler_params=_params(1), name="kvbuild",
    )(ckv, kpe, *consts)


def _attn_kernel(q_ref, k_ref, v_ref, o_ref, m_sc, l_sc, acc_sc, *, tq, tk, lk_true, q_off):
    q0 = q_off + pl.program_id(2) * tq
    lim_first = (q0 // CHUNK + 1) * CHUNK
    lim_last = ((q0 + tq - 1) // CHUNK + 1) * CHUNK
    n_tiles = (jnp.minimum(lim_last, lk_true) + tk - 1) // tk
    n_full = jnp.minimum(lim_first, lk_true) // tk
    m_sc[...] = jnp.full(m_sc.shape, NEG, F32)
    l_sc[...] = jnp.zeros(l_sc.shape, F32)
    acc_sc[...] = jnp.zeros(acc_sc.shape, F32)

    def step(j, masked):
        k0 = pl.multiple_of(j * tk, tk)
        kt = k_ref[pl.ds(k0, tk), :]
        vt = v_ref[pl.ds(k0, tk), :]
        for h in range(2):
            s = _dot_nt(q_ref[:, h * HEAD_PAD:(h + 1) * HEAD_PAD], kt[:, h * HEAD_PAD:(h + 1) * HEAD_PAD])
            if masked:
                kpos = k0 + lax.broadcasted_iota(I32, (tq, tk), 1)
                qpos = q0 + lax.broadcasted_iota(I32, (tq, tk), 0)
                lim = jnp.minimum((qpos // CHUNK + 1) * CHUNK, lk_true)
                s = jnp.where(kpos < lim, s, NEG)
            m_old = m_sc[h]
            m_new = jnp.maximum(m_old, jnp.max(s, axis=-1, keepdims=True))
            alpha = jnp.exp(m_old - m_new)
            p = jnp.exp(s - jnp.concatenate([m_new] * (tk // LANES), axis=-1))
            l_sc[h] = alpha * l_sc[h] + jnp.sum(p, axis=-1, keepdims=True)
            acc_sc[h] = alpha * acc_sc[h] + _dot(p.astype(BF16), vt)
            m_sc[h] = m_new

    def full_step(j, c):
        step(j, False)
        return c

    def masked_step(j, c):
        step(j, True)
        return c

    lax.fori_loop(0, n_full, full_step, 0)
    lax.fori_loop(n_full, n_tiles, masked_step, 0)
    o0 = acc_sc[0] / l_sc[0]
    o1 = acc_sc[1] / l_sc[1]
    lane = lax.broadcasted_iota(I32, o0.shape, 1)
    o_ref[...] = jnp.where(lane < V_HEAD, o0, o1).astype(BF16)


def _attention(q, k, v, lk_true, q_off, tq, tk):
    b, lq, _ = q.shape
    lk = k.shape[1]
    kern = functools.partial(_attn_kernel, tq=tq, tk=tk, lk_true=lk_true, q_off=q_off)
    return pl.pallas_call(
        kern, grid=(b, MLA_HEADS // 2, lq // tq),
        in_specs=[pl.BlockSpec((None, tq, 2 * HEAD_PAD), lambda bi, hp, i: (bi, i, hp)),
                  pl.BlockSpec((None, lk, 2 * HEAD_PAD), lambda bi, hp, i: (bi, 0, hp)),
                  pl.BlockSpec((None, lk, 2 * V_HEAD), lambda bi, hp, i: (bi, 0, hp))],
        out_specs=pl.BlockSpec((None, tq, 2 * V_HEAD), lambda bi, hp, i: (bi, i, hp)),
        out_shape=jax.ShapeDtypeStruct((b, lq, MLA_WIDTH), BF16),
        scratch_shapes=[pltpu.VMEM((2, tq, LANES), F32), pltpu.VMEM((2, tq, LANES), F32), pltpu.VMEM((2, tq, 2 * V_HEAD), F32)],
        compiler_params=_params(3), name="attn",
    )(q, k, v)


def _hgrn_kernel(q_ref, f_ref, i_ref, g_ref, lb_ref, norm_ref, s0_ref, a_ref, sout_ref, st_sc, *, chunk, n_chunks):
    t = pl.program_id(2)

    @pl.when(t == 0)
    def _():
        st_sc[...] = s0_ref[...].T

    lb = lb_ref[...]
    row = lax.broadcasted_iota(I32, (chunk, chunk), 0)
    col = lax.broadcasted_iota(I32, (chunk, chunk), 1)
    tri = jnp.where(row >= col, 1.0, 0.0).astype(BF16)
    nsub = chunk // HG_SUB
    sub_row = lax.broadcasted_iota(I32, (nsub, HG_SUB, 1), 1)

    def body(c, carry):
        r0 = pl.multiple_of(c * chunk, chunk)
        q = q_ref[pl.ds(r0, chunk), :]
        v = i_ref[pl.ds(r0, chunk), :]
        f = lb + (1.0 - lb) * _sigmoid(f_ref[pl.ds(r0, chunk), :])
        lf = jnp.log(f)
        k = 1.0 - f
        hi = lf.astype(BF16)
        r1 = lf - hi.astype(F32)
        mid = r1.astype(BF16)
        lo = (r1 - mid.astype(F32)).astype(BF16)
        b = _dot(tri, hi) + _dot(tri, mid) + _dot(tri, lo)
        st = st_sc[...]
        o = _dot_nt((q * jnp.exp(b)).astype(BF16), st.astype(BF16))
        a_cross = jnp.zeros((chunk, chunk), F32)
        m = chunk // 2
        while m >= HG_SUB:
            blk = 2 * m
            shp = (chunk // blk, blk, HG_DK)
            b3 = b.reshape(shp)
            ref = b3[:, m - 1:m, :]
            first = lax.broadcasted_iota(I32, (chunk // blk, blk, 1), 1) < m
            ql = jnp.where(first, 0.0, q.reshape(shp) * jnp.exp(jnp.minimum(b3 - ref, 0.0)))
            kl = jnp.where(first, k.reshape(shp) * jnp.exp(jnp.minimum(ref - b3, 0.0)), 0.0)
            a_l = _dot_nt(ql.reshape(chunk, HG_DK).astype(BF16), kl.reshape(chunk, HG_DK).astype(BF16))
            a_cross = a_cross + jnp.where(row // blk == col // blk, a_l, 0.0)
            m //= 2
        o = o + _dot(a_cross.astype(BF16), v.astype(BF16))
        shp = (nsub, HG_SUB, HG_DK)
        q3, k3, b3, v3 = q.reshape(shp), k.reshape(shp), b.reshape(shp), v.reshape(shp)
        od = jnp.zeros(shp, F32)
        for s in range(HG_SUB):
            w = q3 * jnp.exp(jnp.minimum(b3 - b3[:, s:s + 1, :], 0.0)) * k3[:, s:s + 1, :]
            a_col = jnp.where(sub_row >= s, jnp.sum(w, axis=-1, keepdims=True), 0.0)
            od = od + a_col * v3[:, s:s + 1, :]
        o = o + od.reshape(chunk, HG_DV)
        b_last = b[chunk - 1:chunk, :]
        kd = k * jnp.exp(b_last - b)
        st_sc[...] = st * jnp.exp(b_last) + _dot_tn(v.astype(BF16), kd.astype(BF16))
        g = g_ref[pl.ds(r0, chunk), :]
        a_ref[pl.ds(r0, chunk), :] = (_rms(o, norm_ref[...]) * (g * _sigmoid(g))).astype(BF16)
        return carry

    lax.fori_loop(0, n_chunks, body, 0)

    @pl.when(t == pl.num_programs(2) - 1)
    def _():
        sout_ref[...] = st_sc[...].T


def _hgrn(hg, lb, norm, s0, lt, chunk):
    b, l, _ = hg.shape
    kern = functools.partial(_hgrn_kernel, chunk=chunk, n_chunks=lt // chunk)
    seg = lambda j: pl.BlockSpec((None, lt, HG_DK), lambda bi, h, t: (bi, t, j * HG_HEADS + h))
    head_vec = pl.BlockSpec((1, HG_DK), lambda bi, h, t: (0, h))
    state = pl.BlockSpec((None, None, HG_DK, HG_DV), lambda bi, h, t: (bi, h, 0, 0))
    return pl.pallas_call(
        kern, grid=(b, HG_HEADS, l // lt),
        in_specs=[seg(0), seg(1), seg(2), seg(3), head_vec, head_vec, state],
        out_specs=(pl.BlockSpec((None, lt, HG_DV), lambda bi, h, t: (bi, t, h)), state),
        out_shape=(jax.ShapeDtypeStruct((b, l, HG_WIDTH), BF16), jax.ShapeDtypeStruct(s0.shape, F32)),
        scratch_shapes=[pltpu.VMEM((HG_DV, HG_DK), F32)],
        compiler_params=_params(3), name="hgrn",
    )(hg, hg, hg, hg, lb, norm, s0)


def _memkv_kernel(m_ref, g_ref, wk_ref, wv_ref, k_ref, v_ref):
    m = _rms(m_ref[...], g_ref[...]).astype(BF16)
    k_ref[...] = _dot(m, wk_ref[...])
    v_ref[...] = _dot(m, wv_ref[...])


def _memkv(mem, lw, tm):
    n = mem.shape[0]
    tok = lambda i: (i, 0)
    consts = [lw["g_mem"], lw["w_mk"], lw["w_mv"]]
    out_shape = (jax.ShapeDtypeStruct((n, X_WIDTH), F32),) * 2
    return pl.pallas_call(
        _memkv_kernel, grid=(n // tm,),
        in_specs=[pl.BlockSpec((tm, D_MODEL), tok)] + [_const_spec(c.shape) for c in consts],
        out_specs=(pl.BlockSpec((tm, X_WIDTH), tok),) * 2, out_shape=out_shape,
        compiler_params=_params(1), name="memkv",
    )(mem, *consts)


def _merge_kernel(x_ref, a_ref, bo_ref, sg_ref, mk_ref, mv_ref, wpa_ref, wpb_ref, wout_ref, gx_ref, wxq_ref, wxo_ref,
                  gffn_ref, wr_ref, br_ref, h_ref, u_ref, lg_ref, *, n_seg, seg):
    sg = sg_ref[...]
    mix = (sg[:, :D_MODEL].astype(F32) * _dot(a_ref[...], wpa_ref[...])
           + sg[:, D_MODEL:].astype(F32) * _dot(bo_ref[...], wpb_ref[...]))
    h1 = x_ref[...] + _dot(mix.astype(BF16), wout_ref[...])
    u2 = _rms(h1, gx_ref[...]).astype(BF16)
    qx = (_dot(u2, wxq_ref[...]) * X_HEAD_DIM ** -0.5).astype(BF16)
    segs = []
    for j in range(n_seg):
        kj = mk_ref[j].astype(BF16)
        vj = mv_ref[j].astype(BF16)
        heads = []
        for h in range(X_HEADS):
            hs = slice(h * X_HEAD_DIM, (h + 1) * X_HEAD_DIM)
            s = _dot_nt(qx[j * seg:(j + 1) * seg, hs], kj[:, hs])
            p = jnp.exp(s - jnp.max(s, axis=-1, keepdims=True))
            heads.append(_dot(p.astype(BF16), vj[:, hs]) / jnp.sum(p, axis=-1, keepdims=True))
        segs.append(jnp.concatenate(heads, axis=-1))
    ox = segs[0] if n_seg == 1 else jnp.concatenate(segs, axis=0)
    h2 = h1 + _dot(ox.astype(BF16), wxo_ref[...])
    h_ref[...] = h2
    u3 = _rms(h2, gffn_ref[...])
    _to_row_tiles(u_ref, u3)
    lg_ref[...] = _dot(u3.astype(BF16), wr_ref[...]) + br_ref[...]


def _merge(x, a, bo, sg, mk, mv, lw, tm, seq_len):
    t = x.shape[0]
    n_mem = mk.shape[1]
    n_seg = max(1, tm // seq_len)
    seg = tm // n_seg
    tiles_per_seq = max(1, seq_len // tm)
    tok = lambda i: (i, 0)
    mem = pl.BlockSpec((n_seg, n_mem, X_WIDTH), lambda i: (i // tiles_per_seq, 0, 0))
    consts = [lw["w_pa"], lw["w_pb"], lw["w_out"], lw["g_x"], lw["w_xq"], lw["w_xo"], lw["g_ffn"], lw["w_router"],
              lw["b_router"]]
    out_shape = (jax.ShapeDtypeStruct((t, D_MODEL), F32), jax.ShapeDtypeStruct((t * ROW_TILE, LANES), F32),
                 jax.ShapeDtypeStruct((t, LANES), F32))
    kern = functools.partial(_merge_kernel, n_seg=n_seg, seg=seg)
    return pl.pallas_call(
        kern, grid=(t // tm,),
        in_specs=[pl.BlockSpec((tm, D_MODEL), tok), pl.BlockSpec((tm, HG_WIDTH), tok), pl.BlockSpec((tm, MLA_WIDTH), tok),
                  pl.BlockSpec((tm, 2 * D_MODEL), tok), mem, mem] + [_const_spec(c.shape) for c in consts],
        out_specs=tuple(pl.BlockSpec((s.shape[0] // (t // tm), s.shape[1]), tok) for s in out_shape), out_shape=out_shape,
        compiler_params=_params(1), name="merge",
    )(x, a, bo, sg, mk, mv, *consts)


def _route_kernel(lg_ref, idx_ref, gate_ref, rank_ref, cnt_ref, carry_sc, *, tr):
    @pl.when(pl.program_id(0) == 0)
    def _():
        carry_sc[...] = jnp.zeros(carry_sc.shape, F32)

    lane = lax.broadcasted_iota(I32, (tr, LANES), 1).astype(F32)
    logit = jnp.where(lane < N_EXPERTS, lg_ref[...], -jnp.inf)
    vals, idxs, hots = [], [], []
    for _ in range(TOP_K):
        top = jnp.max(logit, axis=-1, keepdims=True)
        first = jnp.min(jnp.where(logit == top, lane, float(LANES)), axis=-1, keepdims=True)
        hot = lane == first
        vals.append(top)
        idxs.append(first)
        hots.append(hot)
        logit = jnp.where(hot, -jnp.inf, logit)
    exps = [jnp.exp(v - vals[0]) for v in vals]
    denom = exps[0] + exps[1] + exps[2] + exps[3]
    hot_all = sum(jnp.where(h, 1.0, 0.0) for h in hots)
    row = lax.broadcasted_iota(I32, (tr, tr), 0)
    col = lax.broadcasted_iota(I32, (tr, tr), 1)
    before = jnp.where(row > col, 1.0, 0.0).astype(BF16)
    prefix = _dot(before, hot_all.astype(BF16)) + carry_sc[...]
    lane4 = lax.broadcasted_iota(I32, (tr, TOP_K), 1)
    idx = jnp.zeros((tr, TOP_K), F32)
    gate = jnp.zeros((tr, TOP_K), F32)
    rank = jnp.zeros((tr, TOP_K), F32)
    for k in range(TOP_K):
        rk = jnp.sum(jnp.where(hots[k], prefix, 0.0), axis=-1, keepdims=True)
        idx = jnp.where(lane4 == k, idxs[k], idx)
        gate = jnp.where(lane4 == k, exps[k] / denom, gate)
        rank = jnp.where(lane4 == k, rk, rank)
    idx_ref[...] = idx.astype(I32)
    gate_ref[...] = gate
    rank_ref[...] = rank.astype(I32)
    carry_sc[...] += jnp.sum(hot_all, axis=0, keepdims=True)
    cnt_ref[...] = carry_sc[...]


def _route(logits, tr):
    t = logits.shape[0]
    tok = lambda i: (i, 0)
    out_shape = (jax.ShapeDtypeStruct((t, TOP_K), I32), jax.ShapeDtypeStruct((t, TOP_K), F32),
                 jax.ShapeDtypeStruct((t, TOP_K), I32), jax.ShapeDtypeStruct((1, LANES), F32))
    return pl.pallas_call(
        functools.partial(_route_kernel, tr=tr), grid=(t // tr,),
        in_specs=[pl.BlockSpec((tr, LANES), tok)],
        out_specs=(pl.BlockSpec((tr, TOP_K), tok),) * 3 + (pl.BlockSpec((1, LANES), lambda i: (0, 0)),),
        out_shape=out_shape, scratch_shapes=[pltpu.VMEM((1, LANES), F32)],
        compiler_params=_params(1), name="route",
    )(logits)


def _to_row_tiles(ref, x):
    n = x.shape[0]
    for c in range(ROW_TILE):
        ref[pl.ds(c, n, stride=ROW_TILE), :] = x[:, c * LANES:(c + 1) * LANES]


def _row_tile_chunk(ref, n, c):
    return ref[pl.ds(c, n, stride=ROW_TILE), :]


def _moe_kernel(be_ref, rt0_ref, rtn_ref, rdp_ref, x_hbm, wgu_ref, bgu_ref, wd_ref, bd_ref, y_hbm,
                xs0, xs1, ys0, ys1, gsem, ssem, *, bm, n_blocks):
    del be_ref
    b = pl.program_id(0)
    xs, ys = (xs0, xs1), (ys0, ys1)
    rows = bm * ROW_TILE

    def gather(idx_ref, r, slot):
        src = pl.multiple_of(idx_ref[0, 0, r], ROW_TILE)
        return pltpu.make_async_copy(x_hbm.at[pl.ds(src, ROW_TILE), :], xs[slot].at[pl.ds(r * ROW_TILE, ROW_TILE), :],
                                     gsem.at[slot])

    def scatter(r, slot):
        dst = pl.multiple_of(rdp_ref[0, 0, r], ROW_TILE)
        return pltpu.make_async_copy(ys[slot].at[pl.ds(r * ROW_TILE, ROW_TILE), :], y_hbm.at[pl.ds(dst, ROW_TILE), :],
                                     ssem.at[slot])

    def wait_gathers(slot):
        pltpu.make_async_copy(x_hbm.at[pl.ds(0, rows), :], xs[slot], gsem.at[slot]).wait()

    def wait_scatters(slot):
        pltpu.make_async_copy(ys[slot], y_hbm.at[pl.ds(0, rows), :], ssem.at[slot]).wait()

    @pl.when(b == 0)
    def _():
        ys1[...] = jnp.zeros(ys1.shape, F32)

        def start(r, c):
            gather(rt0_ref, r, 0).start()
            return c

        lax.fori_loop(0, bm, start, 0)

    def step(slot):
        other = 1 - slot
        wait_gathers(slot)

        @pl.when(b >= 1)
        def _():
            wait_scatters(slot)

        for r in range(bm):
            scatter(r, other).start()
            gather(rtn_ref, r, other).start()
        x = jnp.concatenate([_row_tile_chunk(xs[slot], bm, c) for c in range(ROW_TILE)], axis=-1).astype(BF16)
        gu = _dot(x, wgu_ref[...]) + bgu_ref[...]
        g = jnp.minimum(gu[:, :D_FF], SWIGLU_LIMIT)
        up = jnp.clip(gu[:, D_FF:], -SWIGLU_LIMIT, SWIGLU_LIMIT)
        act = (up + 1.0) * g * _sigmoid(SWIGLU_ALPHA * g)
        _to_row_tiles(ys[slot], _dot(act.astype(BF16), wd_ref[...]) + bd_ref[...])

    @pl.when(jnp.logical_and(b < n_blocks, b % 2 == 0))
    def _():
        step(0)

    @pl.when(jnp.logical_and(b < n_blocks, b % 2 == 1))
    def _():
        step(1)

    @pl.when(b == n_blocks)
    def _():
        last = (n_blocks - 1) % 2
        wait_gathers(1 - last)
        wait_scatters(1 - last)

        def start(r, c):
            scatter(r, last).start()
            return c

        lax.fori_loop(0, bm, start, 0)
        wait_scatters(last)


def _moe(x, block_expert, row_src, row_dst, lw, bm, n_out_rows):
    n_blocks = block_expert.shape[0]
    spare = (n_out_rows // ROW_TILE - bm + jnp.arange(bm, dtype=I32)) * ROW_TILE
    src = row_src.reshape(n_blocks, 1, bm)
    dst_prev = jnp.concatenate([spare.reshape(1, 1, bm), row_dst.reshape(n_blocks, 1, bm)], axis=0)
    smem = lambda f: pl.BlockSpec((1, 1, bm), f, memory_space=pltpu.SMEM)
    blk = lambda b: jnp.minimum(b, n_blocks - 1)
    grid_spec = pltpu.PrefetchScalarGridSpec(
        num_scalar_prefetch=1, grid=(n_blocks + 1,),
        in_specs=[smem(lambda b, be: (0, 0, 0)), smem(lambda b, be: (blk(b + 1), 0, 0)), smem(lambda b, be: (b, 0, 0)),
                  pl.BlockSpec(memory_space=pl.ANY),
                  pl.BlockSpec((None, D_MODEL, 2 * D_FF), lambda b, be: (be[blk(b)], 0, 0)),
                  pl.BlockSpec((None, 1, 2 * D_FF), lambda b, be: (be[blk(b)], 0, 0)),
                  pl.BlockSpec((None, D_FF, D_MODEL), lambda b, be: (be[blk(b)], 0, 0)),
                  pl.BlockSpec((None, 1, D_MODEL), lambda b, be: (be[blk(b)], 0, 0))],
        out_specs=pl.BlockSpec(memory_space=pl.ANY),
        scratch_shapes=[pltpu.VMEM((bm * ROW_TILE, LANES), F32)] * 4 + [pltpu.SemaphoreType.DMA((2,))] * 2)
    return pl.pallas_call(
        functools.partial(_moe_kernel, bm=bm, n_blocks=n_blocks), grid_spec=grid_spec,
        out_shape=jax.ShapeDtypeStruct((n_out_rows, LANES), F32),
        compiler_params=_params(1), name="moe",
    )(block_expert, src, src, dst_prev, x, lw["w_gu"], lw["b_gu"], lw["w_down"], lw["b_down"])


def _final_kernel(h_ref, gate_ref, y0_ref, y1_ref, y2_ref, y3_ref, g_ref, o_ref, *, tm):
    gate = gate_ref[...]
    gates = [jnp.broadcast_to(gate[:, k:k + 1], (tm, LANES)) for k in range(TOP_K)]
    ssq = jnp.zeros((tm, 1), F32)
    for c in range(ROW_TILE):
        cs = slice(c * LANES, (c + 1) * LANES)
        h = h_ref[:, cs]
        for k, y_ref in enumerate((y0_ref, y1_ref, y2_ref, y3_ref)):
            h = h + gates[k] * _row_tile_chunk(y_ref, tm, c)
        o_ref[:, cs] = h
        ssq = ssq + jnp.sum(h * h, axis=-1, keepdims=True)
    o_ref[...] = o_ref[...] * lax.rsqrt(ssq * (1.0 / D_MODEL) + EPS) * g_ref[...]


def _final(h, gate, y4, g_final, tm, tok_off, t_all):
    t = h.shape[0]
    tok = lambda i: (i, 0)
    slot = lambda k: pl.BlockSpec((tm * ROW_TILE, LANES), lambda i: ((k * t_all + tok_off) // tm + i, 0))
    return pl.pallas_call(
        functools.partial(_final_kernel, tm=tm), grid=(t // tm,),
        in_specs=[pl.BlockSpec((tm, D_MODEL), tok), pl.BlockSpec((tm, TOP_K), tok), slot(0), slot(1), slot(2), slot(3),
                  _const_spec(g_final.shape)],
        out_specs=pl.BlockSpec((tm, D_MODEL), tok), out_shape=jax.ShapeDtypeStruct((t, D_MODEL), F32),
        compiler_params=_params(1), name="final",
    )(h, gate, y4, y4, y4, y4, g_final)


def _rope_tables(pos, reps):
    half = QK_ROPE // 2
    inv_freq = jnp.exp(-math.log(ROPE_THETA) * jnp.arange(half, dtype=F32) / half)
    ang = pos.astype(F32)[:, None] * inv_freq[None, :]
    cos = jnp.concatenate([jnp.cos(ang)] * 2, axis=-1)
    sin = jnp.concatenate([jnp.sin(ang)] * 2, axis=-1)
    n = pos.shape[0]
    ones = jnp.ones((n, QK_NOPE), F32)
    z = lambda w: jnp.zeros((n, w), F32)
    tabs = {"cosq": jnp.concatenate([ones, cos, z(HEAD_PAD - QK_NOPE - QK_ROPE)], axis=-1),
            "sinq": jnp.concatenate([z(QK_NOPE), sin, z(HEAD_PAD - QK_NOPE - QK_ROPE)], axis=-1),
            "cosk": jnp.concatenate([cos, z(LANES - QK_ROPE)], axis=-1),
            "sink": jnp.concatenate([sin, z(LANES - QK_ROPE)], axis=-1)}
    return {k: jnp.tile(v, (reps, 1)) for k, v in tabs.items()}


def _rot_cols(w):
    half = QK_ROPE // 2
    return jnp.concatenate([-w[..., half:], w[..., :half]], axis=-1)


def _layer_weights(l, lb, g_mix, w_in, g_qa, w_uq, g_kva, w_ukv, hg_norm, w_pa, w_pb, w_out, g_mem, w_mk, w_mv, g_x, w_xq,
                   w_xo, g_ffn, w_router, b_router, w_gu, b_gu, w_down, b_down):
    row = lambda v: v.reshape(1, -1).astype(F32)
    wi = w_in[l]
    o_qa = 4 * HG_WIDTH
    o_kv = o_qa + Q_LORA
    o_kr = o_kv + KV_LORA
    o_g = o_kr + QK_ROPE
    w_kr = wi[:, o_kr:o_g]
    zk = jnp.zeros((D_MODEL, LANES - QK_ROPE), F32)
    uq = w_uq[l].reshape(Q_LORA, MLA_HEADS, QK_NOPE + QK_ROPE)
    zq = lambda w: jnp.zeros((Q_LORA, MLA_HEADS, w), F32)
    pad = HEAD_PAD - QK_NOPE - QK_ROPE
    w_q = jnp.concatenate([uq, zq(pad)], axis=-1).reshape(Q_LORA, MLA_HEADS * HEAD_PAD)
    w_qr = jnp.concatenate([zq(QK_NOPE), _rot_cols(uq[..., QK_NOPE:]), zq(pad)], axis=-1).reshape(Q_LORA, MLA_HEADS * HEAD_PAD)
    ukv = w_ukv[l].reshape(KV_LORA, MLA_HEADS, QK_NOPE + V_HEAD)
    w_kn = jnp.concatenate([ukv[..., :QK_NOPE], jnp.zeros((KV_LORA, MLA_HEADS, HEAD_PAD - QK_NOPE), F32)], axis=-1)
    e_head = jnp.concatenate([jnp.zeros((QK_ROPE, QK_NOPE), F32), jnp.eye(QK_ROPE, dtype=F32),
                              jnp.zeros((QK_ROPE, pad), F32)], axis=-1)
    e_kpe = jnp.concatenate([jnp.tile(e_head, (1, MLA_HEADS)), jnp.zeros((LANES - QK_ROPE, MLA_HEADS * HEAD_PAD), F32)], axis=0)
    bf = lambda w: w.astype(BF16)
    return {
        "lb": row(lb[l]), "g_mix": row(g_mix[l]), "w_hg": bf(wi[:, :o_qa]), "w_g": bf(wi[:, o_g:]),
        "w_qa": bf(wi[:, o_qa:o_kv]), "w_kv": bf(wi[:, o_kv:o_kr]),
        "w_kr": bf(jnp.concatenate([w_kr, zk, _rot_cols(w_kr), zk], axis=-1)),
        "g_qa": row(g_qa[l]), "w_q": bf(w_q), "w_qr": bf(w_qr), "g_kva": row(g_kva[l]),
        "w_kn": bf(w_kn.reshape(KV_LORA, MLA_HEADS * HEAD_PAD)), "w_v": bf(ukv[..., QK_NOPE:].reshape(KV_LORA, MLA_WIDTH)),
        "e_kpe": bf(e_kpe), "hg_norm": row(hg_norm[l]), "w_pa": bf(w_pa[l]), "w_pb": bf(w_pb[l]), "w_out": bf(w_out[l]),
        "g_mem": row(g_mem[l]), "w_mk": bf(w_mk[l]), "w_mv": bf(w_mv[l]), "g_x": row(g_x[l]), "w_xq": bf(w_xq[l]),
        "w_xo": bf(w_xo[l]), "g_ffn": row(g_ffn[l]),
        "w_router": bf(jnp.pad(w_router[l], ((0, 0), (0, LANES - N_EXPERTS)))),
        "b_router": jnp.pad(row(b_router[l]), ((0, 0), (0, LANES - N_EXPERTS))),
        "w_gu": bf(w_gu[l]), "b_gu": b_gu[l].reshape(N_EXPERTS, 1, 2 * D_FF).astype(F32),
        "w_down": bf(w_down[l]), "b_down": b_down[l].reshape(N_EXPERTS, 1, D_MODEL).astype(F32),
    }


def _tile(n, want):
    t = min(n, want)
    while n % t:
        t -= 8
    assert t > 0 and n % t == 0, (n, want)
    return t


def _group_front(x, lw, tabs, seq_len, hg_state, past):
    b, l, _ = x.shape
    t = b * l
    tm = _tile(t, TOKEN_TILE)
    hg, sg, q, ckv, kpe = _inproj(x.reshape(t, D_MODEL), lw, tabs, tm)
    lt = _tile(l, 512)
    a, s_new = _hgrn(hg.reshape(b, l, 4 * HG_WIDTH), lw["lb"], lw["hg_norm"], hg_state, lt, _tile(lt, 128))
    tq = _tile(l, ATTN_TILE)
    if past is None:
        lk_true, q_off = l, 0
        ckv_all, kpe_all = ckv, kpe
        lk = l
    else:
        past_ckv, past_kpe = past
        q_off = past_ckv.shape[1]
        lk_true = q_off + l
        lk = -(-lk_true // ATTN_TILE) * ATTN_TILE
        padr = ((0, 0), (0, lk - lk_true), (0, 0))
        ckv_all = jnp.pad(jnp.concatenate([past_ckv, ckv.reshape(b, l, KV_LORA)], axis=1), padr).reshape(b * lk, KV_LORA)
        past_kpe = jnp.pad(past_kpe, ((0, 0), (0, 0), (0, LANES - QK_ROPE)))
        kpe_all = jnp.pad(jnp.concatenate([past_kpe, kpe.reshape(b, l, LANES)], axis=1), padr).reshape(b * lk, LANES)
    k_arr, v_arr = _kvbuild(ckv_all, kpe_all, lw, _tile(b * lk, TOKEN_TILE))
    bo = _attention(q.reshape(b, l, MLA_HEADS * HEAD_PAD), k_arr.reshape(b, lk, MLA_HEADS * HEAD_PAD),
                    v_arr.reshape(b, lk, MLA_WIDTH), lk_true, q_off, tq, _tile(lk, ATTN_TILE))
    return a.reshape(t, HG_WIDTH), bo.reshape(t, MLA_WIDTH), sg, ckv.reshape(b, l, KV_LORA), kpe[:, :QK_ROPE].reshape(b, l, QK_ROPE), s_new


def kernel(x_prompt, x_sample, cache_mla_ckv, cache_mla_kpe, state_hgrn, cache_mem_k, cache_mem_v, mem_prompt, hg_lb_logits, g_mix, w_in, g_qa, w_uq, g_kva, w_ukv, hg_norm, w_pa, w_pb, w_out, g_mem, w_mk, w_mv, g_x, w_xq, w_xo, g_ffn, w_router, b_router, w_gu, b_gu, w_down, b_down, g_final):
    bp, lp, _ = x_prompt.shape
    bs, ls, _ = x_sample.shape
    depth = w_in.shape[0]
    past_len = cache_mla_ckv.shape[2]
    n_mem = mem_prompt.shape[1]
    tp, ts = bp * lp, bs * ls
    t_all = tp + ts
    tm_p, tm_s = _tile(tp, TOKEN_TILE), _tile(ts, TOKEN_TILE)
    assert tp % tm_s == 0
    lb_all = jnp.cumsum(jax.nn.softmax(hg_lb_logits.astype(F32), axis=0), axis=0)
    tabs_p = _rope_tables(jnp.arange(lp, dtype=I32), max(1, tm_p // lp))
    tabs_s = _rope_tables(past_len + jnp.arange(ls, dtype=I32), max(1, tm_s // ls))
    g_fin = g_final.reshape(1, D_MODEL).astype(F32)

    n_asg = t_all * TOP_K
    bm = MOE_ROWS
    n_blocks = (n_asg + N_EXPERTS * (bm - 1) + bm - 1) // bm
    n_rows = n_blocks * bm

    hp, hs = x_prompt.reshape(tp, D_MODEL), x_sample.reshape(ts, D_MODEL)
    outs = [[] for _ in range(8)]
    for l in range(depth):
        lw = _layer_weights(l, lb_all, g_mix, w_in, g_qa, w_uq, g_kva, w_ukv, hg_norm, w_pa, w_pb, w_out, g_mem, w_mk, w_mv,
                            g_x, w_xq, w_xo, g_ffn, w_router, b_router, w_gu, b_gu, w_down, b_down)
        mk, mv = _memkv(mem_prompt.reshape(bp * n_mem, D_MODEL), lw, _tile(bp * n_mem, TOKEN_TILE))
        mk, mv = mk.reshape(bp, n_mem, X_WIDTH), mv.reshape(bp, n_mem, X_WIDTH)
        zero_state = jnp.zeros((bp, HG_HEADS, HG_DK, HG_DV), F32)
        a_p, bo_p, sg_p, ckv_p, kpe_p, st_p = _group_front(hp.reshape(bp, lp, D_MODEL), lw, tabs_p, lp, zero_state, None)
        a_s, bo_s, sg_s, ckv_s, kpe_s, st_s = _group_front(hs.reshape(bs, ls, D_MODEL), lw, tabs_s, ls, state_hgrn[l],
                                                           (cache_mla_ckv[l], cache_mla_kpe[l]))
        h2_p, u3_p, lg_p = _merge(hp, a_p, bo_p, sg_p, mk, mv, lw, tm_p, lp)
        h2_s, u3_s, lg_s = _merge(hs, a_s, bo_s, sg_s, cache_mem_k[l].reshape(bs, n_mem, X_WIDTH),
                                  cache_mem_v[l].reshape(bs, n_mem, X_WIDTH), lw, tm_s, ls)
        u3 = jnp.concatenate([u3_p, u3_s], axis=0)
        idx, gate, rank, cnt = _route(jnp.concatenate([lg_p, lg_s], axis=0), _tile(t_all, TOKEN_TILE))
        counts = cnt[0, :N_EXPERTS].astype(I32)
        padded = (counts + bm - 1) // bm * bm
        pend = jnp.cumsum(padded)
        dest = ((pend - padded)[idx] + rank).reshape(-1)
        block_start = jnp.arange(n_blocks, dtype=I32) * bm
        block_expert = jnp.minimum(jnp.sum((pend[None, :] <= block_start[:, None]).astype(I32), axis=1), N_EXPERTS - 1)
        row_asg = jnp.full((n_rows,), -1, I32).at[dest].set(jnp.arange(n_asg, dtype=I32))
        row_t, row_k = row_asg // TOP_K, row_asg % TOP_K
        pad_row = row_asg < 0
        row_src = jnp.where(pad_row, 0, row_t) * ROW_TILE
        row_dst = jnp.where(pad_row, n_asg + jnp.arange(n_rows, dtype=I32) % bm, row_k * t_all + row_t) * ROW_TILE
        y4 = _moe(u3, block_expert, row_src, row_dst, lw, bm, (n_asg + bm) * ROW_TILE)
        last = l == depth - 1
        gf = g_fin if last else jnp.ones_like(g_fin)
        yp = _final(h2_p, gate[:tp], y4, gf, tm_p, 0, t_all)
        ys = _final(h2_s, gate[tp:], y4, gf, tm_s, tp, t_all)
        assert last, "multi-layer stacking needs the un-normalised residual stream"
        hp, hs = yp, ys
        for lst, v in zip(outs, (ckv_p, kpe_p, st_p, mk.reshape(bp, n_mem, X_HEADS, X_HEAD_DIM),
                                 mv.reshape(bp, n_mem, X_HEADS, X_HEAD_DIM), ckv_s, kpe_s, st_s)):
            lst.append(v)
    stk = [jnp.stack(o) for o in outs]
    return (hp.reshape(bp, lp, D_MODEL), hs.reshape(bs, ls, D_MODEL), stk[0], stk[1], stk[2], stk[3], stk[4], stk[5], stk[6], stk[7])
```

```python
import functools
import math

import jax
import jax.numpy as jnp
from jax import lax
from jax.experimental import pallas as pl
from jax.experimental.pallas import tpu as pltpu

F32 = jnp.float32
BF16 = jnp.bfloat16
I32 = jnp.int32

D_MODEL = 1024
CHUNK = 64
EPS = 1e-6
HG_HEADS = 4
HG_DK = 128
HG_DV = 128
HG_WIDTH = HG_HEADS * HG_DV
HG_SUB = 8
MLA_HEADS = 8
Q_LORA = 384
KV_LORA = 256
QK_NOPE = 64
QK_ROPE = 32
V_HEAD = 64
MLA_WIDTH = MLA_HEADS * V_HEAD
HEAD_PAD = 128
ROPE_THETA = 10000.0
X_HEADS = 4
X_HEAD_DIM = 128
X_WIDTH = X_HEADS * X_HEAD_DIM
N_EXPERTS = 32
TOP_K = 4
D_FF = D_MODEL
SWIGLU_LIMIT = 7.0
SWIGLU_ALPHA = 1.702
LANES = 128
ROW_TILE = D_MODEL // LANES
NEG = -1e30

VMEM_LIMIT = 56 * 1024 * 1024
TOKEN_TILE = 512
ATTN_TILE = 256
MOE_ROWS = 256


def _dot(a, b):
    return jnp.dot(a, b, preferred_element_type=F32)


def _dot_nt(a, b):
    return lax.dot_general(a, b, (((1,), (1,)), ((), ())), preferred_element_type=F32)


def _dot_tn(a, b):
    return lax.dot_general(a, b, (((0,), (0,)), ((), ())), preferred_element_type=F32)


def _rms(x, g):
    return x * lax.rsqrt(jnp.mean(x * x, axis=-1, keepdims=True) + EPS) * g


def _sigmoid(x):
    return 1.0 / (1.0 + jnp.exp(-x))


def _const_spec(shape):
    zeros = (0,) * len(shape)
    return pl.BlockSpec(shape, lambda *_: zeros, pipeline_mode=pl.Buffered(1))


def _params(n_axes):
    return pltpu.CompilerParams(dimension_semantics=("arbitrary",) * n_axes, vmem_limit_bytes=VMEM_LIMIT)


def _inproj_kernel(x_ref, gmix_ref, whg_ref, wg_ref, wqa_ref, wkv_ref, wkr_ref, gqa_ref, wq_ref, wqr_ref, gkva_ref,
                   cosq_ref, sinq_ref, cosk_ref, sink_ref,
                   hg_ref, sg_ref, q_ref, ckv_ref, kpe_ref):
    u = _rms(x_ref[...], gmix_ref[...]).astype(BF16)
    hg_ref[...] = _dot(u, whg_ref[...])
    sg_ref[...] = _sigmoid(_dot(u, wg_ref[...])).astype(BF16)
    qn = _rms(_dot(u, wqa_ref[...]), gqa_ref[...]).astype(BF16)
    cosq = jnp.concatenate([cosq_ref[...]] * MLA_HEADS, axis=-1)
    sinq = jnp.concatenate([sinq_ref[...]] * MLA_HEADS, axis=-1)
    scale = (QK_NOPE + QK_ROPE) ** -0.5
    q = (_dot(qn, wq_ref[...]) * cosq + _dot(qn, wqr_ref[...]) * sinq) * scale
    q_ref[...] = q.astype(BF16)
    ckv_ref[...] = _rms(_dot(u, wkv_ref[...]), gkva_ref[...])
    pk = _dot(u, wkr_ref[...])
    kpe_ref[...] = pk[:, :LANES] * cosk_ref[...] + pk[:, LANES:] * sink_ref[...]


def _inproj(x, lw, tabs, tm):
    t = x.shape[0]
    nt = tabs["cosq"].shape[0] // tm
    tok = lambda i: (i, 0)
    tab = lambda i: (i % nt, 0)
    consts = [lw["g_mix"], lw["w_hg"], lw["w_g"], lw["w_qa"], lw["w_kv"], lw["w_kr"], lw["g_qa"], lw["w_q"], lw["w_qr"],
              lw["g_kva"]]
    in_specs = ([pl.BlockSpec((tm, D_MODEL), tok)] + [_const_spec(c.shape) for c in consts]
                + [pl.BlockSpec((tm, LANES), tab)] * 4)
    out_shape = (jax.ShapeDtypeStruct((t, 4 * HG_WIDTH), F32), jax.ShapeDtypeStruct((t, 2 * D_MODEL), BF16),
                 jax.ShapeDtypeStruct((t, MLA_HEADS * HEAD_PAD), BF16), jax.ShapeDtypeStruct((t, KV_LORA), F32),
                 jax.ShapeDtypeStruct((t, LANES), F32))
    out_specs = tuple(pl.BlockSpec((tm, s.shape[1]), tok) for s in out_shape)
    return pl.pallas_call(
        _inproj_kernel, grid=(t // tm,), in_specs=in_specs, out_specs=out_specs, out_shape=out_shape,
        compiler_params=_params(1), name="inproj",
    )(x, *consts, tabs["cosq"], tabs["sinq"], tabs["cosk"], tabs["sink"])


def _kvbuild_kernel(ckv_ref, kpe_ref, wkn_ref, wv_ref, e_ref, k_ref, v_ref):
    c = ckv_ref[...].astype(BF16)
    k_ref[...] = (_dot(c, wkn_ref[...]) + _dot(kpe_ref[...].astype(BF16), e_ref[...])).astype(BF16)
    v_ref[...] = _dot(c, wv_ref[...]).astype(BF16)


def _kvbuild(ckv, kpe, lw, tm):
    n = ckv.shape[0]
    tok = lambda i: (i, 0)
    consts = [lw["w_kn"], lw["w_v"], lw["e_kpe"]]
    out_shape = (jax.ShapeDtypeStruct((n, MLA_HEADS * HEAD_PAD), BF16), jax.ShapeDtypeStruct((n, MLA_WIDTH), BF16))
    return pl.pallas_call(
        _kvbuild_kernel, grid=(n // tm,),
        in_specs=[pl.BlockSpec((tm, KV_LORA), tok), pl.BlockSpec((tm, LANES), tok)] + [_const_spec(c.shape) for c in consts],
        out_specs=tuple(pl.BlockSpec((tm, s.shape[1]), tok) for s in out_shape), out_shape=out_shape,
        compiler_params=_params(1), name="kvbuild",
    )(ckv, kpe, *consts)


def _attn_tile(q_ref, k_ref, v_ref, o_ref, *, q0, tq, tk, lk_true):
    lim_first = min((q0 // CHUNK + 1) * CHUNK, lk_true)
    lim_last = min(((q0 + tq - 1) // CHUNK + 1) * CHUNK, lk_true)
    n_keys = -(-lim_last // tk) * tk
    n_open = lim_first // tk * tk
    outs = []
    for h in range(2):
        hs = slice(h * HEAD_PAD, (h + 1) * HEAD_PAD)
        s = _dot_nt(q_ref[:, hs], k_ref[0:n_keys, hs])
        if n_open < n_keys:
            edge = s[:, n_open:]
            kpos = n_open + lax.broadcasted_iota(I32, edge.shape, 1)
            qpos = q0 + lax.broadcasted_iota(I32, edge.shape, 0)
            edge = jnp.where(kpos < jnp.minimum((qpos // CHUNK + 1) * CHUNK, lk_true), edge, NEG)
            s = edge if n_open == 0 else jnp.concatenate([s[:, :n_open], edge], axis=-1)
        p = jnp.exp(s - jnp.max(s, axis=-1, keepdims=True))
        outs.append(_dot(p.astype(BF16), v_ref[0:n_keys, :]) / jnp.sum(p, axis=-1, keepdims=True))
    lane = lax.broadcasted_iota(I32, outs[0].shape, 1)
    o_ref[...] = jnp.where(lane < V_HEAD, outs[0], outs[1]).astype(BF16)


def _attn_kernel(q_ref, k_ref, v_ref, o_ref, *, n_q, tq, tk, lk_true, q_off):
    i = pl.program_id(2)
    for qi in range(n_q):
        @pl.when(i == qi)
        def _(qi=qi):
            _attn_tile(q_ref, k_ref, v_ref, o_ref, q0=q_off + qi * tq, tq=tq, tk=tk, lk_true=lk_true)


def _attention(q, k, v, lk_true, q_off, tq, tk):
    b, lq, _ = q.shape
    lk = k.shape[1]
    kern = functools.partial(_attn_kernel, n_q=lq // tq, tq=tq, tk=tk, lk_true=lk_true, q_off=q_off)
    return pl.pallas_call(
        kern, grid=(b, MLA_HEADS // 2, lq // tq),
        in_specs=[pl.BlockSpec((None, tq, 2 * HEAD_PAD), lambda bi, hp, i: (bi, i, hp)),
                  pl.BlockSpec((None, lk, 2 * HEAD_PAD), lambda bi, hp, i: (bi, 0, hp)),
                  pl.BlockSpec((None, lk, 2 * V_HEAD), lambda bi, hp, i: (bi, 0, hp))],
        out_specs=pl.BlockSpec((None, tq, 2 * V_HEAD), lambda bi, hp, i: (bi, i, hp)),
        out_shape=jax.ShapeDtypeStruct((b, lq, MLA_WIDTH), BF16),
        compiler_params=_params(3), name="attn",
    )(q, k, v)


def _hgrn_kernel(q_ref, f_ref, i_ref, g_ref, lb_ref, norm_ref, s0_ref, a_ref, sout_ref, st_sc, *, chunk, n_chunks):
    t = pl.program_id(1)

    @pl.when(t == 0)
    def _():
        for h in range(HG_HEADS):
            st_sc[h] = s0_ref[h].T

    row = lax.broadcasted_iota(I32, (chunk, chunk), 0)
    col = lax.broadcasted_iota(I32, (chunk, chunk), 1)
    tri = jnp.where(row >= col, 1.0, 0.0).astype(BF16)
    nsub = chunk // HG_SUB
    sub_row = lax.broadcasted_iota(I32, (nsub, HG_SUB, 1), 1)

    def head_chunk(r0, h):
        hs = slice(h * HG_DK, (h + 1) * HG_DK)
        lb = lb_ref[:, hs]
        q = q_ref[pl.ds(r0, chunk), hs]
        v = i_ref[pl.ds(r0, chunk), hs]
        f = lb + (1.0 - lb) * _sigmoid(f_ref[pl.ds(r0, chunk), hs])
        lf = jnp.log(f)
        k = 1.0 - f
        hi = lf.astype(BF16)
        r1 = lf - hi.astype(F32)
        mid = r1.astype(BF16)
        lo = (r1 - mid.astype(F32)).astype(BF16)
        b = _dot(tri, hi) + _dot(tri, mid) + _dot(tri, lo)
        st = st_sc[h]
        o = _dot_nt((q * jnp.exp(b)).astype(BF16), st.astype(BF16))
        a_cross = jnp.zeros((chunk, chunk), F32)
        m = chunk // 2
        while m >= HG_SUB:
            blk = 2 * m
            shp = (chunk // blk, blk, HG_DK)
            b3 = b.reshape(shp)
            ref = b3[:, m - 1:m, :]
            first = lax.broadcasted_iota(I32, (chunk // blk, blk, 1), 1) < m
            decay = jnp.exp(-jnp.abs(b3 - ref))
            ql = jnp.where(first, 0.0, q.reshape(shp) * decay)
            kl = jnp.where(first, k.reshape(shp) * decay, 0.0)
            a_l = _dot_nt(ql.reshape(chunk, HG_DK).astype(BF16), kl.reshape(chunk, HG_DK).astype(BF16))
            a_cross = a_cross + jnp.where(row // blk == col // blk, a_l, 0.0)
            m //= 2
        o = o + _dot(a_cross.astype(BF16), v.astype(BF16))
        shp = (nsub, HG_SUB, HG_DK)
        q3, k3, b3, v3 = q.reshape(shp), k.reshape(shp), b.reshape(shp), v.reshape(shp)
        od = jnp.zeros(shp, F32)
        for s in range(HG_SUB):
            w = q3 * jnp.exp(b3 - b3[:, s:s + 1, :]) * k3[:, s:s + 1, :]
            a_col = jnp.where(sub_row >= s, jnp.sum(w, axis=-1, keepdims=True), 0.0)
            od = od + a_col * v3[:, s:s + 1, :]
        o = o + od.reshape(chunk, HG_DV)
        b_last = b[chunk - 1:chunk, :]
        kd = k * jnp.exp(b_last - b)
        st_sc[h] = st * jnp.exp(b_last) + _dot_tn(v.astype(BF16), kd.astype(BF16))
        g = g_ref[pl.ds(r0, chunk), hs]
        a_ref[pl.ds(r0, chunk), hs] = (_rms(o, norm_ref[:, hs]) * (g * _sigmoid(g))).astype(BF16)

    def body(c, carry):
        r0 = pl.multiple_of(c * chunk, chunk)
        for h in range(HG_HEADS):
            head_chunk(r0, h)
        return carry

    lax.fori_loop(0, n_chunks, body, 0)

    @pl.when(t == pl.num_programs(1) - 1)
    def _():
        for h in range(HG_HEADS):
            sout_ref[h] = st_sc[h].T


def _hgrn(hg, lb, norm, s0, lt, chunk):
    b, l, _ = hg.shape
    kern = functools.partial(_hgrn_kernel, chunk=chunk, n_chunks=lt // chunk)
    seg = lambda j: pl.BlockSpec((None, lt, HG_WIDTH), lambda bi, t: (bi, t, j))
    head_vec = _const_spec((1, HG_WIDTH))
    state = pl.BlockSpec((None, HG_HEADS, HG_DK, HG_DV), lambda bi, t: (bi, 0, 0, 0))
    return pl.pallas_call(
        kern, grid=(b, l // lt),
        in_specs=[seg(0), seg(1), seg(2), seg(3), head_vec, head_vec, state],
        out_specs=(pl.BlockSpec((None, lt, HG_WIDTH), lambda bi, t: (bi, t, 0)), state),
        out_shape=(jax.ShapeDtypeStruct((b, l, HG_WIDTH), BF16), jax.ShapeDtypeStruct(s0.shape, F32)),
        scratch_shapes=[pltpu.VMEM((HG_HEADS, HG_DV, HG_DK), F32)],
        compiler_params=_params(2), name="hgrn",
    )(hg, hg, hg, hg, lb, norm, s0)


def _memkv_kernel(m_ref, g_ref, wk_ref, wv_ref, k_ref, v_ref):
    m = _rms(m_ref[...], g_ref[...]).astype(BF16)
    k_ref[...] = _dot(m, wk_ref[...])
    v_ref[...] = _dot(m, wv_ref[...])


def _memkv(mem, lw, tm):
    n = mem.shape[0]
    tok = lambda i: (i, 0)
    consts = [lw["g_mem"], lw["w_mk"], lw["w_mv"]]
    out_shape = (jax.ShapeDtypeStruct((n, X_WIDTH), F32),) * 2
    return pl.pallas_call(
        _memkv_kernel, grid=(n // tm,),
        in_specs=[pl.BlockSpec((tm, D_MODEL), tok)] + [_const_spec(c.shape) for c in consts],
        out_specs=(pl.BlockSpec((tm, X_WIDTH), tok),) * 2, out_shape=out_shape,
        compiler_params=_params(1), name="memkv",
    )(mem, *consts)


def _merge_kernel(x_ref, a_ref, bo_ref, sg_ref, mk_ref, mv_ref, wpa_ref, wpb_ref, wout_ref, gx_ref, wxq_ref, wxo_ref,
                  gffn_ref, wr_ref, br_ref, h_ref, u_ref, lg_ref, *, n_seg, seg):
    sg = sg_ref[...]
    mix = (sg[:, :D_MODEL].astype(F32) * _dot(a_ref[...], wpa_ref[...])
           + sg[:, D_MODEL:].astype(F32) * _dot(bo_ref[...], wpb_ref[...]))
    h1 = x_ref[...] + _dot(mix.astype(BF16), wout_ref[...])
    u2 = _rms(h1, gx_ref[...]).astype(BF16)
    qx = (_dot(u2, wxq_ref[...]) * X_HEAD_DIM ** -0.5).astype(BF16)
    segs = []
    for j in range(n_seg):
        kj = mk_ref[j].astype(BF16)
        vj = mv_ref[j].astype(BF16)
        heads = []
        for h in range(X_HEADS):
            hs = slice(h * X_HEAD_DIM, (h + 1) * X_HEAD_DIM)
            s = _dot_nt(qx[j * seg:(j + 1) * seg, hs], kj[:, hs])
            p = jnp.exp(s - jnp.max(s, axis=-1, keepdims=True))
            heads.append(_dot(p.astype(BF16), vj[:, hs]) / jnp.sum(p, axis=-1, keepdims=True))
        segs.append(jnp.concatenate(heads, axis=-1))
    ox = segs[0] if n_seg == 1 else jnp.concatenate(segs, axis=0)
    h2 = h1 + _dot(ox.astype(BF16), wxo_ref[...])
    h_ref[...] = h2
    u3 = _rms(h2, gffn_ref[...])
    _to_row_tiles(u_ref, u3)
    lg_ref[...] = _dot(u3.astype(BF16), wr_ref[...]) + br_ref[...]


def _merge(x, a, bo, sg, mk, mv, lw, tm, seq_len):
    t = x.shape[0]
    n_mem = mk.shape[1]
    n_seg = max(1, tm // seq_len)
    seg = tm // n_seg
    tiles_per_seq = max(1, seq_len // tm)
    tok = lambda i: (i, 0)
    mem = pl.BlockSpec((n_seg, n_mem, X_WIDTH), lambda i: (i // tiles_per_seq, 0, 0))
    consts = [lw["w_pa"], lw["w_pb"], lw["w_out"], lw["g_x"], lw["w_xq"], lw["w_xo"], lw["g_ffn"], lw["w_router"],
              lw["b_router"]]
    out_shape = (jax.ShapeDtypeStruct((t, D_MODEL), F32), jax.ShapeDtypeStruct((t * ROW_TILE, LANES), F32),
                 jax.ShapeDtypeStruct((t, LANES), F32))
    kern = functools.partial(_merge_kernel, n_seg=n_seg, seg=seg)
    return pl.pallas_call(
        kern, grid=(t // tm,),
        in_specs=[pl.BlockSpec((tm, D_MODEL), tok), pl.BlockSpec((tm, HG_WIDTH), tok), pl.BlockSpec((tm, MLA_WIDTH), tok),
                  pl.BlockSpec((tm, 2 * D_MODEL), tok), mem, mem] + [_const_spec(c.shape) for c in consts],
        out_specs=tuple(pl.BlockSpec((s.shape[0] // (t // tm), s.shape[1]), tok) for s in out_shape), out_shape=out_shape,
        compiler_params=_params(1), name="merge",
    )(x, a, bo, sg, mk, mv, *consts)


def _route_kernel(lg_ref, idx_ref, gate_ref, rank_ref, cnt_ref, carry_sc, *, tr):
    @pl.when(pl.program_id(0) == 0)
    def _():
        carry_sc[...] = jnp.zeros(carry_sc.shape, F32)

    lane = lax.broadcasted_iota(I32, (tr, LANES), 1).astype(F32)
    logit = jnp.where(lane < N_EXPERTS, lg_ref[...], -jnp.inf)
    vals, idxs, hots = [], [], []
    for _ in range(TOP_K):
        top = jnp.max(logit, axis=-1, keepdims=True)
        first = jnp.min(jnp.where(logit == top, lane, float(LANES)), axis=-1, keepdims=True)
        hot = lane == first
        vals.append(top)
        idxs.append(first)
        hots.append(hot)
        logit = jnp.where(hot, -jnp.inf, logit)
    exps = [jnp.exp(v - vals[0]) for v in vals]
    denom = exps[0] + exps[1] + exps[2] + exps[3]
    hot_all = sum(jnp.where(h, 1.0, 0.0) for h in hots)
    row = lax.broadcasted_iota(I32, (tr, tr), 0)
    col = lax.broadcasted_iota(I32, (tr, tr), 1)
    before = jnp.where(row > col, 1.0, 0.0).astype(BF16)
    prefix = _dot(before, hot_all.astype(BF16)) + carry_sc[...]
    lane4 = lax.broadcasted_iota(I32, (tr, TOP_K), 1)
    idx = jnp.zeros((tr, TOP_K), F32)
    gate = jnp.zeros((tr, TOP_K), F32)
    rank = jnp.zeros((tr, TOP_K), F32)
    for k in range(TOP_K):
        rk = jnp.sum(jnp.where(hots[k], prefix, 0.0), axis=-1, keepdims=True)
        idx = jnp.where(lane4 == k, idxs[k], idx)
        gate = jnp.where(lane4 == k, exps[k] / denom, gate)
        rank = jnp.where(lane4 == k, rk, rank)
    idx_ref[...] = idx.astype(I32)
    gate_ref[...] = gate
    rank_ref[...] = rank.astype(I32)
    carry_sc[...] += jnp.sum(hot_all, axis=0, keepdims=True)
    cnt_ref[...] = carry_sc[...]


def _route(logits, tr):
    t = logits.shape[0]
    tok = lambda i: (i, 0)
    out_shape = (jax.ShapeDtypeStruct((t, TOP_K), I32), jax.ShapeDtypeStruct((t, TOP_K), F32),
                 jax.ShapeDtypeStruct((t, TOP_K), I32), jax.ShapeDtypeStruct((1, LANES), F32))
    return pl.pallas_call(
        functools.partial(_route_kernel, tr=tr), grid=(t // tr,),
        in_specs=[pl.BlockSpec((tr, LANES), tok)],
        out_specs=(pl.BlockSpec((tr, TOP_K), tok),) * 3 + (pl.BlockSpec((1, LANES), lambda i: (0, 0)),),
        out_shape=out_shape, scratch_shapes=[pltpu.VMEM((1, LANES), F32)],
        compiler_params=_params(1), name="route",
    )(logits)


def _to_row_tiles(ref, x):
    n = x.shape[0]
    for c in range(ROW_TILE):
        ref[pl.ds(c, n, stride=ROW_TILE), :] = x[:, c * LANES:(c + 1) * LANES]


def _row_tile_chunk(ref, n, c):
    return ref[pl.ds(c, n, stride=ROW_TILE), :]


def _moe_kernel(be_ref, rt0_ref, rtn_ref, rdp_ref, x_hbm, wgu_ref, bgu_ref, wd_ref, bd_ref, y_hbm,
                xs0, xs1, ys0, ys1, wgu_bf, wd_bf, gsem, ssem, *, bm, n_blocks):
    b = pl.program_id(0)
    xs, ys = (xs0, xs1), (ys0, ys1)
    rows = bm * ROW_TILE

    def gather(idx_ref, r, slot):
        src = pl.multiple_of(idx_ref[0, 0, r], ROW_TILE)
        return pltpu.make_async_copy(x_hbm.at[pl.ds(src, ROW_TILE), :], xs[slot].at[pl.ds(r * ROW_TILE, ROW_TILE), :],
                                     gsem.at[slot])

    def scatter(r, slot):
        dst = pl.multiple_of(rdp_ref[0, 0, r], ROW_TILE)
        return pltpu.make_async_copy(ys[slot].at[pl.ds(r * ROW_TILE, ROW_TILE), :], y_hbm.at[pl.ds(dst, ROW_TILE), :],
                                     ssem.at[slot])

    def wait_gathers(slot):
        pltpu.make_async_copy(x_hbm.at[pl.ds(0, rows), :], xs[slot], gsem.at[slot]).wait()

    def wait_scatters(slot):
        pltpu.make_async_copy(ys[slot], y_hbm.at[pl.ds(0, rows), :], ssem.at[slot]).wait()

    @pl.when(b == 0)
    def _():
        ys1[...] = jnp.zeros(ys1.shape, F32)

        def start(r, c):
            gather(rt0_ref, r, 0).start()
            return c

        lax.fori_loop(0, bm, start, 0)

    def step(slot):
        other = 1 - slot
        wait_gathers(slot)

        @pl.when(b >= 1)
        def _():
            wait_scatters(slot)

        for r in range(bm):
            gather(rtn_ref, r, other).start()
        for r in range(bm):
            scatter(r, other).start(priority=1)
        x = jnp.concatenate([_row_tile_chunk(xs[slot], bm, c) for c in range(ROW_TILE)], axis=-1).astype(BF16)
        gu = _dot(x, wgu_bf[...]) + bgu_ref[...]
        g = jnp.minimum(gu[:, :D_FF], SWIGLU_LIMIT)
        up = jnp.clip(gu[:, D_FF:], -SWIGLU_LIMIT, SWIGLU_LIMIT)
        act = (up + 1.0) * g * _sigmoid(SWIGLU_ALPHA * g)
        _to_row_tiles(ys[slot], _dot(act.astype(BF16), wd_bf[...]) + bd_ref[...])

    expert = be_ref[jnp.minimum(b, n_blocks - 1)]
    new_expert = jnp.logical_or(b == 0, expert != be_ref[jnp.maximum(b - 1, 0)])

    @pl.when(jnp.logical_and(b < n_blocks, new_expert))
    def _():
        wgu_bf[...] = wgu_ref[...].astype(BF16)
        wd_bf[...] = wd_ref[...].astype(BF16)

    @pl.when(jnp.logical_and(b < n_blocks, b % 2 == 0))
    def _():
        step(0)

    @pl.when(jnp.logical_and(b < n_blocks, b % 2 == 1))
    def _():
        step(1)

    @pl.when(b == n_blocks)
    def _():
        last = (n_blocks - 1) % 2
        wait_gathers(1 - last)
        wait_scatters(1 - last)

        def start(r, c):
            scatter(r, last).start()
            return c

        lax.fori_loop(0, bm, start, 0)
        wait_scatters(last)


def _moe(x, block_expert, row_src, row_dst, lw, bm, n_out_rows):
    n_blocks = block_expert.shape[0]
    spare = (n_out_rows // ROW_TILE - bm + jnp.arange(bm, dtype=I32)) * ROW_TILE
    src = row_src.reshape(n_blocks, 1, bm)
    dst_prev = jnp.concatenate([spare.reshape(1, 1, bm), row_dst.reshape(n_blocks, 1, bm)], axis=0)
    smem = lambda f: pl.BlockSpec((1, 1, bm), f, memory_space=pltpu.SMEM)
    blk = lambda b: jnp.minimum(b, n_blocks - 1)
    grid_spec = pltpu.PrefetchScalarGridSpec(
        num_scalar_prefetch=1, grid=(n_blocks + 1,),
        in_specs=[smem(lambda b, be: (0, 0, 0)), smem(lambda b, be: (blk(b + 1), 0, 0)), smem(lambda b, be: (b, 0, 0)),
                  pl.BlockSpec(memory_space=pl.ANY),
                  pl.BlockSpec((None, D_MODEL, 2 * D_FF), lambda b, be: (be[blk(b)], 0, 0)),
                  pl.BlockSpec((None, 1, 2 * D_FF), lambda b, be: (be[blk(b)], 0, 0)),
                  pl.BlockSpec((None, D_FF, D_MODEL), lambda b, be: (be[blk(b)], 0, 0)),
                  pl.BlockSpec((None, 1, D_MODEL), lambda b, be: (be[blk(b)], 0, 0))],
        out_specs=pl.BlockSpec(memory_space=pl.ANY),
        scratch_shapes=[pltpu.VMEM((bm * ROW_TILE, LANES), F32)] * 4
        + [pltpu.VMEM((D_MODEL, 2 * D_FF), BF16), pltpu.VMEM((D_FF, D_MODEL), BF16)] + [pltpu.SemaphoreType.DMA((2,))] * 2)
    return pl.pallas_call(
        functools.partial(_moe_kernel, bm=bm, n_blocks=n_blocks), grid_spec=grid_spec,
        out_shape=jax.ShapeDtypeStruct((n_out_rows, LANES), F32),
        compiler_params=_params(1), name="moe",
    )(block_expert, src, src, dst_prev, x, lw["w_gu"], lw["b_gu"], lw["w_down"], lw["b_down"])


def _final_kernel(h_ref, gate_ref, y0_ref, y1_ref, y2_ref, y3_ref, g_ref, o_ref, *, tm):
    gate = gate_ref[...]
    gates = [jnp.broadcast_to(gate[:, k:k + 1], (tm, LANES)) for k in range(TOP_K)]
    ssq = jnp.zeros((tm, 1), F32)
    for c in range(ROW_TILE):
        cs = slice(c * LANES, (c + 1) * LANES)
        h = h_ref[:, cs]
        for k, y_ref in enumerate((y0_ref, y1_ref, y2_ref, y3_ref)):
            h = h + gates[k] * _row_tile_chunk(y_ref, tm, c)
        o_ref[:, cs] = h
        ssq = ssq + jnp.sum(h * h, axis=-1, keepdims=True)
    o_ref[...] = o_ref[...] * lax.rsqrt(ssq * (1.0 / D_MODEL) + EPS) * g_ref[...]


def _final(h, gate, y4, g_final, tm, tok_off, t_all):
    t = h.shape[0]
    tok = lambda i: (i, 0)
    slot = lambda k: pl.BlockSpec((tm * ROW_TILE, LANES), lambda i: ((k * t_all + tok_off) // tm + i, 0))
    return pl.pallas_call(
        functools.partial(_final_kernel, tm=tm), grid=(t // tm,),
        in_specs=[pl.BlockSpec((tm, D_MODEL), tok), pl.BlockSpec((tm, TOP_K), tok), slot(0), slot(1), slot(2), slot(3),
                  _const_spec(g_final.shape)],
        out_specs=pl.BlockSpec((tm, D_MODEL), tok), out_shape=jax.ShapeDtypeStruct((t, D_MODEL), F32),
        compiler_params=_params(1), name="final",
    )(h, gate, y4, y4, y4, y4, g_final)


def _rope_tables(pos, reps):
    half = QK_ROPE // 2
    inv_freq = jnp.exp(-math.log(ROPE_THETA) * jnp.arange(half, dtype=F32) / half)
    ang = pos.astype(F32)[:, None] * inv_freq[None, :]
    cos = jnp.concatenate([jnp.cos(ang)] * 2, axis=-1)
    sin = jnp.concatenate([jnp.sin(ang)] * 2, axis=-1)
    n = pos.shape[0]
    ones = jnp.ones((n, QK_NOPE), F32)
    z = lambda w: jnp.zeros((n, w), F32)
    tabs = {"cosq": jnp.concatenate([ones, cos, z(HEAD_PAD - QK_NOPE - QK_ROPE)], axis=-1),
            "sinq": jnp.concatenate([z(QK_NOPE), sin, z(HEAD_PAD - QK_NOPE - QK_ROPE)], axis=-1),
            "cosk": jnp.concatenate([cos, z(LANES - QK_ROPE)], axis=-1),
            "sink": jnp.concatenate([sin, z(LANES - QK_ROPE)], axis=-1)}
    return {k: jnp.tile(v, (reps, 1)) for k, v in tabs.items()}


def _rot_cols(w):
    half = QK_ROPE // 2
    return jnp.concatenate([-w[..., half:], w[..., :half]], axis=-1)


def _layer_weights(l, lb, g_mix, w_in, g_qa, w_uq, g_kva, w_ukv, hg_norm, w_pa, w_pb, w_out, g_mem, w_mk, w_mv, g_x, w_xq,
                   w_xo, g_ffn, w_router, b_router, w_gu, b_gu, w_down, b_down):
    row = lambda v: v.reshape(1, -1).astype(F32)
    wi = w_in[l]
    o_qa = 4 * HG_WIDTH
    o_kv = o_qa + Q_LORA
    o_kr = o_kv + KV_LORA
    o_g = o_kr + QK_ROPE
    w_kr = wi[:, o_kr:o_g]
    zk = jnp.zeros((D_MODEL, LANES - QK_ROPE), F32)
    uq = w_uq[l].reshape(Q_LORA, MLA_HEADS, QK_NOPE + QK_ROPE)
    zq = lambda w: jnp.zeros((Q_LORA, MLA_HEADS, w), F32)
    pad = HEAD_PAD - QK_NOPE - QK_ROPE
    w_q = jnp.concatenate([uq, zq(pad)], axis=-1).reshape(Q_LORA, MLA_HEADS * HEAD_PAD)
    w_qr = jnp.concatenate([zq(QK_NOPE), _rot_cols(uq[..., QK_NOPE:]), zq(pad)], axis=-1).reshape(Q_LORA, MLA_HEADS * HEAD_PAD)
    ukv = w_ukv[l].reshape(KV_LORA, MLA_HEADS, QK_NOPE + V_HEAD)
    w_kn = jnp.concatenate([ukv[..., :QK_NOPE], jnp.zeros((KV_LORA, MLA_HEADS, HEAD_PAD - QK_NOPE), F32)], axis=-1)
    e_head = jnp.concatenate([jnp.zeros((QK_ROPE, QK_NOPE), F32), jnp.eye(QK_ROPE, dtype=F32),
                              jnp.zeros((QK_ROPE, pad), F32)], axis=-1)
    e_kpe = jnp.concatenate([jnp.tile(e_head, (1, MLA_HEADS)), jnp.zeros((LANES - QK_ROPE, MLA_HEADS * HEAD_PAD), F32)], axis=0)
    bf = lambda w: w.astype(BF16)
    return {
        "lb": row(lb[l]), "g_mix": row(g_mix[l]), "w_hg": bf(wi[:, :o_qa]), "w_g": bf(wi[:, o_g:]),
        "w_qa": bf(wi[:, o_qa:o_kv]), "w_kv": bf(wi[:, o_kv:o_kr]),
        "w_kr": bf(jnp.concatenate([w_kr, zk, _rot_cols(w_kr), zk], axis=-1)),
        "g_qa": row(g_qa[l]), "w_q": bf(w_q), "w_qr": bf(w_qr), "g_kva": row(g_kva[l]),
        "w_kn": bf(w_kn.reshape(KV_LORA, MLA_HEADS * HEAD_PAD)), "w_v": bf(ukv[..., QK_NOPE:].reshape(KV_LORA, MLA_WIDTH)),
        "e_kpe": bf(e_kpe), "hg_norm": row(hg_norm[l]), "w_pa": bf(w_pa[l]), "w_pb": bf(w_pb[l]), "w_out": bf(w_out[l]),
        "g_mem": row(g_mem[l]), "w_mk": bf(w_mk[l]), "w_mv": bf(w_mv[l]), "g_x": row(g_x[l]), "w_xq": bf(w_xq[l]),
        "w_xo": bf(w_xo[l]), "g_ffn": row(g_ffn[l]),
        "w_router": bf(jnp.pad(w_router[l], ((0, 0), (0, LANES - N_EXPERTS)))),
        "b_router": jnp.pad(row(b_router[l]), ((0, 0), (0, LANES - N_EXPERTS))),
        "w_gu": w_gu[l], "b_gu": b_gu[l].reshape(N_EXPERTS, 1, 2 * D_FF).astype(F32),
        "w_down": w_down[l], "b_down": b_down[l].reshape(N_EXPERTS, 1, D_MODEL).astype(F32),
    }


def _tile(n, want):
    t = min(n, want)
    while n % t:
        t -= 8
    assert t > 0 and n % t == 0, (n, want)
    return t


def _group_front(x, lw, tabs, seq_len, hg_state, past):
    b, l, _ = x.shape
    t = b * l
    tm = _tile(t, TOKEN_TILE)
    hg, sg, q, ckv, kpe = _inproj(x.reshape(t, D_MODEL), lw, tabs, tm)
    lt = _tile(l, 512)
    a, s_new = _hgrn(hg.reshape(b, l, 4 * HG_WIDTH), lw["lb"], lw["hg_norm"], hg_state, lt, _tile(lt, 128))
    tq = _tile(l, ATTN_TILE)
    if past is None:
        lk_true, q_off = l, 0
        ckv_all, kpe_all = ckv, kpe
        lk = l
    else:
        past_ckv, past_kpe = past
        q_off = past_ckv.shape[1]
        lk_true = q_off + l
        lk = -(-lk_true // ATTN_TILE) * ATTN_TILE
        padr = ((0, 0), (0, lk - lk_true), (0, 0))
        ckv_all = jnp.pad(jnp.concatenate([past_ckv, ckv.reshape(b, l, KV_LORA)], axis=1), padr).reshape(b * lk, KV_LORA)
        past_kpe = jnp.pad(past_kpe, ((0, 0), (0, 0), (0, LANES - QK_ROPE)))
        kpe_all = jnp.pad(jnp.concatenate([past_kpe, kpe.reshape(b, l, LANES)], axis=1), padr).reshape(b * lk, LANES)
    k_arr, v_arr = _kvbuild(ckv_all, kpe_all, lw, _tile(b * lk, TOKEN_TILE))
    bo = _attention(q.reshape(b, l, MLA_HEADS * HEAD_PAD), k_arr.reshape(b, lk, MLA_HEADS * HEAD_PAD),
                    v_arr.reshape(b, lk, MLA_WIDTH), lk_true, q_off, tq, _tile(lk, ATTN_TILE))
    return a.reshape(t, HG_WIDTH), bo.reshape(t, MLA_WIDTH), sg, ckv.reshape(b, l, KV_LORA), kpe[:, :QK_ROPE].reshape(b, l, QK_ROPE), s_new


def kernel(x_prompt, x_sample, cache_mla_ckv, cache_mla_kpe, state_hgrn, cache_mem_k, cache_mem_v, mem_prompt, hg_lb_logits, g_mix, w_in, g_qa, w_uq, g_kva, w_ukv, hg_norm, w_pa, w_pb, w_out, g_mem, w_mk, w_mv, g_x, w_xq, w_xo, g_ffn, w_router, b_router, w_gu, b_gu, w_down, b_down, g_final):
    bp, lp, _ = x_prompt.shape
    bs, ls, _ = x_sample.shape
    depth = w_in.shape[0]
    past_len = cache_mla_ckv.shape[2]
    n_mem = mem_prompt.shape[1]
    tp, ts = bp * lp, bs * ls
    t_all = tp + ts
    tm_p, tm_s = _tile(tp, TOKEN_TILE), _tile(ts, TOKEN_TILE)
    assert tp % tm_s == 0
    lb_all = jnp.cumsum(jax.nn.softmax(hg_lb_logits.astype(F32), axis=0), axis=0)
    tabs_p = _rope_tables(jnp.arange(lp, dtype=I32), max(1, tm_p // lp))
    tabs_s = _rope_tables(past_len + jnp.arange(ls, dtype=I32), max(1, tm_s // ls))
    g_fin = g_final.reshape(1, D_MODEL).astype(F32)

    n_asg = t_all * TOP_K
    bm = MOE_ROWS
    n_blocks = (n_asg + N_EXPERTS * (bm - 1) + bm - 1) // bm
    n_rows = n_blocks * bm

    hp, hs = x_prompt.reshape(tp, D_MODEL), x_sample.reshape(ts, D_MODEL)
    outs = [[] for _ in range(8)]
    for l in range(depth):
        lw = _layer_weights(l, lb_all, g_mix, w_in, g_qa, w_uq, g_kva, w_ukv, hg_norm, w_pa, w_pb, w_out, g_mem, w_mk, w_mv,
                            g_x, w_xq, w_xo, g_ffn, w_router, b_router, w_gu, b_gu, w_down, b_down)
        mk, mv = _memkv(mem_prompt.reshape(bp * n_mem, D_MODEL), lw, _tile(bp * n_mem, TOKEN_TILE))
        mk, mv = mk.reshape(bp, n_mem, X_WIDTH), mv.reshape(bp, n_mem, X_WIDTH)
        zero_state = jnp.zeros((bp, HG_HEADS, HG_DK, HG_DV), F32)
        a_p, bo_p, sg_p, ckv_p, kpe_p, st_p = _group_front(hp.reshape(bp, lp, D_MODEL), lw, tabs_p, lp, zero_state, None)
        a_s, bo_s, sg_s, ckv_s, kpe_s, st_s = _group_front(hs.reshape(bs, ls, D_MODEL), lw, tabs_s, ls, state_hgrn[l],
                                                           (cache_mla_ckv[l], cache_mla_kpe[l]))
        h2_p, u3_p, lg_p = _merge(hp, a_p, bo_p, sg_p, mk, mv, lw, tm_p, lp)
        h2_s, u3_s, lg_s = _merge(hs, a_s, bo_s, sg_s, cache_mem_k[l].reshape(bs, n_mem, X_WIDTH),
                                  cache_mem_v[l].reshape(bs, n_mem, X_WIDTH), lw, tm_s, ls)
        u3 = jnp.concatenate([u3_p, u3_s], axis=0)
        idx, gate, rank, cnt = _route(jnp.concatenate([lg_p, lg_s], axis=0), _tile(t_all, TOKEN_TILE))
        counts = cnt[0, :N_EXPERTS].astype(I32)
        padded = (counts + bm - 1) // bm * bm
        pend = jnp.cumsum(padded)
        dest = ((pend - padded)[idx] + rank).reshape(-1)
        block_start = jnp.arange(n_blocks, dtype=I32) * bm
        block_expert = jnp.minimum(jnp.sum((pend[None, :] <= block_start[:, None]).astype(I32), axis=1), N_EXPERTS - 1)
        row_asg = jnp.full((n_rows,), -1, I32).at[dest].set(jnp.arange(n_asg, dtype=I32))
        row_t, row_k = row_asg // TOP_K, row_asg % TOP_K
        pad_row = row_asg < 0
        row_src = jnp.where(pad_row, 0, row_t) * ROW_TILE
        row_dst = jnp.where(pad_row, n_asg + jnp.arange(n_rows, dtype=I32) % bm, row_k * t_all + row_t) * ROW_TILE
        y4 = _moe(u3, block_expert, row_src, row_dst, lw, bm, (n_asg + bm) * ROW_TILE)
        last = l == depth - 1
        gf = g_fin if last else jnp.ones_like(g_fin)
        yp = _final(h2_p, gate[:tp], y4, gf, tm_p, 0, t_all)
        ys = _final(h2_s, gate[tp:], y4, gf, tm_s, tp, t_all)
        assert last, "multi-layer stacking needs the un-normalised residual stream"
        hp, hs = yp, ys
        for lst, v in zip(outs, (ckv_p, kpe_p, st_p, mk.reshape(bp, n_mem, X_HEADS, X_HEAD_DIM),
                                 mv.reshape(bp, n_mem, X_HEADS, X_HEAD_DIM), ckv_s, kpe_s, st_s)):
            lst.append(v)
    stk = [jnp.stack(o) for o in outs]
    return (hp.reshape(bp, lp, D_MODEL), hs.reshape(bs, ls, D_MODEL), stk[0], stk[1], stk[2], stk[3], stk[4], stk[5], stk[6], stk[7])
```

```python
import functools
import math

import jax
import jax.numpy as jnp
from jax import lax
from jax.experimental import pallas as pl
from jax.experimental.pallas import tpu as pltpu

F32 = jnp.float32
BF16 = jnp.bfloat16
I32 = jnp.int32

D_MODEL = 1024
CHUNK = 64
EPS = 1e-6
HG_HEADS = 4
HG_DK = 128
HG_DV = 128
HG_WIDTH = HG_HEADS * HG_DV
HG_SUB = 8
MLA_HEADS = 8
Q_LORA = 384
KV_LORA = 256
QK_NOPE = 64
QK_ROPE = 32
V_HEAD = 64
MLA_WIDTH = MLA_HEADS * V_HEAD
HEAD_PAD = 128
ROPE_THETA = 10000.0
X_HEADS = 4
X_HEAD_DIM = 128
X_WIDTH = X_HEADS * X_HEAD_DIM
N_EXPERTS = 32
TOP_K = 4
D_FF = D_MODEL
SWIGLU_LIMIT = 7.0
SWIGLU_ALPHA = 1.702
LANES = 128
ROW_TILE = D_MODEL // LANES
NEG = -1e30

VMEM_LIMIT = 56 * 1024 * 1024
TOKEN_TILE = 512
ATTN_TILE = 256
MOE_ROWS = 256
MOE_BUFFERS = 3
MOE_AHEAD = MOE_BUFFERS - 1


def _dot(a, b):
    return jnp.dot(a, b, preferred_element_type=F32)


def _dot_nt(a, b):
    return lax.dot_general(a, b, (((1,), (1,)), ((), ())), preferred_element_type=F32)


def _dot_tn(a, b):
    return lax.dot_general(a, b, (((0,), (0,)), ((), ())), preferred_element_type=F32)


def _rms(x, g):
    return x * lax.rsqrt(jnp.mean(x * x, axis=-1, keepdims=True) + EPS) * g


def _sigmoid(x):
    return 1.0 / (1.0 + jnp.exp(-x))


def _const_spec(shape):
    zeros = (0,) * len(shape)
    return pl.BlockSpec(shape, lambda *_: zeros, pipeline_mode=pl.Buffered(1))


def _params(n_axes):
    return pltpu.CompilerParams(dimension_semantics=("arbitrary",) * n_axes, vmem_limit_bytes=VMEM_LIMIT)


def _inproj_kernel(x_ref, gmix_ref, whg_ref, wg_ref, wqa_ref, wkv_ref, wkr_ref, gqa_ref, wq_ref, wqr_ref, gkva_ref,
                   cosq_ref, sinq_ref, cosk_ref, sink_ref,
                   hg_ref, sg_ref, q_ref, ckv_ref, kpe_ref):
    u = _rms(x_ref[...], gmix_ref[...]).astype(BF16)
    hg_ref[...] = _dot(u, whg_ref[...])
    sg_ref[...] = _sigmoid(_dot(u, wg_ref[...])).astype(BF16)
    qn = _rms(_dot(u, wqa_ref[...]), gqa_ref[...]).astype(BF16)
    cosq = jnp.concatenate([cosq_ref[...]] * MLA_HEADS, axis=-1)
    sinq = jnp.concatenate([sinq_ref[...]] * MLA_HEADS, axis=-1)
    scale = (QK_NOPE + QK_ROPE) ** -0.5
    q = (_dot(qn, wq_ref[...]) * cosq + _dot(qn, wqr_ref[...]) * sinq) * scale
    q_ref[...] = q.astype(BF16)
    ckv_ref[...] = _rms(_dot(u, wkv_ref[...]), gkva_ref[...])
    pk = _dot(u, wkr_ref[...])
    kpe_ref[...] = pk[:, :LANES] * cosk_ref[...] + pk[:, LANES:] * sink_ref[...]


def _inproj(x, lw, tabs, tm):
    t = x.shape[0]
    nt = tabs["cosq"].shape[0] // tm
    tok = lambda i: (i, 0)
    tab = lambda i: (i % nt, 0)
    consts = [lw["g_mix"], lw["w_hg"], lw["w_g"], lw["w_qa"], lw["w_kv"], lw["w_kr"], lw["g_qa"], lw["w_q"], lw["w_qr"],
              lw["g_kva"]]
    in_specs = ([pl.BlockSpec((tm, D_MODEL), tok)] + [_const_spec(c.shape) for c in consts]
                + [pl.BlockSpec((tm, LANES), tab)] * 4)
    out_shape = (jax.ShapeDtypeStruct((t, 4 * HG_WIDTH), F32), jax.ShapeDtypeStruct((t, 2 * D_MODEL), BF16),
                 jax.ShapeDtypeStruct((t, MLA_HEADS * HEAD_PAD), BF16), jax.ShapeDtypeStruct((t, KV_LORA), F32),
                 jax.ShapeDtypeStruct((t, LANES), F32))
    out_specs = tuple(pl.BlockSpec((tm, s.shape[1]), tok) for s in out_shape)
    return pl.pallas_call(
        _inproj_kernel, grid=(t // tm,), in_specs=in_specs, out_specs=out_specs, out_shape=out_shape,
        compiler_params=_params(1), name="inproj",
    )(x, *consts, tabs["cosq"], tabs["sinq"], tabs["cosk"], tabs["sink"])


def _kvbuild_kernel(ckv_ref, kpe_ref, wkn_ref, wv_ref, e_ref, k_ref, v_ref):
    c = ckv_ref[...].astype(BF16)
    k_ref[...] = (_dot(c, wkn_ref[...]) + _dot(kpe_ref[...].astype(BF16), e_ref[...])).astype(BF16)
    v_ref[...] = _dot(c, wv_ref[...]).astype(BF16)


def _kvbuild(ckv, kpe, lw, tm):
    n = ckv.shape[0]
    tok = lambda i: (i, 0)
    consts = [lw["w_kn"], lw["w_v"], lw["e_kpe"]]
    out_shape = (jax.ShapeDtypeStruct((n, MLA_HEADS * HEAD_PAD), BF16), jax.ShapeDtypeStruct((n, MLA_WIDTH), BF16))
    return pl.pallas_call(
        _kvbuild_kernel, grid=(n // tm,),
        in_specs=[pl.BlockSpec((tm, KV_LORA), tok), pl.BlockSpec((tm, LANES), tok)] + [_const_spec(c.shape) for c in consts],
        out_specs=tuple(pl.BlockSpec((tm, s.shape[1]), tok) for s in out_shape), out_shape=out_shape,
        compiler_params=_params(1), name="kvbuild",
    )(ckv, kpe, *consts)


def _attn_tile(q_ref, k_ref, v_ref, o_ref, *, q0, tq, tk, lk_true):
    lim_first = min((q0 // CHUNK + 1) * CHUNK, lk_true)
    lim_last = min(((q0 + tq - 1) // CHUNK + 1) * CHUNK, lk_true)
    n_keys = -(-lim_last // tk) * tk
    n_open = lim_first // tk * tk
    outs = []
    for h in range(2):
        hs = slice(h * HEAD_PAD, (h + 1) * HEAD_PAD)
        s = _dot_nt(q_ref[:, hs], k_ref[0:n_keys, hs])
        if n_open < n_keys:
            edge = s[:, n_open:]
            kpos = n_open + lax.broadcasted_iota(I32, edge.shape, 1)
            qpos = q0 + lax.broadcasted_iota(I32, edge.shape, 0)
            edge = jnp.where(kpos < jnp.minimum((qpos // CHUNK + 1) * CHUNK, lk_true), edge, NEG)
            s = edge if n_open == 0 else jnp.concatenate([s[:, :n_open], edge], axis=-1)
        p = jnp.exp(s - jnp.max(s, axis=-1, keepdims=True))
        outs.append(_dot(p.astype(BF16), v_ref[0:n_keys, :]) / jnp.sum(p, axis=-1, keepdims=True))
    lane = lax.broadcasted_iota(I32, outs[0].shape, 1)
    o_ref[...] = jnp.where(lane < V_HEAD, outs[0], outs[1]).astype(BF16)


def _attn_kernel(q_ref, k_ref, v_ref, o_ref, *, n_q, tq, tk, lk_true, q_off):
    i = pl.program_id(2)
    for qi in range(n_q):
        @pl.when(i == qi)
        def _(qi=qi):
            _attn_tile(q_ref, k_ref, v_ref, o_ref, q0=q_off + qi * tq, tq=tq, tk=tk, lk_true=lk_true)


def _attention(q, k, v, lk_true, q_off, tq, tk):
    b, lq, _ = q.shape
    lk = k.shape[1]
    kern = functools.partial(_attn_kernel, n_q=lq // tq, tq=tq, tk=tk, lk_true=lk_true, q_off=q_off)
    return pl.pallas_call(
        kern, grid=(b, MLA_HEADS // 2, lq // tq),
        in_specs=[pl.BlockSpec((None, tq, 2 * HEAD_PAD), lambda bi, hp, i: (bi, i, hp)),
                  pl.BlockSpec((None, lk, 2 * HEAD_PAD), lambda bi, hp, i: (bi, 0, hp)),
                  pl.BlockSpec((None, lk, 2 * V_HEAD), lambda bi, hp, i: (bi, 0, hp))],
        out_specs=pl.BlockSpec((None, tq, 2 * V_HEAD), lambda bi, hp, i: (bi, i, hp)),
        out_shape=jax.ShapeDtypeStruct((b, lq, MLA_WIDTH), BF16),
        compiler_params=_params(3), name="attn",
    )(q, k, v)


def _hgrn_kernel(q_ref, f_ref, i_ref, g_ref, lb_ref, norm_ref, s0_ref, a_ref, sout_ref, st_sc, *, chunk, n_chunks):
    t = pl.program_id(1)

    @pl.when(t == 0)
    def _():
        for h in range(HG_HEADS):
            st_sc[h] = s0_ref[h].T

    row = lax.broadcasted_iota(I32, (chunk, chunk), 0)
    col = lax.broadcasted_iota(I32, (chunk, chunk), 1)
    tri = jnp.where(row >= col, 1.0, 0.0).astype(BF16)
    nsub = chunk // HG_SUB
    sub_row = lax.broadcasted_iota(I32, (nsub, HG_SUB, 1), 1)
    same_block = {}
    m = chunk // 2
    while m >= HG_SUB:
        same_block[2 * m] = jnp.where(row // (2 * m) == col // (2 * m), 1.0, 0.0)
        m //= 2

    def head_chunk(r0, h):
        hs = slice(h * HG_DK, (h + 1) * HG_DK)
        lb = lb_ref[:, hs]
        q = q_ref[pl.ds(r0, chunk), hs]
        v = i_ref[pl.ds(r0, chunk), hs]
        f = lb + (1.0 - lb) * _sigmoid(f_ref[pl.ds(r0, chunk), hs])
        lf = jnp.log(f)
        k = 1.0 - f
        hi = lf.astype(BF16)
        r1 = lf - hi.astype(F32)
        mid = r1.astype(BF16)
        lo = (r1 - mid.astype(F32)).astype(BF16)
        b = _dot(tri, hi) + _dot(tri, mid) + _dot(tri, lo)
        st = st_sc[h]
        o = _dot_nt((q * jnp.exp(b)).astype(BF16), st.astype(BF16))
        a_cross = jnp.zeros((chunk, chunk), F32)
        m = chunk // 2
        while m >= HG_SUB:
            blk = 2 * m
            shp = (chunk // blk, blk, HG_DK)
            b3 = b.reshape(shp)
            ref = b3[:, m - 1:m, :]
            first = lax.broadcasted_iota(I32, (chunk // blk, blk, 1), 1) < m
            decay = jnp.exp(-jnp.abs(b3 - ref))
            ql = jnp.where(first, 0.0, q.reshape(shp) * decay)
            kl = jnp.where(first, k.reshape(shp) * decay, 0.0)
            a_l = _dot_nt(ql.reshape(chunk, HG_DK).astype(BF16), kl.reshape(chunk, HG_DK).astype(BF16))
            a_cross = a_cross + a_l * same_block[blk]
            m //= 2
        o = o + _dot(a_cross.astype(BF16), v.astype(BF16))
        shp = (nsub, HG_SUB, HG_DK)
        q3, k3, b3, v3 = q.reshape(shp), k.reshape(shp), b.reshape(shp), v.reshape(shp)
        od = jnp.zeros(shp, F32)
        for s in range(HG_SUB):
            w = q3 * jnp.exp(b3 - b3[:, s:s + 1, :]) * k3[:, s:s + 1, :]
            a_col = jnp.where(sub_row >= s, jnp.sum(w, axis=-1, keepdims=True), 0.0)
            od = od + a_col * v3[:, s:s + 1, :]
        o = o + od.reshape(chunk, HG_DV)
        b_last = b[chunk - 1:chunk, :]
        kd = k * jnp.exp(b_last - b)
        st_sc[h] = st * jnp.exp(b_last) + _dot_tn(v.astype(BF16), kd.astype(BF16))
        g = g_ref[pl.ds(r0, chunk), hs]
        a_ref[pl.ds(r0, chunk), hs] = (_rms(o, norm_ref[:, hs]) * (g * _sigmoid(g))).astype(BF16)

    def body(c, carry):
        r0 = pl.multiple_of(c * chunk, chunk)
        for h in range(HG_HEADS):
            head_chunk(r0, h)
        return carry

    lax.fori_loop(0, n_chunks, body, 0)

    @pl.when(t == pl.num_programs(1) - 1)
    def _():
        for h in range(HG_HEADS):
            sout_ref[h] = st_sc[h].T


def _hgrn(hg, lb, norm, s0, lt, chunk):
    b, l, _ = hg.shape
    kern = functools.partial(_hgrn_kernel, chunk=chunk, n_chunks=lt // chunk)
    seg = lambda j: pl.BlockSpec((None, lt, HG_WIDTH), lambda bi, t: (bi, t, j))
    head_vec = _const_spec((1, HG_WIDTH))
    state = pl.BlockSpec((None, HG_HEADS, HG_DK, HG_DV), lambda bi, t: (bi, 0, 0, 0))
    return pl.pallas_call(
        kern, grid=(b, l // lt),
        in_specs=[seg(0), seg(1), seg(2), seg(3), head_vec, head_vec, state],
        out_specs=(pl.BlockSpec((None, lt, HG_WIDTH), lambda bi, t: (bi, t, 0)), state),
        out_shape=(jax.ShapeDtypeStruct((b, l, HG_WIDTH), BF16), jax.ShapeDtypeStruct(s0.shape, F32)),
        scratch_shapes=[pltpu.VMEM((HG_HEADS, HG_DV, HG_DK), F32)],
        compiler_params=_params(2), name="hgrn",
    )(hg, hg, hg, hg, lb, norm, s0)


def _memkv_kernel(m_ref, g_ref, wk_ref, wv_ref, k_ref, v_ref):
    m = _rms(m_ref[...], g_ref[...]).astype(BF16)
    k_ref[...] = _dot(m, wk_ref[...])
    v_ref[...] = _dot(m, wv_ref[...])


def _memkv(mem, lw, tm):
    n = mem.shape[0]
    tok = lambda i: (i, 0)
    consts = [lw["g_mem"], lw["w_mk"], lw["w_mv"]]
    out_shape = (jax.ShapeDtypeStruct((n, X_WIDTH), F32),) * 2
    return pl.pallas_call(
        _memkv_kernel, grid=(n // tm,),
        in_specs=[pl.BlockSpec((tm, D_MODEL), tok)] + [_const_spec(c.shape) for c in consts],
        out_specs=(pl.BlockSpec((tm, X_WIDTH), tok),) * 2, out_shape=out_shape,
        compiler_params=_params(1), name="memkv",
    )(mem, *consts)


def _merge_kernel(x_ref, a_ref, bo_ref, sg_ref, mk_ref, mv_ref, wpa_ref, wpb_ref, wout_ref, gx_ref, wxq_ref, wxo_ref,
                  gffn_ref, wr_ref, br_ref, h_ref, u_ref, lg_ref, *, n_seg, seg):
    sg = sg_ref[...]
    mix = (sg[:, :D_MODEL].astype(F32) * _dot(a_ref[...], wpa_ref[...])
           + sg[:, D_MODEL:].astype(F32) * _dot(bo_ref[...], wpb_ref[...]))
    h1 = x_ref[...] + _dot(mix.astype(BF16), wout_ref[...])
    u2 = _rms(h1, gx_ref[...]).astype(BF16)
    qx = (_dot(u2, wxq_ref[...]) * X_HEAD_DIM ** -0.5).astype(BF16)
    segs = []
    for j in range(n_seg):
        kj = mk_ref[j].astype(BF16)
        vj = mv_ref[j].astype(BF16)
        heads = []
        for h in range(X_HEADS):
            hs = slice(h * X_HEAD_DIM, (h + 1) * X_HEAD_DIM)
            s = _dot_nt(qx[j * seg:(j + 1) * seg, hs], kj[:, hs])
            p = jnp.exp(s - jnp.max(s, axis=-1, keepdims=True))
            heads.append(_dot(p.astype(BF16), vj[:, hs]) / jnp.sum(p, axis=-1, keepdims=True))
        segs.append(jnp.concatenate(heads, axis=-1))
    ox = segs[0] if n_seg == 1 else jnp.concatenate(segs, axis=0)
    h2 = h1 + _dot(ox.astype(BF16), wxo_ref[...])
    h_ref[...] = h2
    u3 = _rms(h2, gffn_ref[...])
    _to_row_tiles(u_ref, u3)
    lg_ref[...] = _dot(u3.astype(BF16), wr_ref[...]) + br_ref[...]


def _merge(x, a, bo, sg, mk, mv, lw, tm, seq_len):
    t = x.shape[0]
    n_mem = mk.shape[1]
    n_seg = max(1, tm // seq_len)
    seg = tm // n_seg
    tiles_per_seq = max(1, seq_len // tm)
    tok = lambda i: (i, 0)
    mem = pl.BlockSpec((n_seg, n_mem, X_WIDTH), lambda i: (i // tiles_per_seq, 0, 0))
    consts = [lw["w_pa"], lw["w_pb"], lw["w_out"], lw["g_x"], lw["w_xq"], lw["w_xo"], lw["g_ffn"], lw["w_router"],
              lw["b_router"]]
    out_shape = (jax.ShapeDtypeStruct((t, D_MODEL), F32), jax.ShapeDtypeStruct((t * ROW_TILE, LANES), F32),
                 jax.ShapeDtypeStruct((t, LANES), F32))
    kern = functools.partial(_merge_kernel, n_seg=n_seg, seg=seg)
    return pl.pallas_call(
        kern, grid=(t // tm,),
        in_specs=[pl.BlockSpec((tm, D_MODEL), tok), pl.BlockSpec((tm, HG_WIDTH), tok), pl.BlockSpec((tm, MLA_WIDTH), tok),
                  pl.BlockSpec((tm, 2 * D_MODEL), tok), mem, mem] + [_const_spec(c.shape) for c in consts],
        out_specs=tuple(pl.BlockSpec((s.shape[0] // (t // tm), s.shape[1]), tok) for s in out_shape), out_shape=out_shape,
        compiler_params=_params(1), name="merge",
    )(x, a, bo, sg, mk, mv, *consts)


def _route_kernel(lg_ref, idx_ref, gate_ref, rank_ref, cnt_ref, carry_sc, *, tr):
    @pl.when(pl.program_id(0) == 0)
    def _():
        carry_sc[...] = jnp.zeros(carry_sc.shape, F32)

    lane = lax.broadcasted_iota(I32, (tr, LANES), 1).astype(F32)
    logit = jnp.where(lane < N_EXPERTS, lg_ref[...], -jnp.inf)
    vals, idxs, hots = [], [], []
    for _ in range(TOP_K):
        top = jnp.max(logit, axis=-1, keepdims=True)
        first = jnp.min(jnp.where(logit == top, lane, float(LANES)), axis=-1, keepdims=True)
        hot = lane == first
        vals.append(top)
        idxs.append(first)
        hots.append(hot)
        logit = jnp.where(hot, -jnp.inf, logit)
    exps = [jnp.exp(v - vals[0]) for v in vals]
    denom = exps[0] + exps[1] + exps[2] + exps[3]
    hot_all = sum(jnp.where(h, 1.0, 0.0) for h in hots)
    row = lax.broadcasted_iota(I32, (tr, tr), 0)
    col = lax.broadcasted_iota(I32, (tr, tr), 1)
    before = jnp.where(row > col, 1.0, 0.0).astype(BF16)
    prefix = _dot(before, hot_all.astype(BF16)) + carry_sc[...]
    lane4 = lax.broadcasted_iota(I32, (tr, TOP_K), 1)
    idx = jnp.zeros((tr, TOP_K), F32)
    gate = jnp.zeros((tr, TOP_K), F32)
    rank = jnp.zeros((tr, TOP_K), F32)
    for k in range(TOP_K):
        rk = jnp.sum(jnp.where(hots[k], prefix, 0.0), axis=-1, keepdims=True)
        idx = jnp.where(lane4 == k, idxs[k], idx)
        gate = jnp.where(lane4 == k, exps[k] / denom, gate)
        rank = jnp.where(lane4 == k, rk, rank)
    idx_ref[...] = idx.astype(I32)
    gate_ref[...] = gate
    rank_ref[...] = rank.astype(I32)
    carry_sc[...] += jnp.sum(hot_all, axis=0, keepdims=True)
    cnt_ref[...] = carry_sc[...]


def _route(logits, tr):
    t = logits.shape[0]
    tok = lambda i: (i, 0)
    out_shape = (jax.ShapeDtypeStruct((t, TOP_K), I32), jax.ShapeDtypeStruct((t, TOP_K), F32),
                 jax.ShapeDtypeStruct((t, TOP_K), I32), jax.ShapeDtypeStruct((1, LANES), F32))
    return pl.pallas_call(
        functools.partial(_route_kernel, tr=tr), grid=(t // tr,),
        in_specs=[pl.BlockSpec((tr, LANES), tok)],
        out_specs=(pl.BlockSpec((tr, TOP_K), tok),) * 3 + (pl.BlockSpec((1, LANES), lambda i: (0, 0)),),
        out_shape=out_shape, scratch_shapes=[pltpu.VMEM((1, LANES), F32)],
        compiler_params=_params(1), name="route",
    )(logits)


def _to_row_tiles(ref, x):
    n = x.shape[0]
    for c in range(ROW_TILE):
        ref[pl.ds(c, n, stride=ROW_TILE), :] = x[:, c * LANES:(c + 1) * LANES]


def _row_tile_chunk(ref, n, c):
    return ref[pl.ds(c, n, stride=ROW_TILE), :]


def _moe_kernel(be_ref, *refs, bm, n_blocks):
    rt_first = refs[:MOE_AHEAD]
    rtn_ref, rdp_ref, x_hbm, wgu_ref, bgu_ref, wd_ref, bd_ref, y_hbm = refs[MOE_AHEAD:MOE_AHEAD + 8]
    scratch = refs[MOE_AHEAD + 8:]
    xs, ys = scratch[:MOE_BUFFERS], scratch[MOE_BUFFERS:2 * MOE_BUFFERS]
    wgu_bf, wd_bf, gsem, ssem = scratch[2 * MOE_BUFFERS:]
    b = pl.program_id(0)
    rows = bm * ROW_TILE

    def gather(idx_ref, r, slot):
        src = pl.multiple_of(idx_ref[0, 0, r], ROW_TILE)
        return pltpu.make_async_copy(x_hbm.at[pl.ds(src, ROW_TILE), :], xs[slot].at[pl.ds(r * ROW_TILE, ROW_TILE), :],
                                     gsem.at[slot])

    def scatter(r, slot):
        dst = pl.multiple_of(rdp_ref[0, 0, r], ROW_TILE)
        return pltpu.make_async_copy(ys[slot].at[pl.ds(r * ROW_TILE, ROW_TILE), :], y_hbm.at[pl.ds(dst, ROW_TILE), :],
                                     ssem.at[slot])

    def wait_gathers(slot):
        pltpu.make_async_copy(x_hbm.at[pl.ds(0, rows), :], xs[slot], gsem.at[slot]).wait()

    def wait_scatters(slot):
        pltpu.make_async_copy(ys[slot], y_hbm.at[pl.ds(0, rows), :], ssem.at[slot]).wait()

    @pl.when(b == 0)
    def _():
        ys[MOE_BUFFERS - 1][...] = jnp.zeros(ys[MOE_BUFFERS - 1].shape, F32)
        for j in range(MOE_AHEAD):
            def start(r, c, j=j):
                gather(rt_first[j], r, j).start()
                return c

            lax.fori_loop(0, bm, start, 0)

    def step(slot):
        prev = (slot - 1) % MOE_BUFFERS
        wait_gathers(slot)

        @pl.when(b >= MOE_BUFFERS - 1)
        def _():
            wait_scatters(slot)

        for r in range(bm):
            gather(rtn_ref, r, prev).start()
        for r in range(bm):
            scatter(r, prev).start(priority=1)
        x = jnp.concatenate([_row_tile_chunk(xs[slot], bm, c) for c in range(ROW_TILE)], axis=-1).astype(BF16)
        gu = _dot(x, wgu_bf[...]) + bgu_ref[...]
        g = jnp.minimum(gu[:, :D_FF], SWIGLU_LIMIT)
        up = jnp.clip(gu[:, D_FF:], -SWIGLU_LIMIT, SWIGLU_LIMIT)
        act = (up + 1.0) * g * _sigmoid(SWIGLU_ALPHA * g)
        _to_row_tiles(ys[slot], _dot(act.astype(BF16), wd_bf[...]) + bd_ref[...])

    expert = be_ref[jnp.minimum(b, n_blocks - 1)]
    new_expert = jnp.logical_or(b == 0, expert != be_ref[jnp.maximum(b - 1, 0)])

    @pl.when(jnp.logical_and(b < n_blocks, new_expert))
    def _():
        wgu_bf[...] = wgu_ref[...].astype(BF16)
        wd_bf[...] = wd_ref[...].astype(BF16)

    for v in range(MOE_BUFFERS):
        @pl.when(jnp.logical_and(b < n_blocks, b % MOE_BUFFERS == v))
        def _(v=v):
            step(v)

    @pl.when(b == n_blocks)
    def _():
        for j in range(MOE_AHEAD):
            wait_gathers((n_blocks + j) % MOE_BUFFERS)
        for j in range(2, MOE_BUFFERS + 1):
            wait_scatters((n_blocks - j) % MOE_BUFFERS)
        last = (n_blocks - 1) % MOE_BUFFERS

        def start(r, c):
            scatter(r, last).start()
            return c

        lax.fori_loop(0, bm, start, 0)
        wait_scatters(last)


def _moe(x, block_expert, row_src, row_dst, lw, bm, n_out_rows):
    n_blocks = block_expert.shape[0]
    assert n_blocks >= MOE_BUFFERS
    spare = (n_out_rows // ROW_TILE - bm + jnp.arange(bm, dtype=I32)) * ROW_TILE
    src = row_src.reshape(n_blocks, 1, bm)
    dst_prev = jnp.concatenate([spare.reshape(1, 1, bm), row_dst.reshape(n_blocks, 1, bm)], axis=0)
    smem = lambda f: pl.BlockSpec((1, 1, bm), f, memory_space=pltpu.SMEM)
    blk = lambda b: jnp.minimum(b, n_blocks - 1)
    first = [smem(lambda b, be, j=j: (j, 0, 0)) for j in range(MOE_AHEAD)]
    grid_spec = pltpu.PrefetchScalarGridSpec(
        num_scalar_prefetch=1, grid=(n_blocks + 1,),
        in_specs=first + [smem(lambda b, be: (blk(b + MOE_AHEAD), 0, 0)), smem(lambda b, be: (b, 0, 0)),
                  pl.BlockSpec(memory_space=pl.ANY),
                  pl.BlockSpec((None, D_MODEL, 2 * D_FF), lambda b, be: (be[blk(b)], 0, 0)),
                  pl.BlockSpec((None, 1, 2 * D_FF), lambda b, be: (be[blk(b)], 0, 0)),
                  pl.BlockSpec((None, D_FF, D_MODEL), lambda b, be: (be[blk(b)], 0, 0)),
                  pl.BlockSpec((None, 1, D_MODEL), lambda b, be: (be[blk(b)], 0, 0))],
        out_specs=pl.BlockSpec(memory_space=pl.ANY),
        scratch_shapes=[pltpu.VMEM((bm * ROW_TILE, LANES), F32)] * (2 * MOE_BUFFERS)
        + [pltpu.VMEM((D_MODEL, 2 * D_FF), BF16), pltpu.VMEM((D_FF, D_MODEL), BF16)]
        + [pltpu.SemaphoreType.DMA((MOE_BUFFERS,))] * 2)
    return pl.pallas_call(
        functools.partial(_moe_kernel, bm=bm, n_blocks=n_blocks), grid_spec=grid_spec,
        out_shape=jax.ShapeDtypeStruct((n_out_rows, LANES), F32),
        compiler_params=_params(1), name="moe",
    )(block_expert, *([src] * (MOE_AHEAD + 1)), dst_prev, x, lw["w_gu"], lw["b_gu"], lw["w_down"], lw["b_down"])


def _final_kernel(h_ref, gate_ref, y0_ref, y1_ref, y2_ref, y3_ref, g_ref, o_ref, *, tm):
    gate = gate_ref[...]
    gates = [jnp.broadcast_to(gate[:, k:k + 1], (tm, LANES)) for k in range(TOP_K)]
    ssq = jnp.zeros((tm, 1), F32)
    for c in range(ROW_TILE):
        cs = slice(c * LANES, (c + 1) * LANES)
        h = h_ref[:, cs]
        for k, y_ref in enumerate((y0_ref, y1_ref, y2_ref, y3_ref)):
            h = h + gates[k] * _row_tile_chunk(y_ref, tm, c)
        o_ref[:, cs] = h
        ssq = ssq + jnp.sum(h * h, axis=-1, keepdims=True)
    o_ref[...] = o_ref[...] * lax.rsqrt(ssq * (1.0 / D_MODEL) + EPS) * g_ref[...]


def _final(h, gate, y4, g_final, tm, tok_off, t_all):
    t = h.shape[0]
    tok = lambda i: (i, 0)
    slot = lambda k: pl.BlockSpec((tm * ROW_TILE, LANES), lambda i: ((k * t_all + tok_off) // tm + i, 0))
    return pl.pallas_call(
        functools.partial(_final_kernel, tm=tm), grid=(t // tm,),
        in_specs=[pl.BlockSpec((tm, D_MODEL), tok), pl.BlockSpec((tm, TOP_K), tok), slot(0), slot(1), slot(2), slot(3),
                  _const_spec(g_final.shape)],
        out_specs=pl.BlockSpec((tm, D_MODEL), tok), out_shape=jax.ShapeDtypeStruct((t, D_MODEL), F32),
        compiler_params=_params(1), name="final",
    )(h, gate, y4, y4, y4, y4, g_final)


def _rope_tables(pos, reps):
    half = QK_ROPE // 2
    inv_freq = jnp.exp(-math.log(ROPE_THETA) * jnp.arange(half, dtype=F32) / half)
    ang = pos.astype(F32)[:, None] * inv_freq[None, :]
    cos = jnp.concatenate([jnp.cos(ang)] * 2, axis=-1)
    sin = jnp.concatenate([jnp.sin(ang)] * 2, axis=-1)
    n = pos.shape[0]
    ones = jnp.ones((n, QK_NOPE), F32)
    z = lambda w: jnp.zeros((n, w), F32)
    tabs = {"cosq": jnp.concatenate([ones, cos, z(HEAD_PAD - QK_NOPE - QK_ROPE)], axis=-1),
            "sinq": jnp.concatenate([z(QK_NOPE), sin, z(HEAD_PAD - QK_NOPE - QK_ROPE)], axis=-1),
            "cosk": jnp.concatenate([cos, z(LANES - QK_ROPE)], axis=-1),
            "sink": jnp.concatenate([sin, z(LANES - QK_ROPE)], axis=-1)}
    return {k: jnp.tile(v, (reps, 1)) for k, v in tabs.items()}


def _rot_cols(w):
    half = QK_ROPE // 2
    return jnp.concatenate([-w[..., half:], w[..., :half]], axis=-1)


def _layer_weights(l, lb, g_mix, w_in, g_qa, w_uq, g_kva, w_ukv, hg_norm, w_pa, w_pb, w_out, g_mem, w_mk, w_mv, g_x, w_xq,
                   w_xo, g_ffn, w_router, b_router, w_gu, b_gu, w_down, b_down):
    row = lambda v: v.reshape(1, -1).astype(F32)
    wi = w_in[l]
    o_qa = 4 * HG_WIDTH
    o_kv = o_qa + Q_LORA
    o_kr = o_kv + KV_LORA
    o_g = o_kr + QK_ROPE
    w_kr = wi[:, o_kr:o_g]
    zk = jnp.zeros((D_MODEL, LANES - QK_ROPE), F32)
    uq = w_uq[l].reshape(Q_LORA, MLA_HEADS, QK_NOPE + QK_ROPE)
    zq = lambda w: jnp.zeros((Q_LORA, MLA_HEADS, w), F32)
    pad = HEAD_PAD - QK_NOPE - QK_ROPE
    w_q = jnp.concatenate([uq, zq(pad)], axis=-1).reshape(Q_LORA, MLA_HEADS * HEAD_PAD)
    w_qr = jnp.concatenate([zq(QK_NOPE), _rot_cols(uq[..., QK_NOPE:]), zq(pad)], axis=-1).reshape(Q_LORA, MLA_HEADS * HEAD_PAD)
    ukv = w_ukv[l].reshape(KV_LORA, MLA_HEADS, QK_NOPE + V_HEAD)
    w_kn = jnp.concatenate([ukv[..., :QK_NOPE], jnp.zeros((KV_LORA, MLA_HEADS, HEAD_PAD - QK_NOPE), F32)], axis=-1)
    e_head = jnp.concatenate([jnp.zeros((QK_ROPE, QK_NOPE), F32), jnp.eye(QK_ROPE, dtype=F32),
                              jnp.zeros((QK_ROPE, pad), F32)], axis=-1)
    e_kpe = jnp.concatenate([jnp.tile(e_head, (1, MLA_HEADS)), jnp.zeros((LANES - QK_ROPE, MLA_HEADS * HEAD_PAD), F32)], axis=0)
    bf = lambda w: w.astype(BF16)
    return {
        "lb": row(lb[l]), "g_mix": row(g_mix[l]), "w_hg": bf(wi[:, :o_qa]), "w_g": bf(wi[:, o_g:]),
        "w_qa": bf(wi[:, o_qa:o_kv]), "w_kv": bf(wi[:, o_kv:o_kr]),
        "w_kr": bf(jnp.concatenate([w_kr, zk, _rot_cols(w_kr), zk], axis=-1)),
        "g_qa": row(g_qa[l]), "w_q": bf(w_q), "w_qr": bf(w_qr), "g_kva": row(g_kva[l]),
        "w_kn": bf(w_kn.reshape(KV_LORA, MLA_HEADS * HEAD_PAD)), "w_v": bf(ukv[..., QK_NOPE:].reshape(KV_LORA, MLA_WIDTH)),
        "e_kpe": bf(e_kpe), "hg_norm": row(hg_norm[l]), "w_pa": bf(w_pa[l]), "w_pb": bf(w_pb[l]), "w_out": bf(w_out[l]),
        "g_mem": row(g_mem[l]), "w_mk": bf(w_mk[l]), "w_mv": bf(w_mv[l]), "g_x": row(g_x[l]), "w_xq": bf(w_xq[l]),
        "w_xo": bf(w_xo[l]), "g_ffn": row(g_ffn[l]),
        "w_router": bf(jnp.pad(w_router[l], ((0, 0), (0, LANES - N_EXPERTS)))),
        "b_router": jnp.pad(row(b_router[l]), ((0, 0), (0, LANES - N_EXPERTS))),
        "w_gu": w_gu[l], "b_gu": b_gu[l].reshape(N_EXPERTS, 1, 2 * D_FF).astype(F32),
        "w_down": w_down[l], "b_down": b_down[l].reshape(N_EXPERTS, 1, D_MODEL).astype(F32),
    }


def _tile(n, want):
    t = min(n, want)
    while n % t:
        t -= 8
    assert t > 0 and n % t == 0, (n, want)
    return t


def _group_front(x, lw, tabs, seq_len, hg_state, past):
    b, l, _ = x.shape
    t = b * l
    tm = _tile(t, TOKEN_TILE)
    hg, sg, q, ckv, kpe = _inproj(x.reshape(t, D_MODEL), lw, tabs, tm)
    lt = _tile(l, 512)
    a, s_new = _hgrn(hg.reshape(b, l, 4 * HG_WIDTH), lw["lb"], lw["hg_norm"], hg_state, lt, _tile(lt, 128))
    tq = _tile(l, ATTN_TILE)
    if past is None:
        lk_true, q_off = l, 0
        ckv_all, kpe_all = ckv, kpe
        lk = l
    else:
        past_ckv, past_kpe = past
        q_off = past_ckv.shape[1]
        lk_true = q_off + l
        lk = -(-lk_true // ATTN_TILE) * ATTN_TILE
        padr = ((0, 0), (0, lk - lk_true), (0, 0))
        ckv_all = jnp.pad(jnp.concatenate([past_ckv, ckv.reshape(b, l, KV_LORA)], axis=1), padr).reshape(b * lk, KV_LORA)
        past_kpe = jnp.pad(past_kpe, ((0, 0), (0, 0), (0, LANES - QK_ROPE)))
        kpe_all = jnp.pad(jnp.concatenate([past_kpe, kpe.reshape(b, l, LANES)], axis=1), padr).reshape(b * lk, LANES)
    k_arr, v_arr = _kvbuild(ckv_all, kpe_all, lw, _tile(b * lk, TOKEN_TILE))
    bo = _attention(q.reshape(b, l, MLA_HEADS * HEAD_PAD), k_arr.reshape(b, lk, MLA_HEADS * HEAD_PAD),
                    v_arr.reshape(b, lk, MLA_WIDTH), lk_true, q_off, tq, _tile(lk, ATTN_TILE))
    return a.reshape(t, HG_WIDTH), bo.reshape(t, MLA_WIDTH), sg, ckv.reshape(b, l, KV_LORA), kpe[:, :QK_ROPE].reshape(b, l, QK_ROPE), s_new


def kernel(x_prompt, x_sample, cache_mla_ckv, cache_mla_kpe, state_hgrn, cache_mem_k, cache_mem_v, mem_prompt, hg_lb_logits, g_mix, w_in, g_qa, w_uq, g_kva, w_ukv, hg_norm, w_pa, w_pb, w_out, g_mem, w_mk, w_mv, g_x, w_xq, w_xo, g_ffn, w_router, b_router, w_gu, b_gu, w_down, b_down, g_final):
    bp, lp, _ = x_prompt.shape
    bs, ls, _ = x_sample.shape
    depth = w_in.shape[0]
    past_len = cache_mla_ckv.shape[2]
    n_mem = mem_prompt.shape[1]
    tp, ts = bp * lp, bs * ls
    t_all = tp + ts
    tm_p, tm_s = _tile(tp, TOKEN_TILE), _tile(ts, TOKEN_TILE)
    assert tp % tm_s == 0
    lb_all = jnp.cumsum(jax.nn.softmax(hg_lb_logits.astype(F32), axis=0), axis=0)
    tabs_p = _rope_tables(jnp.arange(lp, dtype=I32), max(1, tm_p // lp))
    tabs_s = _rope_tables(past_len + jnp.arange(ls, dtype=I32), max(1, tm_s // ls))
    g_fin = g_final.reshape(1, D_MODEL).astype(F32)

    n_asg = t_all * TOP_K
    bm = MOE_ROWS
    n_blocks = (n_asg + N_EXPERTS * (bm - 1) + bm - 1) // bm
    n_rows = n_blocks * bm

    hp, hs = x_prompt.reshape(tp, D_MODEL), x_sample.reshape(ts, D_MODEL)
    outs = [[] for _ in range(8)]
    for l in range(depth):
        lw = _layer_weights(l, lb_all, g_mix, w_in, g_qa, w_uq, g_kva, w_ukv, hg_norm, w_pa, w_pb, w_out, g_mem, w_mk, w_mv,
                            g_x, w_xq, w_xo, g_ffn, w_router, b_router, w_gu, b_gu, w_down, b_down)
        mk, mv = _memkv(mem_prompt.reshape(bp * n_mem, D_MODEL), lw, _tile(bp * n_mem, TOKEN_TILE))
        mk, mv = mk.reshape(bp, n_mem, X_WIDTH), mv.reshape(bp, n_mem, X_WIDTH)
        zero_state = jnp.zeros((bp, HG_HEADS, HG_DK, HG_DV), F32)
        a_p, bo_p, sg_p, ckv_p, kpe_p, st_p = _group_front(hp.reshape(bp, lp, D_MODEL), lw, tabs_p, lp, zero_state, None)
        a_s, bo_s, sg_s, ckv_s, kpe_s, st_s = _group_front(hs.reshape(bs, ls, D_MODEL), lw, tabs_s, ls, state_hgrn[l],
                                                           (cache_mla_ckv[l], cache_mla_kpe[l]))
        h2_p, u3_p, lg_p = _merge(hp, a_p, bo_p, sg_p, mk, mv, lw, tm_p, lp)
        h2_s, u3_s, lg_s = _merge(hs, a_s, bo_s, sg_s, cache_mem_k[l].reshape(bs, n_mem, X_WIDTH),
                                  cache_mem_v[l].reshape(bs, n_mem, X_WIDTH), lw, tm_s, ls)
        u3 = jnp.concatenate([u3_p, u3_s], axis=0)
        idx, gate, rank, cnt = _route(jnp.concatenate([lg_p, lg_s], axis=0), _tile(t_all, TOKEN_TILE))
        counts = cnt[0, :N_EXPERTS].astype(I32)
        padded = (counts + bm - 1) // bm * bm
        pend = jnp.cumsum(padded)
        dest = ((pend - padded)[idx] + rank).reshape(-1)
        block_start = jnp.arange(n_blocks, dtype=I32) * bm
        block_expert = jnp.minimum(jnp.sum((pend[None, :] <= block_start[:, None]).astype(I32), axis=1), N_EXPERTS - 1)
        row_asg = jnp.full((n_rows,), -1, I32).at[dest].set(jnp.arange(n_asg, dtype=I32), unique_indices=True,
                                                            mode="promise_in_bounds")
        row_t, row_k = row_asg // TOP_K, row_asg % TOP_K
        pad_row = row_asg < 0
        row_src = jnp.where(pad_row, 0, row_t) * ROW_TILE
        r = jnp.arange(n_rows, dtype=I32)
        spare = n_asg + (r // bm % MOE_BUFFERS) * bm + r % bm
        row_dst = jnp.where(pad_row, spare, row_k * t_all + row_t) * ROW_TILE
        y4 = _moe(u3, block_expert, row_src, row_dst, lw, bm, (n_asg + MOE_BUFFERS * bm) * ROW_TILE)
        last = l == depth - 1
        gf = g_fin if last else jnp.ones_like(g_fin)
        yp = _final(h2_p, gate[:tp], y4, gf, tm_p, 0, t_all)
        ys = _final(h2_s, gate[tp:], y4, gf, tm_s, tp, t_all)
        assert last, "multi-layer stacking needs the un-normalised residual stream"
        hp, hs = yp, ys
        for lst, v in zip(outs, (ckv_p, kpe_p, st_p, mk.reshape(bp, n_mem, X_HEADS, X_HEAD_DIM),
                                 mv.reshape(bp, n_mem, X_HEADS, X_HEAD_DIM), ckv_s, kpe_s, st_s)):
            lst.append(v)
    stk = [jnp.stack(o) for o in outs]
    return (hp.reshape(bp, lp, D_MODEL), hs.reshape(bs, ls, D_MODEL), stk[0], stk[1], stk[2], stk[3], stk[4], stk[5], stk[6], stk[7])
```

```python
import functools
import math

import jax
import jax.numpy as jnp
from jax import lax
from jax.experimental import pallas as pl
from jax.experimental.pallas import tpu as pltpu

F32 = jnp.float32
BF16 = jnp.bfloat16
I32 = jnp.int32

D_MODEL = 1024
CHUNK = 64
EPS = 1e-6
HG_HEADS = 4
HG_DK = 128
HG_DV = 128
HG_WIDTH = HG_HEADS * HG_DV
HG_SUB = 8
MLA_HEADS = 8
Q_LORA = 384
KV_LORA = 256
QK_NOPE = 64
QK_ROPE = 32
V_HEAD = 64
MLA_WIDTH = MLA_HEADS * V_HEAD
HEAD_PAD = 128
ROPE_THETA = 10000.0
X_HEADS = 4
X_HEAD_DIM = 128
X_WIDTH = X_HEADS * X_HEAD_DIM
N_EXPERTS = 32
TOP_K = 4
D_FF = D_MODEL
SWIGLU_LIMIT = 7.0
SWIGLU_ALPHA = 1.702
LANES = 128
ROW_TILE = D_MODEL // LANES
NEG = -1e30

VMEM_LIMIT = 56 * 1024 * 1024
TOKEN_TILE = 512
ATTN_TILE = 256
MOE_ROWS = 256
MOE_BUFFERS = 3
MOE_AHEAD = MOE_BUFFERS - 1


def _dot(a, b):
    return jnp.dot(a, b, preferred_element_type=F32)


def _dot_nt(a, b):
    return lax.dot_general(a, b, (((1,), (1,)), ((), ())), preferred_element_type=F32)


def _dot_tn(a, b):
    return lax.dot_general(a, b, (((0,), (0,)), ((), ())), preferred_element_type=F32)


def _rms(x, g):
    return x * lax.rsqrt(jnp.mean(x * x, axis=-1, keepdims=True) + EPS) * g


def _sigmoid(x):
    return 1.0 / (1.0 + jnp.exp(-x))


def _const_spec(shape):
    zeros = (0,) * len(shape)
    return pl.BlockSpec(shape, lambda *_: zeros, pipeline_mode=pl.Buffered(1))


def _params(n_axes):
    return pltpu.CompilerParams(dimension_semantics=("arbitrary",) * n_axes, vmem_limit_bytes=VMEM_LIMIT)


def _inproj_kernel(x_ref, gmix_ref, whg_ref, wg_ref, wqa_ref, wkv_ref, wkr_ref, gqa_ref, wq_ref, wqr_ref, gkva_ref,
                   cosq_ref, sinq_ref, cosk_ref, sink_ref,
                   hg_ref, sg_ref, q_ref, ckv_ref, kpe_ref):
    u = _rms(x_ref[...], gmix_ref[...]).astype(BF16)
    hg_ref[...] = _dot(u, whg_ref[...])
    sg_ref[...] = _sigmoid(_dot(u, wg_ref[...])).astype(BF16)
    qn = _rms(_dot(u, wqa_ref[...]), gqa_ref[...]).astype(BF16)
    cosq = jnp.concatenate([cosq_ref[...]] * MLA_HEADS, axis=-1)
    sinq = jnp.concatenate([sinq_ref[...]] * MLA_HEADS, axis=-1)
    scale = (QK_NOPE + QK_ROPE) ** -0.5
    q = (_dot(qn, wq_ref[...]) * cosq + _dot(qn, wqr_ref[...]) * sinq) * scale
    q_ref[...] = q.astype(BF16)
    ckv_ref[...] = _rms(_dot(u, wkv_ref[...]), gkva_ref[...])
    pk = _dot(u, wkr_ref[...])
    kpe_ref[...] = pk[:, :LANES] * cosk_ref[...] + pk[:, LANES:] * sink_ref[...]


def _inproj(x, lw, tabs, tm):
    t = x.shape[0]
    nt = tabs["cosq"].shape[0] // tm
    tok = lambda i: (i, 0)
    tab = lambda i: (i % nt, 0)
    consts = [lw["g_mix"], lw["w_hg"], lw["w_g"], lw["w_qa"], lw["w_kv"], lw["w_kr"], lw["g_qa"], lw["w_q"], lw["w_qr"],
              lw["g_kva"]]
    in_specs = ([pl.BlockSpec((tm, D_MODEL), tok)] + [_const_spec(c.shape) for c in consts]
                + [pl.BlockSpec((tm, LANES), tab)] * 4)
    out_shape = (jax.ShapeDtypeStruct((t, 4 * HG_WIDTH), F32), jax.ShapeDtypeStruct((t, 2 * D_MODEL), BF16),
                 jax.ShapeDtypeStruct((t, MLA_HEADS * HEAD_PAD), BF16), jax.ShapeDtypeStruct((t, KV_LORA), F32),
                 jax.ShapeDtypeStruct((t, LANES), F32))
    out_specs = tuple(pl.BlockSpec((tm, s.shape[1]), tok) for s in out_shape)
    return pl.pallas_call(
        _inproj_kernel, grid=(t // tm,), in_specs=in_specs, out_specs=out_specs, out_shape=out_shape,
        compiler_params=_params(1), name="inproj",
    )(x, *consts, tabs["cosq"], tabs["sinq"], tabs["cosk"], tabs["sink"])


def _kvbuild_kernel(ckv_ref, kpe_ref, wkn_ref, wv_ref, e_ref, k_ref, v_ref):
    c = ckv_ref[...].astype(BF16)
    k_ref[...] = (_dot(c, wkn_ref[...]) + _dot(kpe_ref[...].astype(BF16), e_ref[...])).astype(BF16)
    v_ref[...] = _dot(c, wv_ref[...]).astype(BF16)


def _kvbuild(ckv, kpe, lw, tm):
    n = ckv.shape[0]
    tok = lambda i: (i, 0)
    consts = [lw["w_kn"], lw["w_v"], lw["e_kpe"]]
    out_shape = (jax.ShapeDtypeStruct((n, MLA_HEADS * HEAD_PAD), BF16), jax.ShapeDtypeStruct((n, MLA_WIDTH), BF16))
    return pl.pallas_call(
        _kvbuild_kernel, grid=(n // tm,),
        in_specs=[pl.BlockSpec((tm, KV_LORA), tok), pl.BlockSpec((tm, LANES), tok)] + [_const_spec(c.shape) for c in consts],
        out_specs=tuple(pl.BlockSpec((tm, s.shape[1]), tok) for s in out_shape), out_shape=out_shape,
        compiler_params=_params(1), name="kvbuild",
    )(ckv, kpe, *consts)


def _attn_tile(q_ref, k_ref, v_ref, o_ref, *, q0, tq, tk, lk_true):
    lim_first = min((q0 // CHUNK + 1) * CHUNK, lk_true)
    lim_last = min(((q0 + tq - 1) // CHUNK + 1) * CHUNK, lk_true)
    n_keys = -(-lim_last // tk) * tk
    n_open = lim_first // tk * tk
    outs = []
    for h in range(2):
        hs = slice(h * HEAD_PAD, (h + 1) * HEAD_PAD)
        s = _dot_nt(q_ref[:, hs], k_ref[0:n_keys, hs])
        if n_open < n_keys:
            edge = s[:, n_open:]
            kpos = n_open + lax.broadcasted_iota(I32, edge.shape, 1)
            qpos = q0 + lax.broadcasted_iota(I32, edge.shape, 0)
            edge = jnp.where(kpos < jnp.minimum((qpos // CHUNK + 1) * CHUNK, lk_true), edge, NEG)
            s = edge if n_open == 0 else jnp.concatenate([s[:, :n_open], edge], axis=-1)
        p = jnp.exp(s - jnp.max(s, axis=-1, keepdims=True))
        outs.append(_dot(p.astype(BF16), v_ref[0:n_keys, :]) / jnp.sum(p, axis=-1, keepdims=True))
    lane = lax.broadcasted_iota(I32, outs[0].shape, 1)
    o_ref[...] = jnp.where(lane < V_HEAD, outs[0], outs[1]).astype(BF16)


def _attn_kernel(q_ref, k_ref, v_ref, o_ref, *, n_q, tq, tk, lk_true, q_off):
    i = pl.program_id(2)
    for qi in range(n_q):
        @pl.when(i == qi)
        def _(qi=qi):
            _attn_tile(q_ref, k_ref, v_ref, o_ref, q0=q_off + qi * tq, tq=tq, tk=tk, lk_true=lk_true)


def _attention(q, k, v, lk_true, q_off, tq, tk):
    b, lq, _ = q.shape
    lk = k.shape[1]
    kern = functools.partial(_attn_kernel, n_q=lq // tq, tq=tq, tk=tk, lk_true=lk_true, q_off=q_off)
    return pl.pallas_call(
        kern, grid=(b, MLA_HEADS // 2, lq // tq),
        in_specs=[pl.BlockSpec((None, tq, 2 * HEAD_PAD), lambda bi, hp, i: (bi, i, hp)),
                  pl.BlockSpec((None, lk, 2 * HEAD_PAD), lambda bi, hp, i: (bi, 0, hp)),
                  pl.BlockSpec((None, lk, 2 * V_HEAD), lambda bi, hp, i: (bi, 0, hp))],
        out_specs=pl.BlockSpec((None, tq, 2 * V_HEAD), lambda bi, hp, i: (bi, i, hp)),
        out_shape=jax.ShapeDtypeStruct((b, lq, MLA_WIDTH), BF16),
        compiler_params=_params(3), name="attn",
    )(q, k, v)


def _hgrn_kernel(q_ref, f_ref, i_ref, g_ref, lb_ref, norm_ref, s0_ref, a_ref, sout_ref, st_sc, *, chunk, n_chunks):
    t = pl.program_id(1)

    @pl.when(t == 0)
    def _():
        for h in range(HG_HEADS):
            st_sc[h] = s0_ref[h].T

    row = lax.broadcasted_iota(I32, (chunk, chunk), 0)
    col = lax.broadcasted_iota(I32, (chunk, chunk), 1)
    tri = jnp.where(row >= col, 1.0, 0.0).astype(BF16)
    nsub = chunk // HG_SUB
    sub_row = lax.broadcasted_iota(I32, (nsub, HG_SUB, 1), 1)
    same_block = {}
    m = chunk // 2
    while m >= HG_SUB:
        same_block[2 * m] = jnp.where(row // (2 * m) == col // (2 * m), 1.0, 0.0)
        m //= 2

    def head_chunk(r0, h):
        hs = slice(h * HG_DK, (h + 1) * HG_DK)
        lb = lb_ref[:, hs]
        q = q_ref[pl.ds(r0, chunk), hs]
        v = i_ref[pl.ds(r0, chunk), hs]
        f = lb + (1.0 - lb) * _sigmoid(f_ref[pl.ds(r0, chunk), hs])
        lf = jnp.log(f)
        k = 1.0 - f
        hi = lf.astype(BF16)
        r1 = lf - hi.astype(F32)
        mid = r1.astype(BF16)
        lo = (r1 - mid.astype(F32)).astype(BF16)
        b = _dot(tri, hi) + _dot(tri, mid) + _dot(tri, lo)
        st = st_sc[h]
        o = _dot_nt((q * jnp.exp(b)).astype(BF16), st.astype(BF16))
        a_cross = jnp.zeros((chunk, chunk), F32)
        m = chunk // 2
        while m >= HG_SUB:
            blk = 2 * m
            shp = (chunk // blk, blk, HG_DK)
            b3 = b.reshape(shp)
            ref = b3[:, m - 1:m, :]
            first = lax.broadcasted_iota(I32, (chunk // blk, blk, 1), 1) < m
            decay = jnp.exp(-jnp.abs(b3 - ref))
            ql = jnp.where(first, 0.0, q.reshape(shp) * decay)
            kl = jnp.where(first, k.reshape(shp) * decay, 0.0)
            a_l = _dot_nt(ql.reshape(chunk, HG_DK).astype(BF16), kl.reshape(chunk, HG_DK).astype(BF16))
            a_cross = a_cross + a_l * same_block[blk]
            m //= 2
        o = o + _dot(a_cross.astype(BF16), v.astype(BF16))
        shp = (nsub, HG_SUB, HG_DK)
        q3, k3, b3, v3 = q.reshape(shp), k.reshape(shp), b.reshape(shp), v.reshape(shp)
        od = jnp.zeros(shp, F32)
        for s in range(HG_SUB):
            w = q3 * jnp.exp(b3 - b3[:, s:s + 1, :]) * k3[:, s:s + 1, :]
            a_col = jnp.where(sub_row >= s, jnp.sum(w, axis=-1, keepdims=True), 0.0)
            od = od + a_col * v3[:, s:s + 1, :]
        o = o + od.reshape(chunk, HG_DV)
        b_last = b[chunk - 1:chunk, :]
        kd = k * jnp.exp(b_last - b)
        st_sc[h] = st * jnp.exp(b_last) + _dot_tn(v.astype(BF16), kd.astype(BF16))
        g = g_ref[pl.ds(r0, chunk), hs]
        a_ref[pl.ds(r0, chunk), hs] = (_rms(o, norm_ref[:, hs]) * (g * _sigmoid(g))).astype(BF16)

    def body(c, carry):
        r0 = pl.multiple_of(c * chunk, chunk)
        for h in range(HG_HEADS):
            head_chunk(r0, h)
        return carry

    lax.fori_loop(0, n_chunks, body, 0)

    @pl.when(t == pl.num_programs(1) - 1)
    def _():
        for h in range(HG_HEADS):
            sout_ref[h] = st_sc[h].T


def _hgrn(hg, lb, norm, s0, lt, chunk):
    b, l, _ = hg.shape
    kern = functools.partial(_hgrn_kernel, chunk=chunk, n_chunks=lt // chunk)
    seg = lambda j: pl.BlockSpec((None, lt, HG_WIDTH), lambda bi, t: (bi, t, j))
    head_vec = _const_spec((1, HG_WIDTH))
    state = pl.BlockSpec((None, HG_HEADS, HG_DK, HG_DV), lambda bi, t: (bi, 0, 0, 0))
    return pl.pallas_call(
        kern, grid=(b, l // lt),
        in_specs=[seg(0), seg(1), seg(2), seg(3), head_vec, head_vec, state],
        out_specs=(pl.BlockSpec((None, lt, HG_WIDTH), lambda bi, t: (bi, t, 0)), state),
        out_shape=(jax.ShapeDtypeStruct((b, l, HG_WIDTH), BF16), jax.ShapeDtypeStruct(s0.shape, F32)),
        scratch_shapes=[pltpu.VMEM((HG_HEADS, HG_DV, HG_DK), F32)],
        compiler_params=_params(2), name="hgrn",
    )(hg, hg, hg, hg, lb, norm, s0)


def _memkv_kernel(m_ref, g_ref, wk_ref, wv_ref, k_ref, v_ref):
    m = _rms(m_ref[...], g_ref[...]).astype(BF16)
    k_ref[...] = _dot(m, wk_ref[...])
    v_ref[...] = _dot(m, wv_ref[...])


def _memkv(mem, lw, tm):
    n = mem.shape[0]
    tok = lambda i: (i, 0)
    consts = [lw["g_mem"], lw["w_mk"], lw["w_mv"]]
    out_shape = (jax.ShapeDtypeStruct((n, X_WIDTH), F32),) * 2
    return pl.pallas_call(
        _memkv_kernel, grid=(n // tm,),
        in_specs=[pl.BlockSpec((tm, D_MODEL), tok)] + [_const_spec(c.shape) for c in consts],
        out_specs=(pl.BlockSpec((tm, X_WIDTH), tok),) * 2, out_shape=out_shape,
        compiler_params=_params(1), name="memkv",
    )(mem, *consts)


def _merge_kernel(x_ref, a_ref, bo_ref, sg_ref, mk_ref, mv_ref, wpa_ref, wpb_ref, wout_ref, gx_ref, wxq_ref, wxo_ref,
                  gffn_ref, wr_ref, br_ref, *rest, n_seg, seg, n_tiles):
    h_ref, u_ref, lg_ref = rest[-3:]

    @pl.when(pl.program_id(0) >= n_tiles)
    def _():
        u_ref[...] = jnp.zeros(u_ref.shape, F32)
        lg_ref[...] = jnp.zeros(lg_ref.shape, F32)

    @pl.when(pl.program_id(0) < n_tiles)
    def _():
        _merge_tile(x_ref, a_ref, bo_ref, sg_ref, mk_ref, mv_ref, wpa_ref, wpb_ref, wout_ref, gx_ref, wxq_ref, wxo_ref,
                    gffn_ref, wr_ref, br_ref, h_ref, u_ref, lg_ref, n_seg=n_seg, seg=seg)


def _merge_tile(x_ref, a_ref, bo_ref, sg_ref, mk_ref, mv_ref, wpa_ref, wpb_ref, wout_ref, gx_ref, wxq_ref, wxo_ref,
                gffn_ref, wr_ref, br_ref, h_ref, u_ref, lg_ref, *, n_seg, seg):
    sg = sg_ref[...]
    mix = (sg[:, :D_MODEL].astype(F32) * _dot(a_ref[...], wpa_ref[...])
           + sg[:, D_MODEL:].astype(F32) * _dot(bo_ref[...], wpb_ref[...]))
    h1 = x_ref[...] + _dot(mix.astype(BF16), wout_ref[...])
    u2 = _rms(h1, gx_ref[...]).astype(BF16)
    qx = (_dot(u2, wxq_ref[...]) * X_HEAD_DIM ** -0.5).astype(BF16)
    segs = []
    for j in range(n_seg):
        kj = mk_ref[j].astype(BF16)
        vj = mv_ref[j].astype(BF16)
        heads = []
        for h in range(X_HEADS):
            hs = slice(h * X_HEAD_DIM, (h + 1) * X_HEAD_DIM)
            s = _dot_nt(qx[j * seg:(j + 1) * seg, hs], kj[:, hs])
            p = jnp.exp(s - jnp.max(s, axis=-1, keepdims=True))
            heads.append(_dot(p.astype(BF16), vj[:, hs]) / jnp.sum(p, axis=-1, keepdims=True))
        segs.append(jnp.concatenate(heads, axis=-1))
    ox = segs[0] if n_seg == 1 else jnp.concatenate(segs, axis=0)
    h2 = h1 + _dot(ox.astype(BF16), wxo_ref[...])
    h_ref[...] = h2
    u3 = _rms(h2, gffn_ref[...])
    _to_row_tiles(u_ref, u3)
    lg_ref[...] = _dot(u3.astype(BF16), wr_ref[...]) + br_ref[...]


def _merge(x, a, bo, sg, mk, mv, lw, tm, seq_len, t_all, shared=None, tok_off=0):
    t = x.shape[0]
    n_mem = mk.shape[1]
    n_seg = max(1, tm // seq_len)
    seg = tm // n_seg
    tiles_per_seq = max(1, seq_len // tm)
    n_tiles = t // tm
    assert t_all % tm == 0 and tok_off % tm == 0
    n_steps = n_tiles if shared is not None else t_all // tm
    assert shared is not None or tok_off == 0
    tok = lambda i: (jnp.minimum(i, n_tiles - 1), 0)
    out_tok = lambda i: (tok_off // tm + i, 0)
    mem = pl.BlockSpec((n_seg, n_mem, X_WIDTH), lambda i: (jnp.minimum(i, n_tiles - 1) // tiles_per_seq, 0, 0))
    consts = [lw["w_pa"], lw["w_pb"], lw["w_out"], lw["g_x"], lw["w_xq"], lw["w_xo"], lw["g_ffn"], lw["w_router"],
              lw["b_router"]]
    out_shape = (jax.ShapeDtypeStruct((t, D_MODEL), F32), jax.ShapeDtypeStruct((t_all * ROW_TILE, LANES), F32),
                 jax.ShapeDtypeStruct((t_all, LANES), F32))
    out_specs = (pl.BlockSpec((tm, D_MODEL), tok), pl.BlockSpec((tm * ROW_TILE, LANES), out_tok),
                 pl.BlockSpec((tm, LANES), out_tok))
    in_specs = ([pl.BlockSpec((tm, D_MODEL), tok), pl.BlockSpec((tm, HG_WIDTH), tok), pl.BlockSpec((tm, MLA_WIDTH), tok),
                 pl.BlockSpec((tm, 2 * D_MODEL), tok), mem, mem] + [_const_spec(c.shape) for c in consts])
    args = [x, a, bo, sg, mk, mv, *consts]
    aliases = {}
    if shared is not None:
        aliases = {len(args): 1, len(args) + 1: 2}
        in_specs += [pl.BlockSpec(memory_space=pl.ANY)] * 2
        args += list(shared)
    kern = functools.partial(_merge_kernel, n_seg=n_seg, seg=seg, n_tiles=n_tiles)
    return pl.pallas_call(
        kern, grid=(n_steps,), in_specs=in_specs, out_specs=out_specs, out_shape=out_shape, input_output_aliases=aliases,
        compiler_params=_params(1), name="merge",
    )(*args)


def _route_kernel(lg_ref, gate_ref, ir_ref, cum_ref, cnt_ref, carry_sc, *, tr):
    i = pl.program_id(0)

    @pl.when(i == 0)
    def _():
        carry_sc[...] = jnp.zeros(carry_sc.shape, F32)

    lane_i = lax.broadcasted_iota(I32, (tr, LANES), 1)
    lane = lane_i.astype(F32)
    logit = jnp.where(lane < N_EXPERTS, lg_ref[...], -jnp.inf)
    vals, idxs, hots = [], [], []
    for _ in range(TOP_K):
        top = jnp.max(logit, axis=-1, keepdims=True)
        first = jnp.min(jnp.where(logit == top, lane, float(LANES)), axis=-1, keepdims=True)
        hot = lane == first
        vals.append(top)
        idxs.append(first)
        hots.append(hot)
        logit = jnp.where(hot, -jnp.inf, logit)
    exps = [jnp.exp(v - vals[0]) for v in vals]
    denom = exps[0] + exps[1] + exps[2] + exps[3]
    hot_all = sum(jnp.where(h, 1.0, 0.0) for h in hots)
    row = lax.broadcasted_iota(I32, (tr, tr), 0)
    col = lax.broadcasted_iota(I32, (tr, tr), 1)
    before = jnp.where(row > col, 1.0, 0.0).astype(BF16)
    carry = carry_sc[...]
    prefix = _dot(before, hot_all.astype(BF16)) + carry
    lane4 = lax.broadcasted_iota(I32, (tr, TOP_K), 1)
    gate = jnp.zeros((tr, TOP_K), F32)
    cols = jnp.zeros((tr, LANES), F32)
    for k in range(TOP_K):
        rk = jnp.sum(jnp.where(hots[k], prefix, 0.0), axis=-1, keepdims=True)
        gate = jnp.where(lane4 == k, exps[k] / denom, gate)
        cols = jnp.where(lane_i == k, idxs[k], jnp.where(lane_i == TOP_K + k, rk, cols))
    gate_ref[...] = gate
    ir_ref[i] = cols.T[0:2 * TOP_K, :]
    cum_ref[pl.ds(i, 1), :] = carry
    carry = carry + jnp.sum(hot_all, axis=0, keepdims=True)
    carry_sc[...] = carry
    cnt_ref[...] = carry


def _route(logits, tr):
    t = logits.shape[0]
    n_tiles = t // tr
    tok = lambda i: (i, 0)
    out_shape = (jax.ShapeDtypeStruct((t, TOP_K), F32), jax.ShapeDtypeStruct((n_tiles, 2 * TOP_K, tr), F32),
                 jax.ShapeDtypeStruct((n_tiles, LANES), F32), jax.ShapeDtypeStruct((1, LANES), F32))
    return pl.pallas_call(
        functools.partial(_route_kernel, tr=tr), grid=(n_tiles,),
        in_specs=[pl.BlockSpec((tr, LANES), tok)],
        out_specs=(pl.BlockSpec((tr, TOP_K), tok), pl.BlockSpec((n_tiles, 2 * TOP_K, tr), lambda i: (0, 0, 0)),
                   pl.BlockSpec((n_tiles, LANES), lambda i: (0, 0)), pl.BlockSpec((1, LANES), lambda i: (0, 0))),
        out_shape=out_shape, scratch_shapes=[pltpu.VMEM((1, LANES), F32)],
        compiler_params=_params(1), name="route",
    )(logits)


def _rowlist_kernel(be_ref, j0_ref, tlo_ref, thi_ref, ir_ref, out_ref, *, tr, bm, group):
    g = pl.program_id(0)
    n_fold = tr // LANES
    tl = lax.broadcasted_iota(I32, (1, tr), 1)
    lo_code = (tl % 256 + 1).astype(F32).astype(BF16)
    hi_code = (tl // 256 * TOP_K + 1).astype(F32)
    jrow = lax.broadcasted_iota(I32, (bm, tr), 0).astype(F32).astype(BF16)
    kcol = lax.broadcasted_iota(I32, (TOP_K, tr), 0).astype(F32)
    lane = lax.broadcasted_iota(I32, (bm, LANES), 1)
    zero = jnp.zeros((), BF16)

    def fold(x):
        parts = [x[:, c * LANES:(c + 1) * LANES] for c in range(n_fold)]
        while len(parts) > 1:
            parts = [jnp.maximum(parts[c], parts[c + 1]) for c in range(0, len(parts), 2)]
        return parts[0]

    for q in range(group):
        b = g * group + q
        e = be_ref[b].astype(F32)
        j0 = j0_ref[b].astype(F32)

        def tile(t, planes, e=e, j0=j0):
            p_lo, p_hk, p_tile = planes
            ir = ir_ref[t]
            mine = ir[0:TOP_K, :] == e
            rel = jnp.sum(jnp.where(mine, ir[TOP_K:2 * TOP_K, :] - j0, 0.0), axis=0, keepdims=True)
            hit = jnp.sum(jnp.where(mine, 1.0, 0.0), axis=0, keepdims=True) > 0.0
            slot = jnp.sum(jnp.where(mine, kcol, 0.0), axis=0, keepdims=True)
            rel = jnp.where(hit, jnp.clip(rel, -1.0, float(bm)), -1.0).astype(BF16)
            match = jnp.broadcast_to(rel, (bm, tr)) == jrow
            f_lo = fold(jnp.where(match, jnp.broadcast_to(lo_code, (bm, tr)), zero))
            f_hk = fold(jnp.where(match, jnp.broadcast_to((hi_code + slot).astype(BF16), (bm, tr)), zero))
            tile_no = jnp.broadcast_to(jnp.full((1, LANES), t + 1, I32).astype(F32).astype(BF16), (bm, LANES))
            return jnp.maximum(p_lo, f_lo), jnp.maximum(p_hk, f_hk), jnp.where(f_lo > zero, tile_no, p_tile)

        empty = jnp.zeros((bm, LANES), BF16)
        p_lo, p_hk, p_tile = lax.fori_loop(tlo_ref[b], thi_ref[b], tile, (empty, empty, empty))
        lo = jnp.max(p_lo.astype(F32), axis=-1, keepdims=True)
        hk = jnp.max(p_hk.astype(F32), axis=-1, keepdims=True) - 1.0
        tile_no = jnp.max(p_tile.astype(F32), axis=-1, keepdims=True) - 1.0
        found = lo > 0.0
        hi = jnp.floor(hk * (1.0 / TOP_K))
        tok = jnp.where(found, tile_no * float(tr) + hi * 256.0 + lo - 1.0, 0.0)
        k = jnp.where(found, hk - hi * TOP_K, 0.0)
        cols = jnp.where(lane == 0, tok, jnp.where(lane == 1, k, jnp.where(lane == 2, jnp.where(found, 1.0, 0.0), 0.0)))
        out_ref[q] = cols.T[0:ROW_TILE, :].astype(I32)


def _rowlist(ir, block_expert, block_j0, tile_lo, tile_hi, bm):
    n_tiles, _, tr = ir.shape
    n_blocks = block_expert.shape[0]
    assert n_tiles < 256 and bm <= 256 and tr <= 8192
    group = 8
    while n_blocks % group:
        group //= 2
    grid_spec = pltpu.PrefetchScalarGridSpec(
        num_scalar_prefetch=4, grid=(n_blocks // group,),
        in_specs=[pl.BlockSpec((n_tiles, 2 * TOP_K, tr), lambda g, *_: (0, 0, 0))],
        out_specs=pl.BlockSpec((group, ROW_TILE, bm), lambda g, *_: (g, 0, 0)))
    return pl.pallas_call(
        functools.partial(_rowlist_kernel, tr=tr, bm=bm, group=group), grid_spec=grid_spec,
        out_shape=jax.ShapeDtypeStruct((n_blocks, ROW_TILE, bm), I32),
        compiler_params=_params(1), name="rowlist",
    )(block_expert, block_j0, tile_lo, tile_hi, ir)


def _to_row_tiles(ref, x):
    n = x.shape[0]
    for c in range(ROW_TILE):
        ref[pl.ds(c, n, stride=ROW_TILE), :] = x[:, c * LANES:(c + 1) * LANES]


def _row_tile_chunk(ref, n, c):
    return ref[pl.ds(c, n, stride=ROW_TILE), :]


def _moe_kernel(be_ref, *refs, bm, n_blocks):
    rt_first = refs[:MOE_AHEAD]
    rtn_ref, rdp_ref, x_hbm, wgu_ref, bgu_ref, wd_ref, bd_ref, y_hbm = refs[MOE_AHEAD:MOE_AHEAD + 8]
    scratch = refs[MOE_AHEAD + 8:]
    xs, ys = scratch[:MOE_BUFFERS], scratch[MOE_BUFFERS:2 * MOE_BUFFERS]
    wgu_bf, wd_bf, gsem, ssem = scratch[2 * MOE_BUFFERS:]
    b = pl.program_id(0)
    rows = bm * ROW_TILE

    def gather(idx_ref, r, slot):
        src = pl.multiple_of(idx_ref[0, 0, r], ROW_TILE)
        return pltpu.make_async_copy(x_hbm.at[pl.ds(src, ROW_TILE), :], xs[slot].at[pl.ds(r * ROW_TILE, ROW_TILE), :],
                                     gsem.at[slot])

    def scatter(r, slot):
        dst = pl.multiple_of(rdp_ref[0, 0, r], ROW_TILE)
        return pltpu.make_async_copy(ys[slot].at[pl.ds(r * ROW_TILE, ROW_TILE), :], y_hbm.at[pl.ds(dst, ROW_TILE), :],
                                     ssem.at[slot])

    def wait_gathers(slot):
        pltpu.make_async_copy(x_hbm.at[pl.ds(0, rows), :], xs[slot], gsem.at[slot]).wait()

    def wait_scatters(slot):
        pltpu.make_async_copy(ys[slot], y_hbm.at[pl.ds(0, rows), :], ssem.at[slot]).wait()

    @pl.when(b == 0)
    def _():
        ys[MOE_BUFFERS - 1][...] = jnp.zeros(ys[MOE_BUFFERS - 1].shape, F32)
        for j in range(MOE_AHEAD):
            def start(r, c, j=j):
                gather(rt_first[j], r, j).start()
                return c

            lax.fori_loop(0, bm, start, 0)

    def step(slot):
        prev = (slot - 1) % MOE_BUFFERS
        wait_gathers(slot)

        @pl.when(b >= MOE_BUFFERS - 1)
        def _():
            wait_scatters(slot)

        for r in range(bm):
            gather(rtn_ref, r, prev).start()
        for r in range(bm):
            scatter(r, prev).start(priority=1)
        x = jnp.concatenate([_row_tile_chunk(xs[slot], bm, c) for c in range(ROW_TILE)], axis=-1).astype(BF16)
        gu = _dot(x, wgu_bf[...]) + bgu_ref[...]
        g = jnp.minimum(gu[:, :D_FF], SWIGLU_LIMIT)
        up = jnp.clip(gu[:, D_FF:], -SWIGLU_LIMIT, SWIGLU_LIMIT)
        act = (up + 1.0) * g * _sigmoid(SWIGLU_ALPHA * g)
        _to_row_tiles(ys[slot], _dot(act.astype(BF16), wd_bf[...]) + bd_ref[...])

    expert = be_ref[jnp.minimum(b, n_blocks - 1)]
    new_expert = jnp.logical_or(b == 0, expert != be_ref[jnp.maximum(b - 1, 0)])

    @pl.when(jnp.logical_and(b < n_blocks, new_expert))
    def _():
        wgu_bf[...] = wgu_ref[...].astype(BF16)
        wd_bf[...] = wd_ref[...].astype(BF16)

    for v in range(MOE_BUFFERS):
        @pl.when(jnp.logical_and(b < n_blocks, b % MOE_BUFFERS == v))
        def _(v=v):
            step(v)

    @pl.when(b == n_blocks)
    def _():
        for j in range(MOE_AHEAD):
            wait_gathers((n_blocks + j) % MOE_BUFFERS)
        for j in range(2, MOE_BUFFERS + 1):
            wait_scatters((n_blocks - j) % MOE_BUFFERS)
        last = (n_blocks - 1) % MOE_BUFFERS

        def start(r, c):
            scatter(r, last).start()
            return c

        lax.fori_loop(0, bm, start, 0)
        wait_scatters(last)


def _moe(x, block_expert, row_src, row_dst, lw, bm, n_out_rows):
    n_blocks = block_expert.shape[0]
    assert n_blocks >= MOE_BUFFERS
    spare = (n_out_rows // ROW_TILE - bm + jnp.arange(bm, dtype=I32)) * ROW_TILE
    src = row_src.reshape(n_blocks, 1, bm)
    dst_prev = jnp.concatenate([spare.reshape(1, 1, bm), row_dst.reshape(n_blocks, 1, bm)], axis=0)
    smem = lambda f: pl.BlockSpec((1, 1, bm), f, memory_space=pltpu.SMEM)
    blk = lambda b: jnp.minimum(b, n_blocks - 1)
    first = [smem(lambda b, be, j=j: (j, 0, 0)) for j in range(MOE_AHEAD)]
    grid_spec = pltpu.PrefetchScalarGridSpec(
        num_scalar_prefetch=1, grid=(n_blocks + 1,),
        in_specs=first + [smem(lambda b, be: (blk(b + MOE_AHEAD), 0, 0)), smem(lambda b, be: (b, 0, 0)),
                  pl.BlockSpec(memory_space=pl.ANY),
                  pl.BlockSpec((None, D_MODEL, 2 * D_FF), lambda b, be: (be[blk(b)], 0, 0)),
                  pl.BlockSpec((None, 1, 2 * D_FF), lambda b, be: (be[blk(b)], 0, 0)),
                  pl.BlockSpec((None, D_FF, D_MODEL), lambda b, be: (be[blk(b)], 0, 0)),
                  pl.BlockSpec((None, 1, D_MODEL), lambda b, be: (be[blk(b)], 0, 0))],
        out_specs=pl.BlockSpec(memory_space=pl.ANY),
        scratch_shapes=[pltpu.VMEM((bm * ROW_TILE, LANES), F32)] * (2 * MOE_BUFFERS)
        + [pltpu.VMEM((D_MODEL, 2 * D_FF), BF16), pltpu.VMEM((D_FF, D_MODEL), BF16)]
        + [pltpu.SemaphoreType.DMA((MOE_BUFFERS,))] * 2)
    return pl.pallas_call(
        functools.partial(_moe_kernel, bm=bm, n_blocks=n_blocks), grid_spec=grid_spec,
        out_shape=jax.ShapeDtypeStruct((n_out_rows, LANES), F32),
        compiler_params=_params(1), name="moe",
    )(block_expert, *([src] * (MOE_AHEAD + 1)), dst_prev, x, lw["w_gu"], lw["b_gu"], lw["w_down"], lw["b_down"])


def _final_kernel(h_ref, gate_ref, y0_ref, y1_ref, y2_ref, y3_ref, g_ref, o_ref, *, tm):
    gate = gate_ref[...]
    gates = [jnp.broadcast_to(gate[:, k:k + 1], (tm, LANES)) for k in range(TOP_K)]
    ssq = jnp.zeros((tm, 1), F32)
    for c in range(ROW_TILE):
        cs = slice(c * LANES, (c + 1) * LANES)
        h = h_ref[:, cs]
        for k, y_ref in enumerate((y0_ref, y1_ref, y2_ref, y3_ref)):
            h = h + gates[k] * _row_tile_chunk(y_ref, tm, c)
        o_ref[:, cs] = h
        ssq = ssq + jnp.sum(h * h, axis=-1, keepdims=True)
    o_ref[...] = o_ref[...] * lax.rsqrt(ssq * (1.0 / D_MODEL) + EPS) * g_ref[...]


def _final(h, gate, y4, g_final, tm, tok_off, t_all):
    t = h.shape[0]
    tok = lambda i: (i, 0)
    slot = lambda k: pl.BlockSpec((tm * ROW_TILE, LANES), lambda i: ((k * t_all + tok_off) // tm + i, 0))
    return pl.pallas_call(
        functools.partial(_final_kernel, tm=tm), grid=(t // tm,),
        in_specs=[pl.BlockSpec((tm, D_MODEL), tok), pl.BlockSpec((tm, TOP_K), lambda i: (tok_off // tm + i, 0)),
                  slot(0), slot(1), slot(2), slot(3), _const_spec(g_final.shape)],
        out_specs=pl.BlockSpec((tm, D_MODEL), tok), out_shape=jax.ShapeDtypeStruct((t, D_MODEL), F32),
        compiler_params=_params(1), name="final",
    )(h, gate, y4, y4, y4, y4, g_final)


def _rope_tables(pos, reps):
    half = QK_ROPE // 2
    inv_freq = jnp.exp(-math.log(ROPE_THETA) * jnp.arange(half, dtype=F32) / half)
    ang = pos.astype(F32)[:, None] * inv_freq[None, :]
    cos = jnp.concatenate([jnp.cos(ang)] * 2, axis=-1)
    sin = jnp.concatenate([jnp.sin(ang)] * 2, axis=-1)
    n = pos.shape[0]
    ones = jnp.ones((n, QK_NOPE), F32)
    z = lambda w: jnp.zeros((n, w), F32)
    tabs = {"cosq": jnp.concatenate([ones, cos, z(HEAD_PAD - QK_NOPE - QK_ROPE)], axis=-1),
            "sinq": jnp.concatenate([z(QK_NOPE), sin, z(HEAD_PAD - QK_NOPE - QK_ROPE)], axis=-1),
            "cosk": jnp.concatenate([cos, z(LANES - QK_ROPE)], axis=-1),
            "sink": jnp.concatenate([sin, z(LANES - QK_ROPE)], axis=-1)}
    return {k: jnp.tile(v, (reps, 1)) for k, v in tabs.items()}


def _rot_cols(w):
    half = QK_ROPE // 2
    return jnp.concatenate([-w[..., half:], w[..., :half]], axis=-1)


def _layer_weights(l, lb, g_mix, w_in, g_qa, w_uq, g_kva, w_ukv, hg_norm, w_pa, w_pb, w_out, g_mem, w_mk, w_mv, g_x, w_xq,
                   w_xo, g_ffn, w_router, b_router, w_gu, b_gu, w_down, b_down):
    row = lambda v: v.reshape(1, -1).astype(F32)
    wi = w_in[l]
    o_qa = 4 * HG_WIDTH
    o_kv = o_qa + Q_LORA
    o_kr = o_kv + KV_LORA
    o_g = o_kr + QK_ROPE
    w_kr = wi[:, o_kr:o_g]
    zk = jnp.zeros((D_MODEL, LANES - QK_ROPE), F32)
    uq = w_uq[l].reshape(Q_LORA, MLA_HEADS, QK_NOPE + QK_ROPE)
    zq = lambda w: jnp.zeros((Q_LORA, MLA_HEADS, w), F32)
    pad = HEAD_PAD - QK_NOPE - QK_ROPE
    w_q = jnp.concatenate([uq, zq(pad)], axis=-1).reshape(Q_LORA, MLA_HEADS * HEAD_PAD)
    w_qr = jnp.concatenate([zq(QK_NOPE), _rot_cols(uq[..., QK_NOPE:]), zq(pad)], axis=-1).reshape(Q_LORA, MLA_HEADS * HEAD_PAD)
    ukv = w_ukv[l].reshape(KV_LORA, MLA_HEADS, QK_NOPE + V_HEAD)
    w_kn = jnp.concatenate([ukv[..., :QK_NOPE], jnp.zeros((KV_LORA, MLA_HEADS, HEAD_PAD - QK_NOPE), F32)], axis=-1)
    e_head = jnp.concatenate([jnp.zeros((QK_ROPE, QK_NOPE), F32), jnp.eye(QK_ROPE, dtype=F32),
                              jnp.zeros((QK_ROPE, pad), F32)], axis=-1)
    e_kpe = jnp.concatenate([jnp.tile(e_head, (1, MLA_HEADS)), jnp.zeros((LANES - QK_ROPE, MLA_HEADS * HEAD_PAD), F32)], axis=0)
    bf = lambda w: w.astype(BF16)
    return {
        "lb": row(lb[l]), "g_mix": row(g_mix[l]), "w_hg": bf(wi[:, :o_qa]), "w_g": bf(wi[:, o_g:]),
        "w_qa": bf(wi[:, o_qa:o_kv]), "w_kv": bf(wi[:, o_kv:o_kr]),
        "w_kr": bf(jnp.concatenate([w_kr, zk, _rot_cols(w_kr), zk], axis=-1)),
        "g_qa": row(g_qa[l]), "w_q": bf(w_q), "w_qr": bf(w_qr), "g_kva": row(g_kva[l]),
        "w_kn": bf(w_kn.reshape(KV_LORA, MLA_HEADS * HEAD_PAD)), "w_v": bf(ukv[..., QK_NOPE:].reshape(KV_LORA, MLA_WIDTH)),
        "e_kpe": bf(e_kpe), "hg_norm": row(hg_norm[l]), "w_pa": bf(w_pa[l]), "w_pb": bf(w_pb[l]), "w_out": bf(w_out[l]),
        "g_mem": row(g_mem[l]), "w_mk": bf(w_mk[l]), "w_mv": bf(w_mv[l]), "g_x": row(g_x[l]), "w_xq": bf(w_xq[l]),
        "w_xo": bf(w_xo[l]), "g_ffn": row(g_ffn[l]),
        "w_router": bf(jnp.pad(w_router[l], ((0, 0), (0, LANES - N_EXPERTS)))),
        "b_router": jnp.pad(row(b_router[l]), ((0, 0), (0, LANES - N_EXPERTS))),
        "w_gu": w_gu[l], "b_gu": b_gu[l].reshape(N_EXPERTS, 1, 2 * D_FF).astype(F32),
        "w_down": w_down[l], "b_down": b_down[l].reshape(N_EXPERTS, 1, D_MODEL).astype(F32),
    }


def _tile(n, want):
    t = min(n, want)
    while n % t:
        t -= 8
    assert t > 0 and n % t == 0, (n, want)
    return t


def _group_front(x, lw, tabs, seq_len, hg_state, past):
    b, l, _ = x.shape
    t = b * l
    tm = _tile(t, TOKEN_TILE)
    hg, sg, q, ckv, kpe = _inproj(x.reshape(t, D_MODEL), lw, tabs, tm)
    lt = _tile(l, 512)
    a, s_new = _hgrn(hg.reshape(b, l, 4 * HG_WIDTH), lw["lb"], lw["hg_norm"], hg_state, lt, _tile(lt, 128))
    tq = _tile(l, ATTN_TILE)
    if past is None:
        lk_true, q_off = l, 0
        ckv_all, kpe_all = ckv, kpe
        lk = l
    else:
        past_ckv, past_kpe = past
        q_off = past_ckv.shape[1]
        lk_true = q_off + l
        lk = -(-lk_true // ATTN_TILE) * ATTN_TILE
        padr = ((0, 0), (0, lk - lk_true), (0, 0))
        ckv_all = jnp.pad(jnp.concatenate([past_ckv, ckv.reshape(b, l, KV_LORA)], axis=1), padr).reshape(b * lk, KV_LORA)
        past_kpe = jnp.pad(past_kpe, ((0, 0), (0, 0), (0, LANES - QK_ROPE)))
        kpe_all = jnp.pad(jnp.concatenate([past_kpe, kpe.reshape(b, l, LANES)], axis=1), padr).reshape(b * lk, LANES)
    k_arr, v_arr = _kvbuild(ckv_all, kpe_all, lw, _tile(b * lk, TOKEN_TILE))
    bo = _attention(q.reshape(b, l, MLA_HEADS * HEAD_PAD), k_arr.reshape(b, lk, MLA_HEADS * HEAD_PAD),
                    v_arr.reshape(b, lk, MLA_WIDTH), lk_true, q_off, tq, _tile(lk, ATTN_TILE))
    return a.reshape(t, HG_WIDTH), bo.reshape(t, MLA_WIDTH), sg, ckv.reshape(b, l, KV_LORA), kpe[:, :QK_ROPE].reshape(b, l, QK_ROPE), s_new


def kernel(x_prompt, x_sample, cache_mla_ckv, cache_mla_kpe, state_hgrn, cache_mem_k, cache_mem_v, mem_prompt, hg_lb_logits, g_mix, w_in, g_qa, w_uq, g_kva, w_ukv, hg_norm, w_pa, w_pb, w_out, g_mem, w_mk, w_mv, g_x, w_xq, w_xo, g_ffn, w_router, b_router, w_gu, b_gu, w_down, b_down, g_final):
    bp, lp, _ = x_prompt.shape
    bs, ls, _ = x_sample.shape
    depth = w_in.shape[0]
    past_len = cache_mla_ckv.shape[2]
    n_mem = mem_prompt.shape[1]
    tp, ts = bp * lp, bs * ls
    t_all = tp + ts
    tm_p, tm_s = _tile(tp, TOKEN_TILE), _tile(ts, TOKEN_TILE)
    assert tp % tm_s == 0
    lb_all = jnp.cumsum(jax.nn.softmax(hg_lb_logits.astype(F32), axis=0), axis=0)
    tabs_p = _rope_tables(jnp.arange(lp, dtype=I32), max(1, tm_p // lp))
    tabs_s = _rope_tables(past_len + jnp.arange(ls, dtype=I32), max(1, tm_s // ls))
    g_fin = g_final.reshape(1, D_MODEL).astype(F32)

    n_asg = t_all * TOP_K
    bm = MOE_ROWS
    n_blocks = (n_asg + N_EXPERTS * (bm - 1) + bm - 1) // bm
    n_rows = n_blocks * bm

    hp, hs = x_prompt.reshape(tp, D_MODEL), x_sample.reshape(ts, D_MODEL)
    outs = [[] for _ in range(8)]
    for l in range(depth):
        lw = _layer_weights(l, lb_all, g_mix, w_in, g_qa, w_uq, g_kva, w_ukv, hg_norm, w_pa, w_pb, w_out, g_mem, w_mk, w_mv,
                            g_x, w_xq, w_xo, g_ffn, w_router, b_router, w_gu, b_gu, w_down, b_down)
        mk, mv = _memkv(mem_prompt.reshape(bp * n_mem, D_MODEL), lw, _tile(bp * n_mem, TOKEN_TILE))
        mk, mv = mk.reshape(bp, n_mem, X_WIDTH), mv.reshape(bp, n_mem, X_WIDTH)
        zero_state = jnp.zeros((bp, HG_HEADS, HG_DK, HG_DV), F32)
        a_p, bo_p, sg_p, ckv_p, kpe_p, st_p = _group_front(hp.reshape(bp, lp, D_MODEL), lw, tabs_p, lp, zero_state, None)
        a_s, bo_s, sg_s, ckv_s, kpe_s, st_s = _group_front(hs.reshape(bs, ls, D_MODEL), lw, tabs_s, ls, state_hgrn[l],
                                                           (cache_mla_ckv[l], cache_mla_kpe[l]))
        h2_p, u3, lg = _merge(hp, a_p, bo_p, sg_p, mk, mv, lw, tm_p, lp, t_all)
        h2_s, u3, lg = _merge(hs, a_s, bo_s, sg_s, cache_mem_k[l].reshape(bs, n_mem, X_WIDTH),
                              cache_mem_v[l].reshape(bs, n_mem, X_WIDTH), lw, tm_s, ls, t_all, shared=(u3, lg), tok_off=tp)
        gate, ir, cum, cnt = _route(lg, _tile(t_all, TOKEN_TILE))
        counts = cnt[0, :N_EXPERTS].astype(I32)
        padded = (counts + bm - 1) // bm * bm
        pend = jnp.cumsum(padded)
        block_start = jnp.arange(n_blocks, dtype=I32) * bm
        block_expert = jnp.minimum(jnp.sum((pend[None, :] <= block_start[:, None]).astype(I32), axis=1), N_EXPERTS - 1)
        block_j0 = block_start - (pend - padded)[block_expert]
        cum_be = jnp.take(cum[:, :N_EXPERTS].astype(I32), block_expert, axis=1)
        tile_lo = jnp.maximum(jnp.sum((cum_be <= block_j0[None, :]).astype(I32), axis=0) - 1, 0)
        tile_hi = jnp.sum((cum_be < (block_j0 + bm)[None, :]).astype(I32), axis=0)
        rows = _rowlist(ir, block_expert, block_j0, tile_lo, tile_hi, bm)
        row_t, row_k, pad_row = rows[:, 0, :].reshape(-1), rows[:, 1, :].reshape(-1), rows[:, 2, :].reshape(-1) == 0
        row_src = jnp.where(pad_row, 0, row_t) * ROW_TILE
        r = jnp.arange(n_rows, dtype=I32)
        spare = n_asg + (r // bm % MOE_BUFFERS) * bm + r % bm
        row_dst = jnp.where(pad_row, spare, row_k * t_all + row_t) * ROW_TILE
        y4 = _moe(u3, block_expert, row_src, row_dst, lw, bm, (n_asg + MOE_BUFFERS * bm) * ROW_TILE)
        last = l == depth - 1
        gf = g_fin if last else jnp.ones_like(g_fin)
        yp = _final(h2_p, gate, y4, gf, tm_p, 0, t_all)
        ys = _final(h2_s, gate, y4, gf, tm_s, tp, t_all)
        assert last, "multi-layer stacking needs the un-normalised residual stream"
        hp, hs = yp, ys
        for lst, v in zip(outs, (ckv_p, kpe_p, st_p, mk.reshape(bp, n_mem, X_HEADS, X_HEAD_DIM),
                                 mv.reshape(bp, n_mem, X_HEADS, X_HEAD_DIM), ckv_s, kpe_s, st_s)):
            lst.append(v)
    stk = [jnp.stack(o) for o in outs]
    return (hp.reshape(bp, lp, D_MODEL), hs.reshape(bs, ls, D_MODEL), stk[0], stk[1], stk[2], stk[3], stk[4], stk[5], stk[6], stk[7])
```

```python
import functools
import math

import jax
import jax.numpy as jnp
from jax import lax
from jax.experimental import pallas as pl
from jax.experimental.pallas import tpu as pltpu

F32 = jnp.float32
BF16 = jnp.bfloat16
I32 = jnp.int32

D_MODEL = 1024
CHUNK = 64
EPS = 1e-6
HG_HEADS = 4
HG_DK = 128
HG_DV = 128
HG_WIDTH = HG_HEADS * HG_DV
HG_SUB = 8
MLA_HEADS = 8
Q_LORA = 384
KV_LORA = 256
QK_NOPE = 64
QK_ROPE = 32
V_HEAD = 64
MLA_WIDTH = MLA_HEADS * V_HEAD
HEAD_PAD = 128
ROPE_THETA = 10000.0
X_HEADS = 4
X_HEAD_DIM = 128
X_WIDTH = X_HEADS * X_HEAD_DIM
N_EXPERTS = 32
TOP_K = 4
D_FF = D_MODEL
SWIGLU_LIMIT = 7.0
SWIGLU_ALPHA = 1.702
LANES = 128
ROW_TILE = D_MODEL // LANES
NEG = -1e30

VMEM_LIMIT = 56 * 1024 * 1024
TOKEN_TILE = 512
KV_TILE = 2048
ATTN_TILE = 256
ATTN_Q_TILE = 512
MOE_ROWS = 256
ROWLIST_ROWS = 256
MOE_BUFFERS = 3
MOE_AHEAD = MOE_BUFFERS - 1


def _dot(a, b):
    return jnp.dot(a, b, preferred_element_type=F32)


def _dot_nt(a, b):
    return lax.dot_general(a, b, (((1,), (1,)), ((), ())), preferred_element_type=F32)


def _dot_tn(a, b):
    return lax.dot_general(a, b, (((0,), (0,)), ((), ())), preferred_element_type=F32)


def _rms(x, g):
    return x * lax.rsqrt(jnp.mean(x * x, axis=-1, keepdims=True) + EPS) * g


def _sigmoid(x):
    return 1.0 / (1.0 + jnp.exp(-x))


def _const_spec(shape):
    zeros = (0,) * len(shape)
    return pl.BlockSpec(shape, lambda *_: zeros, pipeline_mode=pl.Buffered(1))


def _params(n_axes):
    return pltpu.CompilerParams(dimension_semantics=("arbitrary",) * n_axes, vmem_limit_bytes=VMEM_LIMIT)


def _inproj_kernel(x_ref, gmix_ref, whg_ref, wg_ref, wqa_ref, wkv_ref, wkr_ref, gqa_ref, wq_ref, wqr_ref, gkva_ref,
                   cosq_ref, sinq_ref, cosk_ref, sink_ref,
                   hg_ref, sg_ref, q_ref, ckv_ref, kpe_ref):
    u = _rms(x_ref[...], gmix_ref[...]).astype(BF16)
    hg_ref[...] = _dot(u, whg_ref[...])
    sg_ref[...] = _sigmoid(_dot(u, wg_ref[...])).astype(BF16)
    qn = _rms(_dot(u, wqa_ref[...]), gqa_ref[...]).astype(BF16)
    cosq = jnp.concatenate([cosq_ref[...]] * MLA_HEADS, axis=-1)
    sinq = jnp.concatenate([sinq_ref[...]] * MLA_HEADS, axis=-1)
    scale = (QK_NOPE + QK_ROPE) ** -0.5
    q = (_dot(qn, wq_ref[...]) * cosq + _dot(qn, wqr_ref[...]) * sinq) * scale
    q_ref[...] = q.astype(BF16)
    ckv_ref[...] = _rms(_dot(u, wkv_ref[...]), gkva_ref[...])
    pk = _dot(u, wkr_ref[...])
    kpe_ref[...] = pk[:, :LANES] * cosk_ref[...] + pk[:, LANES:] * sink_ref[...]


def _inproj(x, lw, tabs, tm):
    t = x.shape[0]
    nt = tabs["cosq"].shape[0] // tm
    tok = lambda i: (i, 0)
    tab = lambda i: (i % nt, 0)
    consts = [lw["g_mix"], lw["w_hg"], lw["w_g"], lw["w_qa"], lw["w_kv"], lw["w_kr"], lw["g_qa"], lw["w_q"], lw["w_qr"],
              lw["g_kva"]]
    in_specs = ([pl.BlockSpec((tm, D_MODEL), tok)] + [_const_spec(c.shape) for c in consts]
                + [pl.BlockSpec((tm, LANES), tab)] * 4)
    out_shape = (jax.ShapeDtypeStruct((t, 4 * HG_WIDTH), F32), jax.ShapeDtypeStruct((t, 2 * D_MODEL), BF16),
                 jax.ShapeDtypeStruct((t, MLA_HEADS * HEAD_PAD), BF16), jax.ShapeDtypeStruct((t, KV_LORA), F32),
                 jax.ShapeDtypeStruct((t, LANES), F32))
    out_specs = tuple(pl.BlockSpec((tm, s.shape[1]), tok) for s in out_shape)
    return pl.pallas_call(
        _inproj_kernel, grid=(t // tm,), in_specs=in_specs, out_specs=out_specs, out_shape=out_shape,
        compiler_params=_params(1), name="inproj",
    )(x, *consts, tabs["cosq"], tabs["sinq"], tabs["cosk"], tabs["sink"])


def _kvbuild_kernel(ckv_ref, kpe_ref, wkn_ref, wv_ref, e_ref, k_ref, v_ref):
    c = ckv_ref[...].astype(BF16)
    k_ref[...] = (_dot(c, wkn_ref[...]) + _dot(kpe_ref[...].astype(BF16), e_ref[...])).astype(BF16)
    v_ref[...] = _dot(c, wv_ref[...]).astype(BF16)


def _kvbuild(ckv, kpe, lw, tm):
    n = ckv.shape[0]
    tok = lambda i: (i, 0)
    consts = [lw["w_kn"], lw["w_v"], lw["e_kpe"]]
    out_shape = (jax.ShapeDtypeStruct((n, MLA_HEADS * HEAD_PAD), BF16), jax.ShapeDtypeStruct((n, MLA_WIDTH), BF16))
    return pl.pallas_call(
        _kvbuild_kernel, grid=(n // tm,),
        in_specs=[pl.BlockSpec((tm, KV_LORA), tok), pl.BlockSpec((tm, LANES), tok)] + [_const_spec(c.shape) for c in consts],
        out_specs=tuple(pl.BlockSpec((tm, s.shape[1]), tok) for s in out_shape), out_shape=out_shape,
        compiler_params=_params(1), name="kvbuild",
    )(ckv, kpe, *consts)


def _attn_tile(q_ref, k_ref, v_ref, o_ref, *, q0, tq, tk, lk_true):
    lim_first = min((q0 // CHUNK + 1) * CHUNK, lk_true)
    lim_last = min(((q0 + tq - 1) // CHUNK + 1) * CHUNK, lk_true)
    n_keys = -(-lim_last // tk) * tk
    n_open = lim_first // tk * tk
    outs = []
    for h in range(2):
        hs = slice(h * HEAD_PAD, (h + 1) * HEAD_PAD)
        s = _dot_nt(q_ref[:, hs], k_ref[0:n_keys, hs])
        if n_open < n_keys:
            edge = s[:, n_open:]
            kpos = n_open + lax.broadcasted_iota(I32, edge.shape, 1)
            qpos = q0 + lax.broadcasted_iota(I32, edge.shape, 0)
            edge = jnp.where(kpos < jnp.minimum((qpos // CHUNK + 1) * CHUNK, lk_true), edge, NEG)
            s = edge if n_open == 0 else jnp.concatenate([s[:, :n_open], edge], axis=-1)
        p = jnp.exp(s - jnp.max(s, axis=-1, keepdims=True))
        outs.append(_dot(p.astype(BF16), v_ref[0:n_keys, :]) / jnp.sum(p, axis=-1, keepdims=True))
    lane = lax.broadcasted_iota(I32, outs[0].shape, 1)
    o_ref[...] = jnp.where(lane < V_HEAD, outs[0], outs[1]).astype(BF16)


def _attn_kernel(q_ref, k_ref, v_ref, o_ref, *, n_q, tq, tk, lk_true, q_off):
    i = pl.program_id(2)
    for qi in range(n_q):
        @pl.when(i == qi)
        def _(qi=qi):
            _attn_tile(q_ref, k_ref, v_ref, o_ref, q0=q_off + qi * tq, tq=tq, tk=tk, lk_true=lk_true)


def _attention(q, k, v, lk_true, q_off, tq, tk):
    b, lq, _ = q.shape
    lk = k.shape[1]
    kern = functools.partial(_attn_kernel, n_q=lq // tq, tq=tq, tk=tk, lk_true=lk_true, q_off=q_off)
    return pl.pallas_call(
        kern, grid=(b, MLA_HEADS // 2, lq // tq),
        in_specs=[pl.BlockSpec((None, tq, 2 * HEAD_PAD), lambda bi, hp, i: (bi, i, hp)),
                  pl.BlockSpec((None, lk, 2 * HEAD_PAD), lambda bi, hp, i: (bi, 0, hp)),
                  pl.BlockSpec((None, lk, 2 * V_HEAD), lambda bi, hp, i: (bi, 0, hp))],
        out_specs=pl.BlockSpec((None, tq, 2 * V_HEAD), lambda bi, hp, i: (bi, i, hp)),
        out_shape=jax.ShapeDtypeStruct((b, lq, MLA_WIDTH), BF16),
        compiler_params=_params(3), name="attn",
    )(q, k, v)


def _hgrn_kernel(q_ref, f_ref, i_ref, g_ref, lb_ref, norm_ref, s0_ref, a_ref, sout_ref, st_sc, *, chunk, n_chunks):
    t = pl.program_id(1)

    @pl.when(t == 0)
    def _():
        for h in range(HG_HEADS):
            st_sc[h] = s0_ref[h].T

    row = lax.broadcasted_iota(I32, (chunk, chunk), 0)
    col = lax.broadcasted_iota(I32, (chunk, chunk), 1)
    tri = jnp.where(row >= col, 1.0, 0.0).astype(BF16)
    nsub = chunk // HG_SUB
    sub_row = lax.broadcasted_iota(I32, (nsub, HG_SUB, 1), 1)
    same_block = {}
    m = chunk // 2
    while m >= HG_SUB:
        same_block[2 * m] = jnp.where(row // (2 * m) == col // (2 * m), 1.0, 0.0)
        m //= 2

    def head_chunk(r0, h):
        hs = slice(h * HG_DK, (h + 1) * HG_DK)
        lb = lb_ref[:, hs]
        q = q_ref[pl.ds(r0, chunk), hs]
        v = i_ref[pl.ds(r0, chunk), hs]
        f = lb + (1.0 - lb) * _sigmoid(f_ref[pl.ds(r0, chunk), hs])
        lf = jnp.log(f)
        k = 1.0 - f
        hi = lf.astype(BF16)
        r1 = lf - hi.astype(F32)
        mid = r1.astype(BF16)
        lo = (r1 - mid.astype(F32)).astype(BF16)
        b = _dot(tri, hi) + _dot(tri, mid) + _dot(tri, lo)
        st = st_sc[h]
        o = _dot_nt((q * jnp.exp(b)).astype(BF16), st.astype(BF16))
        a_cross = jnp.zeros((chunk, chunk), F32)
        m = chunk // 2
        while m >= HG_SUB:
            blk = 2 * m
            shp = (chunk // blk, blk, HG_DK)
            b3 = b.reshape(shp)
            ref = b3[:, m - 1:m, :]
            first = lax.broadcasted_iota(I32, (chunk // blk, blk, 1), 1) < m
            decay = jnp.exp(-jnp.abs(b3 - ref))
            ql = jnp.where(first, 0.0, q.reshape(shp) * decay)
            kl = jnp.where(first, k.reshape(shp) * decay, 0.0)
            a_l = _dot_nt(ql.reshape(chunk, HG_DK).astype(BF16), kl.reshape(chunk, HG_DK).astype(BF16))
            a_cross = a_cross + a_l * same_block[blk]
            m //= 2
        o = o + _dot(a_cross.astype(BF16), v.astype(BF16))
        shp = (nsub, HG_SUB, HG_DK)
        q3, k3, b3, v3 = q.reshape(shp), k.reshape(shp), b.reshape(shp), v.reshape(shp)
        od = jnp.zeros(shp, F32)
        for s in range(HG_SUB):
            w = q3 * jnp.exp(b3 - b3[:, s:s + 1, :]) * k3[:, s:s + 1, :]
            a_col = jnp.where(sub_row >= s, jnp.sum(w, axis=-1, keepdims=True), 0.0)
            od = od + a_col * v3[:, s:s + 1, :]
        o = o + od.reshape(chunk, HG_DV)
        b_last = b[chunk - 1:chunk, :]
        kd = k * jnp.exp(b_last - b)
        st_sc[h] = st * jnp.exp(b_last) + _dot_tn(v.astype(BF16), kd.astype(BF16))
        g = g_ref[pl.ds(r0, chunk), hs]
        a_ref[pl.ds(r0, chunk), hs] = (_rms(o, norm_ref[:, hs]) * (g * _sigmoid(g))).astype(BF16)

    def body(c, carry):
        r0 = pl.multiple_of(c * chunk, chunk)
        for h in range(HG_HEADS):
            head_chunk(r0, h)
        return carry

    lax.fori_loop(0, n_chunks, body, 0)

    @pl.when(t == pl.num_programs(1) - 1)
    def _():
        for h in range(HG_HEADS):
            sout_ref[h] = st_sc[h].T


def _hgrn(hg, lb, norm, s0, lt, chunk):
    b, l, _ = hg.shape
    kern = functools.partial(_hgrn_kernel, chunk=chunk, n_chunks=lt // chunk)
    seg = lambda j: pl.BlockSpec((None, lt, HG_WIDTH), lambda bi, t: (bi, t, j))
    head_vec = _const_spec((1, HG_WIDTH))
    state = pl.BlockSpec((None, HG_HEADS, HG_DK, HG_DV), lambda bi, t: (bi, 0, 0, 0))
    return pl.pallas_call(
        kern, grid=(b, l // lt),
        in_specs=[seg(0), seg(1), seg(2), seg(3), head_vec, head_vec, state],
        out_specs=(pl.BlockSpec((None, lt, HG_WIDTH), lambda bi, t: (bi, t, 0)), state),
        out_shape=(jax.ShapeDtypeStruct((b, l, HG_WIDTH), BF16), jax.ShapeDtypeStruct(s0.shape, F32)),
        scratch_shapes=[pltpu.VMEM((HG_HEADS, HG_DV, HG_DK), F32)],
        compiler_params=_params(2), name="hgrn",
    )(hg, hg, hg, hg, lb, norm, s0)


def _memkv_kernel(m_ref, g_ref, wk_ref, wv_ref, k_ref, v_ref):
    m = _rms(m_ref[...], g_ref[...]).astype(BF16)
    k_ref[...] = _dot(m, wk_ref[...])
    v_ref[...] = _dot(m, wv_ref[...])


def _memkv(mem, lw, tm):
    n = mem.shape[0]
    tok = lambda i: (i, 0)
    consts = [lw["g_mem"], lw["w_mk"], lw["w_mv"]]
    out_shape = (jax.ShapeDtypeStruct((n, X_WIDTH), F32),) * 2
    return pl.pallas_call(
        _memkv_kernel, grid=(n // tm,),
        in_specs=[pl.BlockSpec((tm, D_MODEL), tok)] + [_const_spec(c.shape) for c in consts],
        out_specs=(pl.BlockSpec((tm, X_WIDTH), tok),) * 2, out_shape=out_shape,
        compiler_params=_params(1), name="memkv",
    )(mem, *consts)


def _merge_kernel(x_ref, a_ref, bo_ref, sg_ref, mk_ref, mv_ref, wpa_ref, wpb_ref, wout_ref, gx_ref, wxq_ref, wxo_ref,
                  gffn_ref, wr_ref, br_ref, *rest, n_seg, seg, n_tiles):
    h_ref, u_ref, lg_ref = rest[-3:]

    @pl.when(pl.program_id(0) >= n_tiles)
    def _():
        u_ref[...] = jnp.zeros(u_ref.shape, F32)
        lg_ref[...] = jnp.zeros(lg_ref.shape, F32)

    @pl.when(pl.program_id(0) < n_tiles)
    def _():
        _merge_tile(x_ref, a_ref, bo_ref, sg_ref, mk_ref, mv_ref, wpa_ref, wpb_ref, wout_ref, gx_ref, wxq_ref, wxo_ref,
                    gffn_ref, wr_ref, br_ref, h_ref, u_ref, lg_ref, n_seg=n_seg, seg=seg)


def _merge_tile(x_ref, a_ref, bo_ref, sg_ref, mk_ref, mv_ref, wpa_ref, wpb_ref, wout_ref, gx_ref, wxq_ref, wxo_ref,
                gffn_ref, wr_ref, br_ref, h_ref, u_ref, lg_ref, *, n_seg, seg):
    sg = sg_ref[...]
    mix = (sg[:, :D_MODEL].astype(F32) * _dot(a_ref[...], wpa_ref[...])
           + sg[:, D_MODEL:].astype(F32) * _dot(bo_ref[...], wpb_ref[...]))
    h1 = x_ref[...] + _dot(mix.astype(BF16), wout_ref[...])
    u2 = _rms(h1, gx_ref[...]).astype(BF16)
    qx = (_dot(u2, wxq_ref[...]) * X_HEAD_DIM ** -0.5).astype(BF16)
    segs = []
    for j in range(n_seg):
        kj = mk_ref[j].astype(BF16)
        vj = mv_ref[j].astype(BF16)
        heads = []
        for h in range(X_HEADS):
            hs = slice(h * X_HEAD_DIM, (h + 1) * X_HEAD_DIM)
            s = _dot_nt(qx[j * seg:(j + 1) * seg, hs], kj[:, hs])
            p = jnp.exp(s - jnp.max(s, axis=-1, keepdims=True))
            heads.append(_dot(p.astype(BF16), vj[:, hs]) / jnp.sum(p, axis=-1, keepdims=True))
        segs.append(jnp.concatenate(heads, axis=-1))
    ox = segs[0] if n_seg == 1 else jnp.concatenate(segs, axis=0)
    h2 = h1 + _dot(ox.astype(BF16), wxo_ref[...])
    h_ref[...] = h2
    u3 = _rms(h2, gffn_ref[...])
    _to_row_tiles(u_ref, u3)
    lg_ref[...] = _dot(u3.astype(BF16), wr_ref[...]) + br_ref[...]


def _merge(x, a, bo, sg, mk, mv, lw, tm, seq_len, t_all, shared=None, tok_off=0):
    t = x.shape[0]
    n_mem = mk.shape[1]
    n_seg = max(1, tm // seq_len)
    seg = tm // n_seg
    tiles_per_seq = max(1, seq_len // tm)
    n_tiles = t // tm
    assert t_all % tm == 0 and tok_off % tm == 0
    n_steps = n_tiles if shared is not None else t_all // tm
    assert shared is not None or tok_off == 0
    tok = lambda i: (jnp.minimum(i, n_tiles - 1), 0)
    out_tok = lambda i: (tok_off // tm + i, 0)
    mem = pl.BlockSpec((n_seg, n_mem, X_WIDTH), lambda i: (jnp.minimum(i, n_tiles - 1) // tiles_per_seq, 0, 0))
    consts = [lw["w_pa"], lw["w_pb"], lw["w_out"], lw["g_x"], lw["w_xq"], lw["w_xo"], lw["g_ffn"], lw["w_router"],
              lw["b_router"]]
    out_shape = (jax.ShapeDtypeStruct((t, D_MODEL), F32), jax.ShapeDtypeStruct((t_all * ROW_TILE, LANES), F32),
                 jax.ShapeDtypeStruct((t_all, LANES), F32))
    out_specs = (pl.BlockSpec((tm, D_MODEL), tok), pl.BlockSpec((tm * ROW_TILE, LANES), out_tok),
                 pl.BlockSpec((tm, LANES), out_tok))
    in_specs = ([pl.BlockSpec((tm, D_MODEL), tok), pl.BlockSpec((tm, HG_WIDTH), tok), pl.BlockSpec((tm, MLA_WIDTH), tok),
                 pl.BlockSpec((tm, 2 * D_MODEL), tok), mem, mem] + [_const_spec(c.shape) for c in consts])
    args = [x, a, bo, sg, mk, mv, *consts]
    aliases = {}
    if shared is not None:
        aliases = {len(args): 1, len(args) + 1: 2}
        in_specs += [pl.BlockSpec(memory_space=pl.ANY)] * 2
        args += list(shared)
    kern = functools.partial(_merge_kernel, n_seg=n_seg, seg=seg, n_tiles=n_tiles)
    return pl.pallas_call(
        kern, grid=(n_steps,), in_specs=in_specs, out_specs=out_specs, out_shape=out_shape, input_output_aliases=aliases,
        compiler_params=_params(1), name="merge",
    )(*args)


def _route_kernel(lg_ref, gate_ref, ir_ref, cum_ref, cnt_ref, carry_sc, *, tr):
    i = pl.program_id(0)

    @pl.when(i == 0)
    def _():
        carry_sc[...] = jnp.zeros(carry_sc.shape, F32)

    lane_i = lax.broadcasted_iota(I32, (tr, LANES), 1)
    lane = lane_i.astype(F32)
    logit = jnp.where(lane < N_EXPERTS, lg_ref[...], -jnp.inf)
    vals, idxs, hots = [], [], []
    for _ in range(TOP_K):
        top = jnp.max(logit, axis=-1, keepdims=True)
        first = jnp.min(jnp.where(logit == top, lane, float(LANES)), axis=-1, keepdims=True)
        hot = lane == first
        vals.append(top)
        idxs.append(first)
        hots.append(hot)
        logit = jnp.where(hot, -jnp.inf, logit)
    exps = [jnp.exp(v - vals[0]) for v in vals]
    denom = exps[0] + exps[1] + exps[2] + exps[3]
    hot_all = sum(jnp.where(h, 1.0, 0.0) for h in hots)
    row = lax.broadcasted_iota(I32, (tr, tr), 0)
    col = lax.broadcasted_iota(I32, (tr, tr), 1)
    before = jnp.where(row > col, 1.0, 0.0).astype(BF16)
    carry = carry_sc[...]
    prefix = _dot(before, hot_all.astype(BF16)) + carry
    lane4 = lax.broadcasted_iota(I32, (tr, TOP_K), 1)
    gate = jnp.zeros((tr, TOP_K), F32)
    cols = jnp.zeros((tr, LANES), F32)
    for k in range(TOP_K):
        rk = jnp.sum(jnp.where(hots[k], prefix, 0.0), axis=-1, keepdims=True)
        gate = jnp.where(lane4 == k, exps[k] / denom, gate)
        cols = jnp.where(lane_i == k, idxs[k], jnp.where(lane_i == TOP_K + k, rk, cols))
    gate_ref[...] = gate
    ir_ref[i] = cols.T[0:2 * TOP_K, :]
    cum_ref[pl.ds(i, 1), :] = carry
    carry = carry + jnp.sum(hot_all, axis=0, keepdims=True)
    carry_sc[...] = carry
    cnt_ref[...] = carry


def _route(logits, tr):
    t = logits.shape[0]
    n_tiles = t // tr
    tok = lambda i: (i, 0)
    out_shape = (jax.ShapeDtypeStruct((t, TOP_K), F32), jax.ShapeDtypeStruct((n_tiles, 2 * TOP_K, tr), F32),
                 jax.ShapeDtypeStruct((n_tiles, LANES), F32), jax.ShapeDtypeStruct((1, LANES), F32))
    return pl.pallas_call(
        functools.partial(_route_kernel, tr=tr), grid=(n_tiles,),
        in_specs=[pl.BlockSpec((tr, LANES), tok)],
        out_specs=(pl.BlockSpec((tr, TOP_K), tok), pl.BlockSpec((n_tiles, 2 * TOP_K, tr), lambda i: (0, 0, 0)),
                   pl.BlockSpec((n_tiles, LANES), lambda i: (0, 0)), pl.BlockSpec((1, LANES), lambda i: (0, 0))),
        out_shape=out_shape, scratch_shapes=[pltpu.VMEM((1, LANES), F32)],
        compiler_params=_params(1), name="route",
    )(logits)


def _rowlist_kernel(be_ref, j0_ref, tlo_ref, thi_ref, ir_ref, out_ref, *, tr, bm, group):
    g = pl.program_id(0)
    n_fold = tr // LANES
    tl = lax.broadcasted_iota(I32, (1, tr), 1)
    lo_code = (tl % 256 + 1).astype(F32).astype(BF16)
    hi_code = (tl // 256 * TOP_K + 1).astype(F32)
    jrow = lax.broadcasted_iota(I32, (bm, tr), 0).astype(F32).astype(BF16)
    kcol = lax.broadcasted_iota(I32, (TOP_K, tr), 0).astype(F32)
    lane = lax.broadcasted_iota(I32, (bm, LANES), 1)
    zero = jnp.zeros((), BF16)

    def fold(x):
        parts = [x[:, c * LANES:(c + 1) * LANES] for c in range(n_fold)]
        while len(parts) > 1:
            parts = [jnp.maximum(parts[c], parts[c + 1]) for c in range(0, len(parts), 2)]
        return parts[0]

    for q in range(group):
        b = g * group + q
        e = be_ref[b].astype(F32)
        j0 = j0_ref[b].astype(F32)

        def tile(t, planes, e=e, j0=j0):
            p_lo, p_hk, p_tile = planes
            ir = ir_ref[t]
            mine = ir[0:TOP_K, :] == e
            rel = jnp.sum(jnp.where(mine, ir[TOP_K:2 * TOP_K, :] - j0, 0.0), axis=0, keepdims=True)
            hit = jnp.sum(jnp.where(mine, 1.0, 0.0), axis=0, keepdims=True) > 0.0
            slot = jnp.sum(jnp.where(mine, kcol, 0.0), axis=0, keepdims=True)
            rel = jnp.where(hit, jnp.clip(rel, -1.0, float(bm)), -1.0).astype(BF16)
            match = jnp.broadcast_to(rel, (bm, tr)) == jrow
            f_lo = fold(jnp.where(match, jnp.broadcast_to(lo_code, (bm, tr)), zero))
            f_hk = fold(jnp.where(match, jnp.broadcast_to((hi_code + slot).astype(BF16), (bm, tr)), zero))
            tile_no = jnp.broadcast_to(jnp.full((1, LANES), t + 1, I32).astype(F32).astype(BF16), (bm, LANES))
            return jnp.maximum(p_lo, f_lo), jnp.maximum(p_hk, f_hk), jnp.where(f_lo > zero, tile_no, p_tile)

        empty = jnp.zeros((bm, LANES), BF16)
        p_lo, p_hk, p_tile = lax.fori_loop(tlo_ref[b], thi_ref[b], tile, (empty, empty, empty))
        lo = jnp.max(p_lo.astype(F32), axis=-1, keepdims=True)
        hk = jnp.max(p_hk.astype(F32), axis=-1, keepdims=True) - 1.0
        tile_no = jnp.max(p_tile.astype(F32), axis=-1, keepdims=True) - 1.0
        found = lo > 0.0
        hi = jnp.floor(hk * (1.0 / TOP_K))
        tok = jnp.where(found, tile_no * float(tr) + hi * 256.0 + lo - 1.0, 0.0)
        k = jnp.where(found, hk - hi * TOP_K, 0.0)
        cols = jnp.where(lane == 0, tok, jnp.where(lane == 1, k, jnp.where(lane == 2, jnp.where(found, 1.0, 0.0), 0.0)))
        out_ref[q] = cols.T[0:ROW_TILE, :].astype(I32)


def _rowlist(ir, block_expert, block_j0, tile_lo, tile_hi, bm):
    n_tiles, _, tr = ir.shape
    n_blocks = block_expert.shape[0]
    assert n_tiles < 256 and bm <= 256 and tr <= 8192
    group = 8
    while n_blocks % group:
        group //= 2
    grid_spec = pltpu.PrefetchScalarGridSpec(
        num_scalar_prefetch=4, grid=(n_blocks // group,),
        in_specs=[pl.BlockSpec((n_tiles, 2 * TOP_K, tr), lambda g, *_: (0, 0, 0))],
        out_specs=pl.BlockSpec((group, ROW_TILE, bm), lambda g, *_: (g, 0, 0)))
    return pl.pallas_call(
        functools.partial(_rowlist_kernel, tr=tr, bm=bm, group=group), grid_spec=grid_spec,
        out_shape=jax.ShapeDtypeStruct((n_blocks, ROW_TILE, bm), I32),
        compiler_params=_params(1), name="rowlist",
    )(block_expert, block_j0, tile_lo, tile_hi, ir)


def _to_row_tiles(ref, x):
    n = x.shape[0]
    for c in range(ROW_TILE):
        ref[pl.ds(c, n, stride=ROW_TILE), :] = x[:, c * LANES:(c + 1) * LANES]


def _row_tile_chunk(ref, n, c):
    return ref[pl.ds(c, n, stride=ROW_TILE), :]


def _moe_kernel(be_ref, *refs, bm, n_blocks):
    rt_first = refs[:MOE_AHEAD]
    rtn_ref, rdp_ref, x_hbm, wgu_ref, bgu_ref, wd_ref, bd_ref, y_hbm = refs[MOE_AHEAD:MOE_AHEAD + 8]
    scratch = refs[MOE_AHEAD + 8:]
    xs, ys = scratch[:MOE_BUFFERS], scratch[MOE_BUFFERS:2 * MOE_BUFFERS]
    wgu_bf, wd_bf, gsem, ssem = scratch[2 * MOE_BUFFERS:]
    b = pl.program_id(0)
    rows = bm * ROW_TILE

    def gather(idx_ref, r, slot):
        src = pl.multiple_of(idx_ref[0, 0, r], ROW_TILE)
        return pltpu.make_async_copy(x_hbm.at[pl.ds(src, ROW_TILE), :], xs[slot].at[pl.ds(r * ROW_TILE, ROW_TILE), :],
                                     gsem.at[slot])

    def scatter(r, slot):
        dst = pl.multiple_of(rdp_ref[0, 0, r], ROW_TILE)
        return pltpu.make_async_copy(ys[slot].at[pl.ds(r * ROW_TILE, ROW_TILE), :], y_hbm.at[pl.ds(dst, ROW_TILE), :],
                                     ssem.at[slot])

    def wait_gathers(slot):
        pltpu.make_async_copy(x_hbm.at[pl.ds(0, rows), :], xs[slot], gsem.at[slot]).wait()

    def wait_scatters(slot):
        pltpu.make_async_copy(ys[slot], y_hbm.at[pl.ds(0, rows), :], ssem.at[slot]).wait()

    @pl.when(b == 0)
    def _():
        ys[MOE_BUFFERS - 1][...] = jnp.zeros(ys[MOE_BUFFERS - 1].shape, F32)
        for j in range(MOE_AHEAD):
            def start(r, c, j=j):
                gather(rt_first[j], r, j).start()
                return c

            lax.fori_loop(0, bm, start, 0)

    def step(slot):
        prev = (slot - 1) % MOE_BUFFERS
        wait_gathers(slot)

        @pl.when(b >= MOE_BUFFERS - 1)
        def _():
            wait_scatters(slot)

        for r in range(bm):
            gather(rtn_ref, r, prev).start()
        for r in range(bm):
            scatter(r, prev).start(priority=1)
        x = jnp.concatenate([_row_tile_chunk(xs[slot], bm, c) for c in range(ROW_TILE)], axis=-1).astype(BF16)
        gu = _dot(x, wgu_bf[...]) + bgu_ref[...]
        g = jnp.minimum(gu[:, :D_FF], SWIGLU_LIMIT)
        up = jnp.clip(gu[:, D_FF:], -SWIGLU_LIMIT, SWIGLU_LIMIT)
        act = (up + 1.0) * g * _sigmoid(SWIGLU_ALPHA * g)
        _to_row_tiles(ys[slot], _dot(act.astype(BF16), wd_bf[...]) + bd_ref[...])

    expert = be_ref[jnp.minimum(b, n_blocks - 1)]
    new_expert = jnp.logical_or(b == 0, expert != be_ref[jnp.maximum(b - 1, 0)])

    @pl.when(jnp.logical_and(b < n_blocks, new_expert))
    def _():
        wgu_bf[...] = wgu_ref[...].astype(BF16)
        wd_bf[...] = wd_ref[...].astype(BF16)

    for v in range(MOE_BUFFERS):
        @pl.when(jnp.logical_and(b < n_blocks, b % MOE_BUFFERS == v))
        def _(v=v):
            step(v)

    @pl.when(b == n_blocks)
    def _():
        for j in range(MOE_AHEAD):
            wait_gathers((n_blocks + j) % MOE_BUFFERS)
        for j in range(2, MOE_BUFFERS + 1):
            wait_scatters((n_blocks - j) % MOE_BUFFERS)
        last = (n_blocks - 1) % MOE_BUFFERS

        def start(r, c):
            scatter(r, last).start()
            return c

        lax.fori_loop(0, bm, start, 0)
        wait_scatters(last)


def _moe(x, block_expert, row_src, row_dst, lw, bm, n_out_rows):
    n_blocks = block_expert.shape[0]
    assert n_blocks >= MOE_BUFFERS
    spare = (n_out_rows // ROW_TILE - bm + jnp.arange(bm, dtype=I32)) * ROW_TILE
    src = row_src.reshape(n_blocks, 1, bm)
    dst_prev = jnp.concatenate([spare.reshape(1, 1, bm), row_dst.reshape(n_blocks, 1, bm)], axis=0)
    smem = lambda f: pl.BlockSpec((1, 1, bm), f, memory_space=pltpu.SMEM)
    blk = lambda b: jnp.minimum(b, n_blocks - 1)
    first = [smem(lambda b, be, j=j: (j, 0, 0)) for j in range(MOE_AHEAD)]
    grid_spec = pltpu.PrefetchScalarGridSpec(
        num_scalar_prefetch=1, grid=(n_blocks + 1,),
        in_specs=first + [smem(lambda b, be: (blk(b + MOE_AHEAD), 0, 0)), smem(lambda b, be: (b, 0, 0)),
                  pl.BlockSpec(memory_space=pl.ANY),
                  pl.BlockSpec((None, D_MODEL, 2 * D_FF), lambda b, be: (be[blk(b)], 0, 0)),
                  pl.BlockSpec((None, 1, 2 * D_FF), lambda b, be: (be[blk(b)], 0, 0)),
                  pl.BlockSpec((None, D_FF, D_MODEL), lambda b, be: (be[blk(b)], 0, 0)),
                  pl.BlockSpec((None, 1, D_MODEL), lambda b, be: (be[blk(b)], 0, 0))],
        out_specs=pl.BlockSpec(memory_space=pl.ANY),
        scratch_shapes=[pltpu.VMEM((bm * ROW_TILE, LANES), F32)] * (2 * MOE_BUFFERS)
        + [pltpu.VMEM((D_MODEL, 2 * D_FF), BF16), pltpu.VMEM((D_FF, D_MODEL), BF16)]
        + [pltpu.SemaphoreType.DMA((MOE_BUFFERS,))] * 2)
    return pl.pallas_call(
        functools.partial(_moe_kernel, bm=bm, n_blocks=n_blocks), grid_spec=grid_spec,
        out_shape=jax.ShapeDtypeStruct((n_out_rows, LANES), F32),
        compiler_params=_params(1), name="moe",
    )(block_expert, *([src] * (MOE_AHEAD + 1)), dst_prev, x, lw["w_gu"], lw["b_gu"], lw["w_down"], lw["b_down"])


def _final_kernel(h_ref, gate_ref, y0_ref, y1_ref, y2_ref, y3_ref, g_ref, o_ref, *, tm):
    gate = gate_ref[...]
    gates = [jnp.broadcast_to(gate[:, k:k + 1], (tm, LANES)) for k in range(TOP_K)]
    ssq = jnp.zeros((tm, 1), F32)
    for c in range(ROW_TILE):
        cs = slice(c * LANES, (c + 1) * LANES)
        h = h_ref[:, cs]
        for k, y_ref in enumerate((y0_ref, y1_ref, y2_ref, y3_ref)):
            h = h + gates[k] * _row_tile_chunk(y_ref, tm, c)
        o_ref[:, cs] = h
        ssq = ssq + jnp.sum(h * h, axis=-1, keepdims=True)
    o_ref[...] = o_ref[...] * lax.rsqrt(ssq * (1.0 / D_MODEL) + EPS) * g_ref[...]


def _final(h, gate, y4, g_final, tm, tok_off, t_all):
    t = h.shape[0]
    tok = lambda i: (i, 0)
    slot = lambda k: pl.BlockSpec((tm * ROW_TILE, LANES), lambda i: ((k * t_all + tok_off) // tm + i, 0))
    return pl.pallas_call(
        functools.partial(_final_kernel, tm=tm), grid=(t // tm,),
        in_specs=[pl.BlockSpec((tm, D_MODEL), tok), pl.BlockSpec((tm, TOP_K), lambda i: (tok_off // tm + i, 0)),
                  slot(0), slot(1), slot(2), slot(3), _const_spec(g_final.shape)],
        out_specs=pl.BlockSpec((tm, D_MODEL), tok), out_shape=jax.ShapeDtypeStruct((t, D_MODEL), F32),
        compiler_params=_params(1), name="final",
    )(h, gate, y4, y4, y4, y4, g_final)


def _rope_tables(pos, reps):
    half = QK_ROPE // 2
    inv_freq = jnp.exp(-math.log(ROPE_THETA) * jnp.arange(half, dtype=F32) / half)
    ang = pos.astype(F32)[:, None] * inv_freq[None, :]
    cos = jnp.concatenate([jnp.cos(ang)] * 2, axis=-1)
    sin = jnp.concatenate([jnp.sin(ang)] * 2, axis=-1)
    n = pos.shape[0]
    ones = jnp.ones((n, QK_NOPE), F32)
    z = lambda w: jnp.zeros((n, w), F32)
    tabs = {"cosq": jnp.concatenate([ones, cos, z(HEAD_PAD - QK_NOPE - QK_ROPE)], axis=-1),
            "sinq": jnp.concatenate([z(QK_NOPE), sin, z(HEAD_PAD - QK_NOPE - QK_ROPE)], axis=-1),
            "cosk": jnp.concatenate([cos, z(LANES - QK_ROPE)], axis=-1),
            "sink": jnp.concatenate([sin, z(LANES - QK_ROPE)], axis=-1)}
    return {k: jnp.tile(v, (reps, 1)) for k, v in tabs.items()}


def _rot_cols(w):
    half = QK_ROPE // 2
    return jnp.concatenate([-w[..., half:], w[..., :half]], axis=-1)


def _layer_weights(l, lb, g_mix, w_in, g_qa, w_uq, g_kva, w_ukv, hg_norm, w_pa, w_pb, w_out, g_mem, w_mk, w_mv, g_x, w_xq,
                   w_xo, g_ffn, w_router, b_router, w_gu, b_gu, w_down, b_down):
    row = lambda v: v.reshape(1, -1).astype(F32)
    wi = w_in[l]
    o_qa = 4 * HG_WIDTH
    o_kv = o_qa + Q_LORA
    o_kr = o_kv + KV_LORA
    o_g = o_kr + QK_ROPE
    w_kr = wi[:, o_kr:o_g]
    zk = jnp.zeros((D_MODEL, LANES - QK_ROPE), F32)
    uq = w_uq[l].reshape(Q_LORA, MLA_HEADS, QK_NOPE + QK_ROPE)
    zq = lambda w: jnp.zeros((Q_LORA, MLA_HEADS, w), F32)
    pad = HEAD_PAD - QK_NOPE - QK_ROPE
    w_q = jnp.concatenate([uq, zq(pad)], axis=-1).reshape(Q_LORA, MLA_HEADS * HEAD_PAD)
    w_qr = jnp.concatenate([zq(QK_NOPE), _rot_cols(uq[..., QK_NOPE:]), zq(pad)], axis=-1).reshape(Q_LORA, MLA_HEADS * HEAD_PAD)
    ukv = w_ukv[l].reshape(KV_LORA, MLA_HEADS, QK_NOPE + V_HEAD)
    w_kn = jnp.concatenate([ukv[..., :QK_NOPE], jnp.zeros((KV_LORA, MLA_HEADS, HEAD_PAD - QK_NOPE), F32)], axis=-1)
    e_head = jnp.concatenate([jnp.zeros((QK_ROPE, QK_NOPE), F32), jnp.eye(QK_ROPE, dtype=F32),
                              jnp.zeros((QK_ROPE, pad), F32)], axis=-1)
    e_kpe = jnp.concatenate([jnp.tile(e_head, (1, MLA_HEADS)), jnp.zeros((LANES - QK_ROPE, MLA_HEADS * HEAD_PAD), F32)], axis=0)
    bf = lambda w: w.astype(BF16)
    return {
        "lb": row(lb[l]), "g_mix": row(g_mix[l]), "w_hg": bf(wi[:, :o_qa]), "w_g": bf(wi[:, o_g:]),
        "w_qa": bf(wi[:, o_qa:o_kv]), "w_kv": bf(wi[:, o_kv:o_kr]),
        "w_kr": bf(jnp.concatenate([w_kr, zk, _rot_cols(w_kr), zk], axis=-1)),
        "g_qa": row(g_qa[l]), "w_q": bf(w_q), "w_qr": bf(w_qr), "g_kva": row(g_kva[l]),
        "w_kn": bf(w_kn.reshape(KV_LORA, MLA_HEADS * HEAD_PAD)), "w_v": bf(ukv[..., QK_NOPE:].reshape(KV_LORA, MLA_WIDTH)),
        "e_kpe": bf(e_kpe), "hg_norm": row(hg_norm[l]), "w_pa": bf(w_pa[l]), "w_pb": bf(w_pb[l]), "w_out": bf(w_out[l]),
        "g_mem": row(g_mem[l]), "w_mk": bf(w_mk[l]), "w_mv": bf(w_mv[l]), "g_x": row(g_x[l]), "w_xq": bf(w_xq[l]),
        "w_xo": bf(w_xo[l]), "g_ffn": row(g_ffn[l]),
        "w_router": bf(jnp.pad(w_router[l], ((0, 0), (0, LANES - N_EXPERTS)))),
        "b_router": jnp.pad(row(b_router[l]), ((0, 0), (0, LANES - N_EXPERTS))),
        "w_gu": w_gu[l], "b_gu": b_gu[l].reshape(N_EXPERTS, 1, 2 * D_FF).astype(F32),
        "w_down": w_down[l], "b_down": b_down[l].reshape(N_EXPERTS, 1, D_MODEL).astype(F32),
    }


def _tile(n, want):
    t = min(n, want)
    while n % t:
        t -= 8
    assert t > 0 and n % t == 0, (n, want)
    return t


def _group_front(x, lw, tabs, seq_len, hg_state, past):
    b, l, _ = x.shape
    t = b * l
    tm = _tile(t, TOKEN_TILE)
    hg, sg, q, ckv, kpe = _inproj(x.reshape(t, D_MODEL), lw, tabs, tm)
    lt = _tile(l, 512)
    a, s_new = _hgrn(hg.reshape(b, l, 4 * HG_WIDTH), lw["lb"], lw["hg_norm"], hg_state, lt, _tile(lt, 128))
    tq = _tile(l, ATTN_Q_TILE)
    if past is None:
        lk_true, q_off = l, 0
        ckv_all, kpe_all = ckv, kpe
        lk = l
    else:
        past_ckv, past_kpe = past
        q_off = past_ckv.shape[1]
        lk_true = q_off + l
        lk = -(-lk_true // ATTN_TILE) * ATTN_TILE
        padr = ((0, 0), (0, lk - lk_true), (0, 0))
        ckv_all = jnp.pad(jnp.concatenate([past_ckv, ckv.reshape(b, l, KV_LORA)], axis=1), padr).reshape(b * lk, KV_LORA)
        past_kpe = jnp.pad(past_kpe, ((0, 0), (0, 0), (0, LANES - QK_ROPE)))
        kpe_all = jnp.pad(jnp.concatenate([past_kpe, kpe.reshape(b, l, LANES)], axis=1), padr).reshape(b * lk, LANES)
    k_arr, v_arr = _kvbuild(ckv_all, kpe_all, lw, _tile(b * lk, KV_TILE))
    bo = _attention(q.reshape(b, l, MLA_HEADS * HEAD_PAD), k_arr.reshape(b, lk, MLA_HEADS * HEAD_PAD),
                    v_arr.reshape(b, lk, MLA_WIDTH), lk_true, q_off, tq, _tile(lk, ATTN_TILE))
    return a.reshape(t, HG_WIDTH), bo.reshape(t, MLA_WIDTH), sg, ckv.reshape(b, l, KV_LORA), kpe[:, :QK_ROPE].reshape(b, l, QK_ROPE), s_new


def kernel(x_prompt, x_sample, cache_mla_ckv, cache_mla_kpe, state_hgrn, cache_mem_k, cache_mem_v, mem_prompt, hg_lb_logits, g_mix, w_in, g_qa, w_uq, g_kva, w_ukv, hg_norm, w_pa, w_pb, w_out, g_mem, w_mk, w_mv, g_x, w_xq, w_xo, g_ffn, w_router, b_router, w_gu, b_gu, w_down, b_down, g_final):
    bp, lp, _ = x_prompt.shape
    bs, ls, _ = x_sample.shape
    depth = w_in.shape[0]
    past_len = cache_mla_ckv.shape[2]
    n_mem = mem_prompt.shape[1]
    tp, ts = bp * lp, bs * ls
    t_all = tp + ts
    tm_p, tm_s = _tile(tp, TOKEN_TILE), _tile(ts, TOKEN_TILE)
    assert tp % tm_s == 0
    lb_all = jnp.cumsum(jax.nn.softmax(hg_lb_logits.astype(F32), axis=0), axis=0)
    tabs_p = _rope_tables(jnp.arange(lp, dtype=I32), max(1, tm_p // lp))
    tabs_s = _rope_tables(past_len + jnp.arange(ls, dtype=I32), max(1, tm_s // ls))
    g_fin = g_final.reshape(1, D_MODEL).astype(F32)

    n_asg = t_all * TOP_K
    bm = MOE_ROWS
    n_blocks = (n_asg + N_EXPERTS * (bm - 1) + bm - 1) // bm
    n_rows = n_blocks * bm

    hp, hs = x_prompt.reshape(tp, D_MODEL), x_sample.reshape(ts, D_MODEL)
    outs = [[] for _ in range(8)]
    for l in range(depth):
        lw = _layer_weights(l, lb_all, g_mix, w_in, g_qa, w_uq, g_kva, w_ukv, hg_norm, w_pa, w_pb, w_out, g_mem, w_mk, w_mv,
                            g_x, w_xq, w_xo, g_ffn, w_router, b_router, w_gu, b_gu, w_down, b_down)
        mk, mv = _memkv(mem_prompt.reshape(bp * n_mem, D_MODEL), lw, _tile(bp * n_mem, TOKEN_TILE))
        mk, mv = mk.reshape(bp, n_mem, X_WIDTH), mv.reshape(bp, n_mem, X_WIDTH)
        zero_state = jnp.zeros((bp, HG_HEADS, HG_DK, HG_DV), F32)
        a_p, bo_p, sg_p, ckv_p, kpe_p, st_p = _group_front(hp.reshape(bp, lp, D_MODEL), lw, tabs_p, lp, zero_state, None)
        a_s, bo_s, sg_s, ckv_s, kpe_s, st_s = _group_front(hs.reshape(bs, ls, D_MODEL), lw, tabs_s, ls, state_hgrn[l],
                                                           (cache_mla_ckv[l], cache_mla_kpe[l]))
        h2_p, u3, lg = _merge(hp, a_p, bo_p, sg_p, mk, mv, lw, tm_p, lp, t_all)
        h2_s, u3, lg = _merge(hs, a_s, bo_s, sg_s, cache_mem_k[l].reshape(bs, n_mem, X_WIDTH),
                              cache_mem_v[l].reshape(bs, n_mem, X_WIDTH), lw, tm_s, ls, t_all, shared=(u3, lg), tok_off=tp)
        gate, ir, cum, cnt = _route(lg, _tile(t_all, TOKEN_TILE))
        counts = cnt[0, :N_EXPERTS].astype(I32)
        padded = (counts + bm - 1) // bm * bm
        pend = jnp.cumsum(padded)

        def owner(start):
            return jnp.minimum(jnp.sum((pend[None, :] <= start[:, None]).astype(I32), axis=1), N_EXPERTS - 1)

        block_expert = owner(jnp.arange(n_blocks, dtype=I32) * bm)
        sub_start = jnp.arange(n_rows // ROWLIST_ROWS, dtype=I32) * ROWLIST_ROWS
        sub_expert = owner(sub_start)
        sub_j0 = sub_start - (pend - padded)[sub_expert]
        cum_be = jnp.take(cum[:, :N_EXPERTS].astype(I32), sub_expert, axis=1)
        tile_lo = jnp.maximum(jnp.sum((cum_be <= sub_j0[None, :]).astype(I32), axis=0) - 1, 0)
        tile_hi = jnp.sum((cum_be < (sub_j0 + ROWLIST_ROWS)[None, :]).astype(I32), axis=0)
        rows = _rowlist(ir, sub_expert, sub_j0, tile_lo, tile_hi, ROWLIST_ROWS)
        row_t, row_k, pad_row = rows[:, 0, :].reshape(-1), rows[:, 1, :].reshape(-1), rows[:, 2, :].reshape(-1) == 0
        row_src = jnp.where(pad_row, 0, row_t) * ROW_TILE
        r = jnp.arange(n_rows, dtype=I32)
        spare = n_asg + (r // bm % MOE_BUFFERS) * bm + r % bm
        row_dst = jnp.where(pad_row, spare, row_k * t_all + row_t) * ROW_TILE
        y4 = _moe(u3, block_expert, row_src, row_dst, lw, bm, (n_asg + MOE_BUFFERS * bm) * ROW_TILE)
        last = l == depth - 1
        gf = g_fin if last else jnp.ones_like(g_fin)
        yp = _final(h2_p, gate, y4, gf, tm_p, 0, t_all)
        ys = _final(h2_s, gate, y4, gf, tm_s, tp, t_all)
        assert last, "multi-layer stacking needs the un-normalised residual stream"
        hp, hs = yp, ys
        for lst, v in zip(outs, (ckv_p, kpe_p, st_p, mk.reshape(bp, n_mem, X_HEADS, X_HEAD_DIM),
                                 mv.reshape(bp, n_mem, X_HEADS, X_HEAD_DIM), ckv_s, kpe_s, st_s)):
            lst.append(v)
    stk = [jnp.stack(o) for o in outs]
    return (hp.reshape(bp, lp, D_MODEL), hs.reshape(bs, ls, D_MODEL), stk[0], stk[1], stk[2], stk[3], stk[4], stk[5], stk[6], stk[7])
```

```python
import functools
import math

import jax
import jax.numpy as jnp
from jax import lax
from jax.experimental import pallas as pl
from jax.experimental.pallas import tpu as pltpu

F32 = jnp.float32
BF16 = jnp.bfloat16
I32 = jnp.int32

D_MODEL = 1024
CHUNK = 64
EPS = 1e-6
HG_HEADS = 4
HG_DK = 128
HG_DV = 128
HG_WIDTH = HG_HEADS * HG_DV
HG_SUB = 8
HG_CHUNK = 128
MLA_HEADS = 8
Q_LORA = 384
KV_LORA = 256
QK_NOPE = 64
QK_ROPE = 32
V_HEAD = 64
MLA_WIDTH = MLA_HEADS * V_HEAD
HEAD_PAD = 128
ROPE_THETA = 10000.0
X_HEADS = 4
X_HEAD_DIM = 128
X_WIDTH = X_HEADS * X_HEAD_DIM
N_EXPERTS = 32
TOP_K = 4
D_FF = D_MODEL
SWIGLU_LIMIT = 7.0
SWIGLU_ALPHA = 1.702
LANES = 128
ROW_TILE = D_MODEL // LANES
NEG = -1e30

VMEM_LIMIT = 56 * 1024 * 1024
TOKEN_TILE = 512
KV_TILE = 2048
ATTN_TILE = 256
ATTN_Q_TILE = 512
MOE_ROWS = 256
ROWLIST_ROWS = 256
MOE_BUFFERS = 3
MOE_AHEAD = MOE_BUFFERS - 1


def _dot(a, b):
    return jnp.dot(a, b, preferred_element_type=F32)


def _dot_nt(a, b):
    return lax.dot_general(a, b, (((1,), (1,)), ((), ())), preferred_element_type=F32)


def _dot_tn(a, b):
    return lax.dot_general(a, b, (((0,), (0,)), ((), ())), preferred_element_type=F32)


def _rms(x, g):
    return x * lax.rsqrt(jnp.mean(x * x, axis=-1, keepdims=True) + EPS) * g


def _sigmoid(x):
    return 1.0 / (1.0 + jnp.exp(-x))


def _const_spec(shape):
    zeros = (0,) * len(shape)
    return pl.BlockSpec(shape, lambda *_: zeros, pipeline_mode=pl.Buffered(1))


def _params(n_axes):
    return pltpu.CompilerParams(dimension_semantics=("arbitrary",) * n_axes, vmem_limit_bytes=VMEM_LIMIT)


def _inproj_kernel(x_ref, gmix_ref, whg_ref, wg_ref, wqa_ref, wkv_ref, wkr_ref, gqa_ref, wq_ref, gkva_ref,
                   cosq_ref, sinq_ref, cosk_ref, sink_ref,
                   hg_ref, sg_ref, q_ref, ckv_ref, kpe_ref):
    u = _rms(x_ref[...], gmix_ref[...]).astype(BF16)
    hg_ref[...] = _dot(u, whg_ref[...])
    sg_ref[...] = _sigmoid(_dot(u, wg_ref[...])).astype(BF16)
    qn = _rms(_dot(u, wqa_ref[...]), gqa_ref[...]).astype(BF16)
    cosq = jnp.concatenate([cosq_ref[...]] * MLA_HEADS, axis=-1)
    sinq = jnp.concatenate([sinq_ref[...]] * MLA_HEADS, axis=-1)
    scale = (QK_NOPE + QK_ROPE) ** -0.5
    qr = _dot(qn, wq_ref[...])
    q = (qr * cosq + pltpu.roll(qr, MLA_HEADS * HEAD_PAD - QK_ROPE, axis=1) * sinq) * scale
    q_ref[...] = q.astype(BF16)
    ckv_ref[...] = _rms(_dot(u, wkv_ref[...]), gkva_ref[...])
    pk = _dot(u, wkr_ref[...])
    kpe_ref[...] = pk[:, :LANES] * cosk_ref[...] + pk[:, LANES:] * sink_ref[...]


def _inproj(x, lw, tabs, tm):
    t = x.shape[0]
    nt = tabs["cosq"].shape[0] // tm
    tok = lambda i: (i, 0)
    tab = lambda i: (i % nt, 0)
    consts = [lw["g_mix"], lw["w_hg"], lw["w_g"], lw["w_qa"], lw["w_kv"], lw["w_kr"], lw["g_qa"], lw["w_q"],
              lw["g_kva"]]
    in_specs = ([pl.BlockSpec((tm, D_MODEL), tok)] + [_const_spec(c.shape) for c in consts]
                + [pl.BlockSpec((tm, LANES), tab)] * 4)
    out_shape = (jax.ShapeDtypeStruct((t, 4 * HG_WIDTH), F32), jax.ShapeDtypeStruct((t, 2 * D_MODEL), BF16),
                 jax.ShapeDtypeStruct((t, MLA_HEADS * HEAD_PAD), BF16), jax.ShapeDtypeStruct((t, KV_LORA), F32),
                 jax.ShapeDtypeStruct((t, LANES), F32))
    out_specs = tuple(pl.BlockSpec((tm, s.shape[1]), tok) for s in out_shape)
    return pl.pallas_call(
        _inproj_kernel, grid=(t // tm,), in_specs=in_specs, out_specs=out_specs, out_shape=out_shape,
        compiler_params=_params(1), name="inproj",
    )(x, *consts, tabs["cosq"], tabs["sinq"], tabs["cosk"], tabs["sink"])


def _kvbuild_kernel(ckv_ref, kpe_ref, wkn_ref, wv_ref, e_ref, k_ref, v_ref):
    c = ckv_ref[...].astype(BF16)
    k_ref[...] = (_dot(c, wkn_ref[...]) + _dot(kpe_ref[...].astype(BF16), e_ref[...])).astype(BF16)
    v_ref[...] = _dot(c, wv_ref[...]).astype(BF16)


def _kvbuild(ckv, kpe, lw, tm):
    n = ckv.shape[0]
    tok = lambda i: (i, 0)
    consts = [lw["w_kn"], lw["w_v"], lw["e_kpe"]]
    out_shape = (jax.ShapeDtypeStruct((n, MLA_HEADS * HEAD_PAD), BF16), jax.ShapeDtypeStruct((n, MLA_WIDTH), BF16))
    return pl.pallas_call(
        _kvbuild_kernel, grid=(n // tm,),
        in_specs=[pl.BlockSpec((tm, KV_LORA), tok), pl.BlockSpec((tm, LANES), tok)] + [_const_spec(c.shape) for c in consts],
        out_specs=tuple(pl.BlockSpec((tm, s.shape[1]), tok) for s in out_shape), out_shape=out_shape,
        compiler_params=_params(1), name="kvbuild",
    )(ckv, kpe, *consts)


def _attn_tile(q_ref, k_ref, v_ref, o_ref, *, q0, tq, tk, lk_true):
    lim_first = min((q0 // CHUNK + 1) * CHUNK, lk_true)
    lim_last = min(((q0 + tq - 1) // CHUNK + 1) * CHUNK, lk_true)
    n_keys = -(-lim_last // tk) * tk
    n_open = lim_first // tk * tk
    outs = []
    for h in range(2):
        hs = slice(h * HEAD_PAD, (h + 1) * HEAD_PAD)
        s = _dot_nt(q_ref[:, hs], k_ref[0:n_keys, hs])
        if n_open < n_keys:
            edge = s[:, n_open:]
            kpos = n_open + lax.broadcasted_iota(I32, edge.shape, 1)
            qpos = q0 + lax.broadcasted_iota(I32, edge.shape, 0)
            edge = jnp.where(kpos < jnp.minimum((qpos // CHUNK + 1) * CHUNK, lk_true), edge, NEG)
            s = edge if n_open == 0 else jnp.concatenate([s[:, :n_open], edge], axis=-1)
        p = jnp.exp(s - jnp.max(s, axis=-1, keepdims=True))
        outs.append(_dot(p.astype(BF16), v_ref[0:n_keys, :]) / jnp.sum(p, axis=-1, keepdims=True))
    lane = lax.broadcasted_iota(I32, outs[0].shape, 1)
    o_ref[...] = jnp.where(lane < V_HEAD, outs[0], outs[1]).astype(BF16)


def _attn_kernel(q_ref, k_ref, v_ref, o_ref, *, n_q, tq, tk, lk_true, q_off):
    i = pl.program_id(2)
    for qi in range(n_q):
        @pl.when(i == qi)
        def _(qi=qi):
            _attn_tile(q_ref, k_ref, v_ref, o_ref, q0=q_off + qi * tq, tq=tq, tk=tk, lk_true=lk_true)


def _attention(q, k, v, lk_true, q_off, tq, tk):
    b, lq, _ = q.shape
    lk = k.shape[1]
    kern = functools.partial(_attn_kernel, n_q=lq // tq, tq=tq, tk=tk, lk_true=lk_true, q_off=q_off)
    return pl.pallas_call(
        kern, grid=(b, MLA_HEADS // 2, lq // tq),
        in_specs=[pl.BlockSpec((None, tq, 2 * HEAD_PAD), lambda bi, hp, i: (bi, i, hp)),
                  pl.BlockSpec((None, lk, 2 * HEAD_PAD), lambda bi, hp, i: (bi, 0, hp)),
                  pl.BlockSpec((None, lk, 2 * V_HEAD), lambda bi, hp, i: (bi, 0, hp))],
        out_specs=pl.BlockSpec((None, tq, 2 * V_HEAD), lambda bi, hp, i: (bi, i, hp)),
        out_shape=jax.ShapeDtypeStruct((b, lq, MLA_WIDTH), BF16),
        compiler_params=_params(3), name="attn",
    )(q, k, v)


def _hgrn_kernel(q_ref, f_ref, i_ref, g_ref, lb_ref, norm_ref, s0_ref, a_ref, sout_ref, st_sc, *, chunk, n_chunks):
    t = pl.program_id(1)

    @pl.when(t == 0)
    def _():
        for h in range(HG_HEADS):
            st_sc[h] = s0_ref[h].T

    row = lax.broadcasted_iota(I32, (chunk, chunk), 0)
    col = lax.broadcasted_iota(I32, (chunk, chunk), 1)
    tri = jnp.where(row >= col, 1.0, 0.0).astype(BF16)
    nsub = chunk // HG_SUB
    sub_row = lax.broadcasted_iota(I32, (nsub, HG_SUB, 1), 1)
    same_block = {}
    m = chunk // 2
    while m >= HG_SUB:
        same_block[2 * m] = jnp.where(row // (2 * m) == col // (2 * m), 1.0, 0.0)
        m //= 2

    def head_chunk(r0, h):
        hs = slice(h * HG_DK, (h + 1) * HG_DK)
        lb = lb_ref[:, hs]
        q = q_ref[pl.ds(r0, chunk), hs]
        v = i_ref[pl.ds(r0, chunk), hs]
        f = lb + (1.0 - lb) * _sigmoid(f_ref[pl.ds(r0, chunk), hs])
        lf = jnp.log(f)
        k = 1.0 - f
        hi = lf.astype(BF16)
        r1 = lf - hi.astype(F32)
        mid = r1.astype(BF16)
        lo = (r1 - mid.astype(F32)).astype(BF16)
        b = _dot(tri, hi) + _dot(tri, mid) + _dot(tri, lo)
        st = st_sc[h]
        o = _dot_nt((q * jnp.exp(b)).astype(BF16), st.astype(BF16))
        a_cross = jnp.zeros((chunk, chunk), F32)
        m = chunk // 2
        while m >= HG_SUB:
            blk = 2 * m
            shp = (chunk // blk, blk, HG_DK)
            b3 = b.reshape(shp)
            ref = b3[:, m - 1:m, :]
            first = lax.broadcasted_iota(I32, (chunk // blk, blk, 1), 1) < m
            decay = jnp.exp(-jnp.abs(b3 - ref))
            ql = jnp.where(first, 0.0, q.reshape(shp) * decay)
            kl = jnp.where(first, k.reshape(shp) * decay, 0.0)
            a_l = _dot_nt(ql.reshape(chunk, HG_DK).astype(BF16), kl.reshape(chunk, HG_DK).astype(BF16))
            a_cross = a_cross + a_l * same_block[blk]
            m //= 2
        o = o + _dot(a_cross.astype(BF16), v.astype(BF16))
        shp = (nsub, HG_SUB, HG_DK)
        q3, k3, b3, v3 = q.reshape(shp), k.reshape(shp), b.reshape(shp), v.reshape(shp)
        od = jnp.zeros(shp, F32)
        for s in range(HG_SUB):
            w = q3 * jnp.exp(b3 - b3[:, s:s + 1, :]) * k3[:, s:s + 1, :]
            a_col = jnp.where(sub_row >= s, jnp.sum(w, axis=-1, keepdims=True), 0.0)
            od = od + a_col * v3[:, s:s + 1, :]
        o = o + od.reshape(chunk, HG_DV)
        b_last = b[chunk - 1:chunk, :]
        kd = k * jnp.exp(b_last - b)
        st_sc[h] = st * jnp.exp(b_last) + _dot_tn(v.astype(BF16), kd.astype(BF16))
        g = g_ref[pl.ds(r0, chunk), hs]
        a_ref[pl.ds(r0, chunk), hs] = (_rms(o, norm_ref[:, hs]) * (g * _sigmoid(g))).astype(BF16)

    def body(c, carry):
        r0 = pl.multiple_of(c * chunk, chunk)
        for h in range(HG_HEADS):
            head_chunk(r0, h)
        return carry

    lax.fori_loop(0, n_chunks, body, 0)

    @pl.when(t == pl.num_programs(1) - 1)
    def _():
        for h in range(HG_HEADS):
            sout_ref[h] = st_sc[h].T


def _hgrn(hg, lb, norm, s0, lt, chunk):
    b, l, _ = hg.shape
    kern = functools.partial(_hgrn_kernel, chunk=chunk, n_chunks=lt // chunk)
    seg = lambda j: pl.BlockSpec((None, lt, HG_WIDTH), lambda bi, t: (bi, t, j))
    head_vec = _const_spec((1, HG_WIDTH))
    state = pl.BlockSpec((None, HG_HEADS, HG_DK, HG_DV), lambda bi, t: (bi, 0, 0, 0))
    return pl.pallas_call(
        kern, grid=(b, l // lt),
        in_specs=[seg(0), seg(1), seg(2), seg(3), head_vec, head_vec, state],
        out_specs=(pl.BlockSpec((None, lt, HG_WIDTH), lambda bi, t: (bi, t, 0)), state),
        out_shape=(jax.ShapeDtypeStruct((b, l, HG_WIDTH), BF16), jax.ShapeDtypeStruct(s0.shape, F32)),
        scratch_shapes=[pltpu.VMEM((HG_HEADS, HG_DV, HG_DK), F32)],
        compiler_params=_params(2), name="hgrn",
    )(hg, hg, hg, hg, lb, norm, s0)


def _memkv_kernel(m_ref, g_ref, wk_ref, wv_ref, k_ref, v_ref):
    m = _rms(m_ref[...], g_ref[...]).astype(BF16)
    k_ref[...] = _dot(m, wk_ref[...])
    v_ref[...] = _dot(m, wv_ref[...])


def _memkv(mem, lw, tm):
    n = mem.shape[0]
    tok = lambda i: (i, 0)
    consts = [lw["g_mem"], lw["w_mk"], lw["w_mv"]]
    out_shape = (jax.ShapeDtypeStruct((n, X_WIDTH), F32),) * 2
    return pl.pallas_call(
        _memkv_kernel, grid=(n // tm,),
        in_specs=[pl.BlockSpec((tm, D_MODEL), tok)] + [_const_spec(c.shape) for c in consts],
        out_specs=(pl.BlockSpec((tm, X_WIDTH), tok),) * 2, out_shape=out_shape,
        compiler_params=_params(1), name="memkv",
    )(mem, *consts)


def _merge_kernel(x_ref, a_ref, bo_ref, sg_ref, mk_ref, mv_ref, wpa_ref, wpb_ref, wout_ref, gx_ref, wxq_ref, wxo_ref,
                  gffn_ref, wr_ref, br_ref, *rest, n_seg, seg, n_tiles):
    h_ref, u_ref, lg_ref = rest[-3:]

    @pl.when(pl.program_id(0) >= n_tiles)
    def _():
        u_ref[...] = jnp.zeros(u_ref.shape, F32)
        lg_ref[...] = jnp.zeros(lg_ref.shape, F32)

    @pl.when(pl.program_id(0) < n_tiles)
    def _():
        _merge_tile(x_ref, a_ref, bo_ref, sg_ref, mk_ref, mv_ref, wpa_ref, wpb_ref, wout_ref, gx_ref, wxq_ref, wxo_ref,
                    gffn_ref, wr_ref, br_ref, h_ref, u_ref, lg_ref, n_seg=n_seg, seg=seg)


def _merge_tile(x_ref, a_ref, bo_ref, sg_ref, mk_ref, mv_ref, wpa_ref, wpb_ref, wout_ref, gx_ref, wxq_ref, wxo_ref,
                gffn_ref, wr_ref, br_ref, h_ref, u_ref, lg_ref, *, n_seg, seg):
    sg = sg_ref[...]
    mix = (sg[:, :D_MODEL].astype(F32) * _dot(a_ref[...], wpa_ref[...])
           + sg[:, D_MODEL:].astype(F32) * _dot(bo_ref[...], wpb_ref[...]))
    h1 = x_ref[...] + _dot(mix.astype(BF16), wout_ref[...])
    u2 = _rms(h1, gx_ref[...]).astype(BF16)
    qx = (_dot(u2, wxq_ref[...]) * X_HEAD_DIM ** -0.5).astype(BF16)
    segs = []
    for j in range(n_seg):
        kj = mk_ref[j].astype(BF16)
        vj = mv_ref[j].astype(BF16)
        heads = []
        for h in range(X_HEADS):
            hs = slice(h * X_HEAD_DIM, (h + 1) * X_HEAD_DIM)
            s = _dot_nt(qx[j * seg:(j + 1) * seg, hs], kj[:, hs])
            p = jnp.exp(s - jnp.max(s, axis=-1, keepdims=True))
            heads.append(_dot(p.astype(BF16), vj[:, hs]) / jnp.sum(p, axis=-1, keepdims=True))
        segs.append(jnp.concatenate(heads, axis=-1))
    ox = segs[0] if n_seg == 1 else jnp.concatenate(segs, axis=0)
    h2 = h1 + _dot(ox.astype(BF16), wxo_ref[...])
    h_ref[...] = h2
    u3 = _rms(h2, gffn_ref[...])
    _to_row_tiles(u_ref, u3)
    lg_ref[...] = _dot(u3.astype(BF16), wr_ref[...]) + br_ref[...]


def _merge(x, a, bo, sg, mk, mv, lw, tm, seq_len, t_all, shared=None, tok_off=0):
    t = x.shape[0]
    n_mem = mk.shape[1]
    n_seg = max(1, tm // seq_len)
    seg = tm // n_seg
    tiles_per_seq = max(1, seq_len // tm)
    n_tiles = t // tm
    assert t_all % tm == 0 and tok_off % tm == 0
    n_steps = n_tiles if shared is not None else t_all // tm
    assert shared is not None or tok_off == 0
    tok = lambda i: (jnp.minimum(i, n_tiles - 1), 0)
    out_tok = lambda i: (tok_off // tm + i, 0)
    mem = pl.BlockSpec((n_seg, n_mem, X_WIDTH), lambda i: (jnp.minimum(i, n_tiles - 1) // tiles_per_seq, 0, 0))
    consts = [lw["w_pa"], lw["w_pb"], lw["w_out"], lw["g_x"], lw["w_xq"], lw["w_xo"], lw["g_ffn"], lw["w_router"],
              lw["b_router"]]
    out_shape = (jax.ShapeDtypeStruct((t, D_MODEL), F32), jax.ShapeDtypeStruct((t_all * ROW_TILE, LANES), F32),
                 jax.ShapeDtypeStruct((t_all, LANES), F32))
    out_specs = (pl.BlockSpec((tm, D_MODEL), tok), pl.BlockSpec((tm * ROW_TILE, LANES), out_tok),
                 pl.BlockSpec((tm, LANES), out_tok))
    in_specs = ([pl.BlockSpec((tm, D_MODEL), tok), pl.BlockSpec((tm, HG_WIDTH), tok), pl.BlockSpec((tm, MLA_WIDTH), tok),
                 pl.BlockSpec((tm, 2 * D_MODEL), tok), mem, mem] + [_const_spec(c.shape) for c in consts])
    args = [x, a, bo, sg, mk, mv, *consts]
    aliases = {}
    if shared is not None:
        aliases = {len(args): 1, len(args) + 1: 2}
        in_specs += [pl.BlockSpec(memory_space=pl.ANY)] * 2
        args += list(shared)
    kern = functools.partial(_merge_kernel, n_seg=n_seg, seg=seg, n_tiles=n_tiles)
    return pl.pallas_call(
        kern, grid=(n_steps,), in_specs=in_specs, out_specs=out_specs, out_shape=out_shape, input_output_aliases=aliases,
        compiler_params=_params(1), name="merge",
    )(*args)


def _route_kernel(lg_ref, gate_ref, ir_ref, cum_ref, cnt_ref, carry_sc, *, tr):
    i = pl.program_id(0)

    @pl.when(i == 0)
    def _():
        carry_sc[...] = jnp.zeros(carry_sc.shape, F32)

    lane_i = lax.broadcasted_iota(I32, (tr, LANES), 1)
    lane = lane_i.astype(F32)
    logit = jnp.where(lane < N_EXPERTS, lg_ref[...], -jnp.inf)
    vals, idxs, hots = [], [], []
    for _ in range(TOP_K):
        top = jnp.max(logit, axis=-1, keepdims=True)
        first = jnp.min(jnp.where(logit == top, lane, float(LANES)), axis=-1, keepdims=True)
        hot = lane == first
        vals.append(top)
        idxs.append(first)
        hots.append(hot)
        logit = jnp.where(hot, -jnp.inf, logit)
    exps = [jnp.exp(v - vals[0]) for v in vals]
    denom = exps[0] + exps[1] + exps[2] + exps[3]
    hot_all = sum(jnp.where(h, 1.0, 0.0) for h in hots)
    row = lax.broadcasted_iota(I32, (tr, tr), 0)
    col = lax.broadcasted_iota(I32, (tr, tr), 1)
    before = jnp.where(row > col, 1.0, 0.0).astype(BF16)
    carry = carry_sc[...]
    prefix = _dot(before, hot_all.astype(BF16)) + carry
    lane4 = lax.broadcasted_iota(I32, (tr, TOP_K), 1)
    gate = jnp.zeros((tr, TOP_K), F32)
    cols = jnp.zeros((tr, LANES), F32)
    for k in range(TOP_K):
        rk = jnp.sum(jnp.where(hots[k], prefix, 0.0), axis=-1, keepdims=True)
        gate = jnp.where(lane4 == k, exps[k] / denom, gate)
        cols = jnp.where(lane_i == k, idxs[k], jnp.where(lane_i == TOP_K + k, rk, cols))
    gate_ref[...] = gate
    ir_ref[i] = cols.T[0:2 * TOP_K, :]
    cum_ref[pl.ds(i, 1), :] = carry
    carry = carry + jnp.sum(hot_all, axis=0, keepdims=True)
    carry_sc[...] = carry
    cnt_ref[...] = carry


def _route(logits, tr):
    t = logits.shape[0]
    n_tiles = t // tr
    tok = lambda i: (i, 0)
    out_shape = (jax.ShapeDtypeStruct((t, TOP_K), F32), jax.ShapeDtypeStruct((n_tiles, 2 * TOP_K, tr), F32),
                 jax.ShapeDtypeStruct((n_tiles, LANES), F32), jax.ShapeDtypeStruct((1, LANES), F32))
    return pl.pallas_call(
        functools.partial(_route_kernel, tr=tr), grid=(n_tiles,),
        in_specs=[pl.BlockSpec((tr, LANES), tok)],
        out_specs=(pl.BlockSpec((tr, TOP_K), tok), pl.BlockSpec((n_tiles, 2 * TOP_K, tr), lambda i: (0, 0, 0)),
                   pl.BlockSpec((n_tiles, LANES), lambda i: (0, 0)), pl.BlockSpec((1, LANES), lambda i: (0, 0))),
        out_shape=out_shape, scratch_shapes=[pltpu.VMEM((1, LANES), F32)],
        compiler_params=_params(1), name="route",
    )(logits)


def _rowlist_kernel(be_ref, j0_ref, tlo_ref, thi_ref, ir_ref, out_ref, *, tr, bm, group):
    g = pl.program_id(0)
    n_fold = tr // LANES
    tl = lax.broadcasted_iota(I32, (1, tr), 1)
    lo_code = (tl % 256 + 1).astype(F32).astype(BF16)
    hi_code = (tl // 256 * TOP_K + 1).astype(F32)
    jrow = lax.broadcasted_iota(I32, (bm, tr), 0).astype(F32).astype(BF16)
    kcol = lax.broadcasted_iota(I32, (TOP_K, tr), 0).astype(F32)
    lane = lax.broadcasted_iota(I32, (bm, LANES), 1)
    zero = jnp.zeros((), BF16)

    def fold(x):
        parts = [x[:, c * LANES:(c + 1) * LANES] for c in range(n_fold)]
        while len(parts) > 1:
            parts = [jnp.maximum(parts[c], parts[c + 1]) for c in range(0, len(parts), 2)]
        return parts[0]

    for q in range(group):
        b = g * group + q
        e = be_ref[b].astype(F32)
        j0 = j0_ref[b].astype(F32)

        def tile(t, planes, e=e, j0=j0):
            p_lo, p_hk, p_tile = planes
            ir = ir_ref[t]
            mine = ir[0:TOP_K, :] == e
            rel = jnp.sum(jnp.where(mine, ir[TOP_K:2 * TOP_K, :] - j0, 0.0), axis=0, keepdims=True)
            hit = jnp.sum(jnp.where(mine, 1.0, 0.0), axis=0, keepdims=True) > 0.0
            slot = jnp.sum(jnp.where(mine, kcol, 0.0), axis=0, keepdims=True)
            rel = jnp.where(hit, jnp.clip(rel, -1.0, float(bm)), -1.0).astype(BF16)
            match = jnp.broadcast_to(rel, (bm, tr)) == jrow
            f_lo = fold(jnp.where(match, jnp.broadcast_to(lo_code, (bm, tr)), zero))
            f_hk = fold(jnp.where(match, jnp.broadcast_to((hi_code + slot).astype(BF16), (bm, tr)), zero))
            tile_no = jnp.broadcast_to(jnp.full((1, LANES), t + 1, I32).astype(F32).astype(BF16), (bm, LANES))
            return jnp.maximum(p_lo, f_lo), jnp.maximum(p_hk, f_hk), jnp.where(f_lo > zero, tile_no, p_tile)

        empty = jnp.zeros((bm, LANES), BF16)
        p_lo, p_hk, p_tile = lax.fori_loop(tlo_ref[b], thi_ref[b], tile, (empty, empty, empty))
        lo = jnp.max(p_lo.astype(F32), axis=-1, keepdims=True)
        hk = jnp.max(p_hk.astype(F32), axis=-1, keepdims=True) - 1.0
        tile_no = jnp.max(p_tile.astype(F32), axis=-1, keepdims=True) - 1.0
        found = lo > 0.0
        hi = jnp.floor(hk * (1.0 / TOP_K))
        tok = jnp.where(found, tile_no * float(tr) + hi * 256.0 + lo - 1.0, 0.0)
        k = jnp.where(found, hk - hi * TOP_K, 0.0)
        cols = jnp.where(lane == 0, tok, jnp.where(lane == 1, k, jnp.where(lane == 2, jnp.where(found, 1.0, 0.0), 0.0)))
        out_ref[q] = cols.T[0:ROW_TILE, :].astype(I32)


def _rowlist(ir, block_expert, block_j0, tile_lo, tile_hi, bm):
    n_tiles, _, tr = ir.shape
    n_blocks = block_expert.shape[0]
    assert n_tiles < 256 and bm <= 256 and tr <= 8192
    group = 8
    while n_blocks % group:
        group //= 2
    grid_spec = pltpu.PrefetchScalarGridSpec(
        num_scalar_prefetch=4, grid=(n_blocks // group,),
        in_specs=[pl.BlockSpec((n_tiles, 2 * TOP_K, tr), lambda g, *_: (0, 0, 0))],
        out_specs=pl.BlockSpec((group, ROW_TILE, bm), lambda g, *_: (g, 0, 0)))
    return pl.pallas_call(
        functools.partial(_rowlist_kernel, tr=tr, bm=bm, group=group), grid_spec=grid_spec,
        out_shape=jax.ShapeDtypeStruct((n_blocks, ROW_TILE, bm), I32),
        compiler_params=_params(1), name="rowlist",
    )(block_expert, block_j0, tile_lo, tile_hi, ir)


def _to_row_tiles(ref, x):
    n = x.shape[0]
    for c in range(ROW_TILE):
        ref[pl.ds(c, n, stride=ROW_TILE), :] = x[:, c * LANES:(c + 1) * LANES]


def _row_tile_chunk(ref, n, c):
    return ref[pl.ds(c, n, stride=ROW_TILE), :]


def _moe_kernel(be_ref, *refs, bm, n_blocks):
    rt_first = refs[:MOE_AHEAD]
    rtn_ref, rdp_ref, x_hbm, wgu_ref, bgu_ref, wd_ref, bd_ref, y_hbm = refs[MOE_AHEAD:MOE_AHEAD + 8]
    scratch = refs[MOE_AHEAD + 8:]
    xs, ys = scratch[:MOE_BUFFERS], scratch[MOE_BUFFERS:2 * MOE_BUFFERS]
    wgu_bf, wd_bf, gsem, ssem = scratch[2 * MOE_BUFFERS:]
    b = pl.program_id(0)
    rows = bm * ROW_TILE

    def gather(idx_ref, r, slot):
        src = pl.multiple_of(idx_ref[0, 0, r], ROW_TILE)
        return pltpu.make_async_copy(x_hbm.at[pl.ds(src, ROW_TILE), :], xs[slot].at[pl.ds(r * ROW_TILE, ROW_TILE), :],
                                     gsem.at[slot])

    def scatter(r, slot):
        dst = pl.multiple_of(rdp_ref[0, 0, r], ROW_TILE)
        return pltpu.make_async_copy(ys[slot].at[pl.ds(r * ROW_TILE, ROW_TILE), :], y_hbm.at[pl.ds(dst, ROW_TILE), :],
                                     ssem.at[slot])

    def wait_gathers(slot):
        pltpu.make_async_copy(x_hbm.at[pl.ds(0, rows), :], xs[slot], gsem.at[slot]).wait()

    def wait_scatters(slot):
        pltpu.make_async_copy(ys[slot], y_hbm.at[pl.ds(0, rows), :], ssem.at[slot]).wait()

    @pl.when(b == 0)
    def _():
        ys[MOE_BUFFERS - 1][...] = jnp.zeros(ys[MOE_BUFFERS - 1].shape, F32)
        for j in range(MOE_AHEAD):
            def start(r, c, j=j):
                gather(rt_first[j], r, j).start()
                return c

            lax.fori_loop(0, bm, start, 0)

    def step(slot):
        prev = (slot - 1) % MOE_BUFFERS
        wait_gathers(slot)

        @pl.when(b >= MOE_BUFFERS - 1)
        def _():
            wait_scatters(slot)

        for r in range(bm):
            gather(rtn_ref, r, prev).start()
        for r in range(bm):
            scatter(r, prev).start(priority=1)
        x = jnp.concatenate([_row_tile_chunk(xs[slot], bm, c) for c in range(ROW_TILE)], axis=-1).astype(BF16)
        gu = _dot(x, wgu_bf[...]) + bgu_ref[...]
        g = jnp.minimum(gu[:, :D_FF], SWIGLU_LIMIT)
        up = jnp.clip(gu[:, D_FF:], -SWIGLU_LIMIT, SWIGLU_LIMIT)
        act = (up + 1.0) * g * _sigmoid(SWIGLU_ALPHA * g)
        _to_row_tiles(ys[slot], _dot(act.astype(BF16), wd_bf[...]) + bd_ref[...])

    expert = be_ref[jnp.minimum(b, n_blocks - 1)]
    new_expert = jnp.logical_or(b == 0, expert != be_ref[jnp.maximum(b - 1, 0)])

    @pl.when(jnp.logical_and(b < n_blocks, new_expert))
    def _():
        wgu_bf[...] = wgu_ref[...].astype(BF16)
        wd_bf[...] = wd_ref[...].astype(BF16)

    for v in range(MOE_BUFFERS):
        @pl.when(jnp.logical_and(b < n_blocks, b % MOE_BUFFERS == v))
        def _(v=v):
            step(v)

    @pl.when(b == n_blocks)
    def _():
        for j in range(MOE_AHEAD):
            wait_gathers((n_blocks + j) % MOE_BUFFERS)
        for j in range(2, MOE_BUFFERS + 1):
            wait_scatters((n_blocks - j) % MOE_BUFFERS)
        last = (n_blocks - 1) % MOE_BUFFERS

        def start(r, c):
            scatter(r, last).start()
            return c

        lax.fori_loop(0, bm, start, 0)
        wait_scatters(last)


def _moe(x, block_expert, row_src, row_dst, lw, bm, n_out_rows):
    n_blocks = block_expert.shape[0]
    assert n_blocks >= MOE_BUFFERS
    spare = (n_out_rows // ROW_TILE - bm + jnp.arange(bm, dtype=I32)) * ROW_TILE
    src = row_src.reshape(n_blocks, 1, bm)
    dst_prev = jnp.concatenate([spare.reshape(1, 1, bm), row_dst.reshape(n_blocks, 1, bm)], axis=0)
    smem = lambda f: pl.BlockSpec((1, 1, bm), f, memory_space=pltpu.SMEM)
    blk = lambda b: jnp.minimum(b, n_blocks - 1)
    first = [smem(lambda b, be, j=j: (j, 0, 0)) for j in range(MOE_AHEAD)]
    grid_spec = pltpu.PrefetchScalarGridSpec(
        num_scalar_prefetch=1, grid=(n_blocks + 1,),
        in_specs=first + [smem(lambda b, be: (blk(b + MOE_AHEAD), 0, 0)), smem(lambda b, be: (b, 0, 0)),
                  pl.BlockSpec(memory_space=pl.ANY),
                  pl.BlockSpec((None, D_MODEL, 2 * D_FF), lambda b, be: (be[blk(b)], 0, 0)),
                  pl.BlockSpec((None, 1, 2 * D_FF), lambda b, be: (be[blk(b)], 0, 0)),
                  pl.BlockSpec((None, D_FF, D_MODEL), lambda b, be: (be[blk(b)], 0, 0)),
                  pl.BlockSpec((None, 1, D_MODEL), lambda b, be: (be[blk(b)], 0, 0))],
        out_specs=pl.BlockSpec(memory_space=pl.ANY),
        scratch_shapes=[pltpu.VMEM((bm * ROW_TILE, LANES), F32)] * (2 * MOE_BUFFERS)
        + [pltpu.VMEM((D_MODEL, 2 * D_FF), BF16), pltpu.VMEM((D_FF, D_MODEL), BF16)]
        + [pltpu.SemaphoreType.DMA((MOE_BUFFERS,))] * 2)
    return pl.pallas_call(
        functools.partial(_moe_kernel, bm=bm, n_blocks=n_blocks), grid_spec=grid_spec,
        out_shape=jax.ShapeDtypeStruct((n_out_rows, LANES), F32),
        compiler_params=_params(1), name="moe",
    )(block_expert, *([src] * (MOE_AHEAD + 1)), dst_prev, x, lw["w_gu"], lw["b_gu"], lw["w_down"], lw["b_down"])


def _final_kernel(h_ref, gate_ref, y0_ref, y1_ref, y2_ref, y3_ref, g_ref, o_ref, *, tm):
    gate = gate_ref[...]
    gates = [jnp.broadcast_to(gate[:, k:k + 1], (tm, LANES)) for k in range(TOP_K)]
    ssq = jnp.zeros((tm, 1), F32)
    for c in range(ROW_TILE):
        cs = slice(c * LANES, (c + 1) * LANES)
        h = h_ref[:, cs]
        for k, y_ref in enumerate((y0_ref, y1_ref, y2_ref, y3_ref)):
            h = h + gates[k] * _row_tile_chunk(y_ref, tm, c)
        o_ref[:, cs] = h
        ssq = ssq + jnp.sum(h * h, axis=-1, keepdims=True)
    o_ref[...] = o_ref[...] * lax.rsqrt(ssq * (1.0 / D_MODEL) + EPS) * g_ref[...]


def _final(h, gate, y4, g_final, tm, tok_off, t_all):
    t = h.shape[0]
    tok = lambda i: (i, 0)
    slot = lambda k: pl.BlockSpec((tm * ROW_TILE, LANES), lambda i: ((k * t_all + tok_off) // tm + i, 0))
    return pl.pallas_call(
        functools.partial(_final_kernel, tm=tm), grid=(t // tm,),
        in_specs=[pl.BlockSpec((tm, D_MODEL), tok), pl.BlockSpec((tm, TOP_K), lambda i: (tok_off // tm + i, 0)),
                  slot(0), slot(1), slot(2), slot(3), _const_spec(g_final.shape)],
        out_specs=pl.BlockSpec((tm, D_MODEL), tok), out_shape=jax.ShapeDtypeStruct((t, D_MODEL), F32),
        compiler_params=_params(1), name="final",
    )(h, gate, y4, y4, y4, y4, g_final)


def _rope_tables(pos, reps):
    half = QK_ROPE // 2
    inv_freq = jnp.exp(-math.log(ROPE_THETA) * jnp.arange(half, dtype=F32) / half)
    ang = pos.astype(F32)[:, None] * inv_freq[None, :]
    cos = jnp.concatenate([jnp.cos(ang)] * 2, axis=-1)
    sin = jnp.concatenate([jnp.sin(ang)] * 2, axis=-1)
    n = pos.shape[0]
    ones = jnp.ones((n, QK_NOPE), F32)
    z = lambda w: jnp.zeros((n, w), F32)
    tabs = {"cosq": jnp.concatenate([ones, cos, z(HEAD_PAD - QK_NOPE - QK_ROPE)], axis=-1),
            "sinq": jnp.concatenate([z(QK_NOPE), sin, z(HEAD_PAD - QK_NOPE - QK_ROPE)], axis=-1),
            "cosk": jnp.concatenate([cos, z(LANES - QK_ROPE)], axis=-1),
            "sink": jnp.concatenate([sin, z(LANES - QK_ROPE)], axis=-1)}
    return {k: jnp.tile(v, (reps, 1)) for k, v in tabs.items()}


def _rot_cols(w):
    half = QK_ROPE // 2
    return jnp.concatenate([-w[..., half:], w[..., :half]], axis=-1)


def _layer_weights(l, lb, g_mix, w_in, g_qa, w_uq, g_kva, w_ukv, hg_norm, w_pa, w_pb, w_out, g_mem, w_mk, w_mv, g_x, w_xq,
                   w_xo, g_ffn, w_router, b_router, w_gu, b_gu, w_down, b_down):
    row = lambda v: v.reshape(1, -1).astype(F32)
    wi = w_in[l]
    o_qa = 4 * HG_WIDTH
    o_kv = o_qa + Q_LORA
    o_kr = o_kv + KV_LORA
    o_g = o_kr + QK_ROPE
    w_kr = wi[:, o_kr:o_g]
    zk = jnp.zeros((D_MODEL, LANES - QK_ROPE), F32)
    uq = w_uq[l].reshape(Q_LORA, MLA_HEADS, QK_NOPE + QK_ROPE)
    pad = HEAD_PAD - QK_NOPE - QK_ROPE
    assert pad == QK_ROPE
    w_q = jnp.concatenate([uq, _rot_cols(uq[..., QK_NOPE:])], axis=-1).reshape(Q_LORA, MLA_HEADS * HEAD_PAD)
    ukv = w_ukv[l].reshape(KV_LORA, MLA_HEADS, QK_NOPE + V_HEAD)
    w_kn = jnp.concatenate([ukv[..., :QK_NOPE], jnp.zeros((KV_LORA, MLA_HEADS, HEAD_PAD - QK_NOPE), F32)], axis=-1)
    e_head = jnp.concatenate([jnp.zeros((QK_ROPE, QK_NOPE), F32), jnp.eye(QK_ROPE, dtype=F32),
                              jnp.zeros((QK_ROPE, pad), F32)], axis=-1)
    e_kpe = jnp.concatenate([jnp.tile(e_head, (1, MLA_HEADS)), jnp.zeros((LANES - QK_ROPE, MLA_HEADS * HEAD_PAD), F32)], axis=0)
    bf = lambda w: w.astype(BF16)
    return {
        "lb": row(lb[l]), "g_mix": row(g_mix[l]), "w_hg": bf(wi[:, :o_qa]), "w_g": bf(wi[:, o_g:]),
        "w_qa": bf(wi[:, o_qa:o_kv]), "w_kv": bf(wi[:, o_kv:o_kr]),
        "w_kr": bf(jnp.concatenate([w_kr, zk, _rot_cols(w_kr), zk], axis=-1)),
        "g_qa": row(g_qa[l]), "w_q": bf(w_q), "g_kva": row(g_kva[l]),
        "w_kn": bf(w_kn.reshape(KV_LORA, MLA_HEADS * HEAD_PAD)), "w_v": bf(ukv[..., QK_NOPE:].reshape(KV_LORA, MLA_WIDTH)),
        "e_kpe": bf(e_kpe), "hg_norm": row(hg_norm[l]), "w_pa": bf(w_pa[l]), "w_pb": bf(w_pb[l]), "w_out": bf(w_out[l]),
        "g_mem": row(g_mem[l]), "w_mk": bf(w_mk[l]), "w_mv": bf(w_mv[l]), "g_x": row(g_x[l]), "w_xq": bf(w_xq[l]),
        "w_xo": bf(w_xo[l]), "g_ffn": row(g_ffn[l]),
        "w_router": bf(jnp.pad(w_router[l], ((0, 0), (0, LANES - N_EXPERTS)))),
        "b_router": jnp.pad(row(b_router[l]), ((0, 0), (0, LANES - N_EXPERTS))),
        "w_gu": w_gu[l], "b_gu": b_gu[l].reshape(N_EXPERTS, 1, 2 * D_FF).astype(F32),
        "w_down": w_down[l], "b_down": b_down[l].reshape(N_EXPERTS, 1, D_MODEL).astype(F32),
    }


def _tile(n, want):
    t = min(n, want)
    while n % t:
        t -= 8
    assert t > 0 and n % t == 0, (n, want)
    return t


def _group_front(x, lw, tabs, seq_len, hg_state, past):
    b, l, _ = x.shape
    t = b * l
    tm = _tile(t, TOKEN_TILE)
    hg, sg, q, ckv, kpe = _inproj(x.reshape(t, D_MODEL), lw, tabs, tm)
    lt = _tile(l, 512)
    a, s_new = _hgrn(hg.reshape(b, l, 4 * HG_WIDTH), lw["lb"], lw["hg_norm"], hg_state, lt, _tile(lt, HG_CHUNK))
    tq = _tile(l, ATTN_Q_TILE)
    if past is None:
        lk_true, q_off = l, 0
        ckv_all, kpe_all = ckv, kpe
        lk = l
    else:
        past_ckv, past_kpe = past
        q_off = past_ckv.shape[1]
        lk_true = q_off + l
        lk = -(-lk_true // ATTN_TILE) * ATTN_TILE
        padr = ((0, 0), (0, lk - lk_true), (0, 0))
        ckv_all = jnp.pad(jnp.concatenate([past_ckv, ckv.reshape(b, l, KV_LORA)], axis=1), padr).reshape(b * lk, KV_LORA)
        past_kpe = jnp.pad(past_kpe, ((0, 0), (0, 0), (0, LANES - QK_ROPE)))
        kpe_all = jnp.pad(jnp.concatenate([past_kpe, kpe.reshape(b, l, LANES)], axis=1), padr).reshape(b * lk, LANES)
    k_arr, v_arr = _kvbuild(ckv_all, kpe_all, lw, _tile(b * lk, KV_TILE))
    bo = _attention(q.reshape(b, l, MLA_HEADS * HEAD_PAD), k_arr.reshape(b, lk, MLA_HEADS * HEAD_PAD),
                    v_arr.reshape(b, lk, MLA_WIDTH), lk_true, q_off, tq, _tile(lk, ATTN_TILE))
    return a.reshape(t, HG_WIDTH), bo.reshape(t, MLA_WIDTH), sg, ckv.reshape(b, l, KV_LORA), kpe[:, :QK_ROPE].reshape(b, l, QK_ROPE), s_new


def kernel(x_prompt, x_sample, cache_mla_ckv, cache_mla_kpe, state_hgrn, cache_mem_k, cache_mem_v, mem_prompt, hg_lb_logits, g_mix, w_in, g_qa, w_uq, g_kva, w_ukv, hg_norm, w_pa, w_pb, w_out, g_mem, w_mk, w_mv, g_x, w_xq, w_xo, g_ffn, w_router, b_router, w_gu, b_gu, w_down, b_down, g_final):
    bp, lp, _ = x_prompt.shape
    bs, ls, _ = x_sample.shape
    depth = w_in.shape[0]
    past_len = cache_mla_ckv.shape[2]
    n_mem = mem_prompt.shape[1]
    tp, ts = bp * lp, bs * ls
    t_all = tp + ts
    tm_p, tm_s = _tile(tp, TOKEN_TILE), _tile(ts, TOKEN_TILE)
    assert tp % tm_s == 0
    lb_all = jnp.cumsum(jax.nn.softmax(hg_lb_logits.astype(F32), axis=0), axis=0)
    tabs_p = _rope_tables(jnp.arange(lp, dtype=I32), max(1, tm_p // lp))
    tabs_s = _rope_tables(past_len + jnp.arange(ls, dtype=I32), max(1, tm_s // ls))
    g_fin = g_final.reshape(1, D_MODEL).astype(F32)

    n_asg = t_all * TOP_K
    bm = MOE_ROWS
    n_blocks = (n_asg + N_EXPERTS * (bm - 1) + bm - 1) // bm
    n_rows = n_blocks * bm

    hp, hs = x_prompt.reshape(tp, D_MODEL), x_sample.reshape(ts, D_MODEL)
    outs = [[] for _ in range(8)]
    for l in range(depth):
        lw = _layer_weights(l, lb_all, g_mix, w_in, g_qa, w_uq, g_kva, w_ukv, hg_norm, w_pa, w_pb, w_out, g_mem, w_mk, w_mv,
                            g_x, w_xq, w_xo, g_ffn, w_router, b_router, w_gu, b_gu, w_down, b_down)
        mk, mv = _memkv(mem_prompt.reshape(bp * n_mem, D_MODEL), lw, _tile(bp * n_mem, TOKEN_TILE))
        mk, mv = mk.reshape(bp, n_mem, X_WIDTH), mv.reshape(bp, n_mem, X_WIDTH)
        zero_state = jnp.zeros((bp, HG_HEADS, HG_DK, HG_DV), F32)
        a_p, bo_p, sg_p, ckv_p, kpe_p, st_p = _group_front(hp.reshape(bp, lp, D_MODEL), lw, tabs_p, lp, zero_state, None)
        a_s, bo_s, sg_s, ckv_s, kpe_s, st_s = _group_front(hs.reshape(bs, ls, D_MODEL), lw, tabs_s, ls, state_hgrn[l],
                                                           (cache_mla_ckv[l], cache_mla_kpe[l]))
        h2_p, u3, lg = _merge(hp, a_p, bo_p, sg_p, mk, mv, lw, tm_p, lp, t_all)
        h2_s, u3, lg = _merge(hs, a_s, bo_s, sg_s, cache_mem_k[l].reshape(bs, n_mem, X_WIDTH),
                              cache_mem_v[l].reshape(bs, n_mem, X_WIDTH), lw, tm_s, ls, t_all, shared=(u3, lg), tok_off=tp)
        gate, ir, cum, cnt = _route(lg, _tile(t_all, TOKEN_TILE))
        counts = cnt[0, :N_EXPERTS].astype(I32)
        padded = (counts + bm - 1) // bm * bm
        pend = jnp.cumsum(padded)

        def owner(start):
            return jnp.minimum(jnp.sum((pend[None, :] <= start[:, None]).astype(I32), axis=1), N_EXPERTS - 1)

        block_expert = owner(jnp.arange(n_blocks, dtype=I32) * bm)
        sub_start = jnp.arange(n_rows // ROWLIST_ROWS, dtype=I32) * ROWLIST_ROWS
        sub_expert = owner(sub_start)
        sub_j0 = sub_start - (pend - padded)[sub_expert]
        cum_be = jnp.take(cum[:, :N_EXPERTS].astype(I32), sub_expert, axis=1)
        tile_lo = jnp.maximum(jnp.sum((cum_be <= sub_j0[None, :]).astype(I32), axis=0) - 1, 0)
        tile_hi = jnp.sum((cum_be < (sub_j0 + ROWLIST_ROWS)[None, :]).astype(I32), axis=0)
        rows = _rowlist(ir, sub_expert, sub_j0, tile_lo, tile_hi, ROWLIST_ROWS)
        row_t, row_k, pad_row = rows[:, 0, :].reshape(-1), rows[:, 1, :].reshape(-1), rows[:, 2, :].reshape(-1) == 0
        row_src = jnp.where(pad_row, 0, row_t) * ROW_TILE
        r = jnp.arange(n_rows, dtype=I32)
        spare = n_asg + (r // bm % MOE_BUFFERS) * bm + r % bm
        row_dst = jnp.where(pad_row, spare, row_k * t_all + row_t) * ROW_TILE
        y4 = _moe(u3, block_expert, row_src, row_dst, lw, bm, (n_asg + MOE_BUFFERS * bm) * ROW_TILE)
        last = l == depth - 1
        gf = g_fin if last else jnp.ones_like(g_fin)
        yp = _final(h2_p, gate, y4, gf, tm_p, 0, t_all)
        ys = _final(h2_s, gate, y4, gf, tm_s, tp, t_all)
        assert last, "multi-layer stacking needs the un-normalised residual stream"
        hp, hs = yp, ys
        for lst, v in zip(outs, (ckv_p, kpe_p, st_p, mk.reshape(bp, n_mem, X_HEADS, X_HEAD_DIM),
                                 mv.reshape(bp, n_mem, X_HEADS, X_HEAD_DIM), ckv_s, kpe_s, st_s)):
            lst.append(v)
    stk = [jnp.stack(o) for o in outs]
    return (hp.reshape(bp, lp, D_MODEL), hs.reshape(bs, ls, D_MODEL), stk[0], stk[1], stk[2], stk[3], stk[4], stk[5], stk[6], stk[7])
```

```python
import functools
import math

import jax
import jax.numpy as jnp
from jax import lax
from jax.experimental import pallas as pl
from jax.experimental.pallas import tpu as pltpu

F32 = jnp.float32
BF16 = jnp.bfloat16
I32 = jnp.int32

D_MODEL = 1024
CHUNK = 64
EPS = 1e-6
HG_HEADS = 4
HG_DK = 128
HG_DV = 128
HG_WIDTH = HG_HEADS * HG_DV
HG_SUB = 8
HG_CHUNK = 128
MLA_HEADS = 8
Q_LORA = 384
KV_LORA = 256
QK_NOPE = 64
QK_ROPE = 32
V_HEAD = 64
MLA_WIDTH = MLA_HEADS * V_HEAD
HEAD_PAD = 128
ROPE_THETA = 10000.0
X_HEADS = 4
X_HEAD_DIM = 128
X_WIDTH = X_HEADS * X_HEAD_DIM
N_EXPERTS = 32
TOP_K = 4
D_FF = D_MODEL
SWIGLU_LIMIT = 7.0
SWIGLU_ALPHA = 1.702
LANES = 128
ROW_TILE = D_MODEL // LANES
NEG = -1e30

VMEM_LIMIT = 56 * 1024 * 1024
TOKEN_TILE = 512
KV_TILE = 2048
ATTN_TILE = 256
ATTN_Q_TILE = 512
MOE_ROWS = 256
ROWLIST_ROWS = 256
ROWLIST_WINDOW = 128
ROWLIST_ALIGN = 16
MOE_BUFFERS = 3
MOE_AHEAD = MOE_BUFFERS - 1


def _dot(a, b):
    return jnp.dot(a, b, preferred_element_type=F32)


def _dot_nt(a, b):
    return lax.dot_general(a, b, (((1,), (1,)), ((), ())), preferred_element_type=F32)


def _dot_tn(a, b):
    return lax.dot_general(a, b, (((0,), (0,)), ((), ())), preferred_element_type=F32)


def _rms(x, g):
    return x * lax.rsqrt(jnp.mean(x * x, axis=-1, keepdims=True) + EPS) * g


def _sigmoid(x):
    return 1.0 / (1.0 + jnp.exp(-x))


def _const_spec(shape):
    zeros = (0,) * len(shape)
    return pl.BlockSpec(shape, lambda *_: zeros, pipeline_mode=pl.Buffered(1))


def _params(n_axes):
    return pltpu.CompilerParams(dimension_semantics=("arbitrary",) * n_axes, vmem_limit_bytes=VMEM_LIMIT)


def _inproj_kernel(x_ref, gmix_ref, whg_ref, wg_ref, wqa_ref, wkv_ref, wkr_ref, gqa_ref, wq_ref, gkva_ref,
                   cosq_ref, sinq_ref, cosk_ref, sink_ref,
                   hg_ref, sg_ref, q_ref, ckv_ref, kpe_ref):
    u = _rms(x_ref[...], gmix_ref[...]).astype(BF16)
    hg_ref[...] = _dot(u, whg_ref[...])
    sg_ref[...] = _sigmoid(_dot(u, wg_ref[...])).astype(BF16)
    qn = _rms(_dot(u, wqa_ref[...]), gqa_ref[...]).astype(BF16)
    cosq = jnp.concatenate([cosq_ref[...]] * MLA_HEADS, axis=-1)
    sinq = jnp.concatenate([sinq_ref[...]] * MLA_HEADS, axis=-1)
    scale = (QK_NOPE + QK_ROPE) ** -0.5
    qr = _dot(qn, wq_ref[...])
    q = (qr * cosq + pltpu.roll(qr, MLA_HEADS * HEAD_PAD - QK_ROPE, axis=1) * sinq) * scale
    q_ref[...] = q.astype(BF16)
    ckv_ref[...] = _rms(_dot(u, wkv_ref[...]), gkva_ref[...])
    pk = _dot(u, wkr_ref[...])
    kpe_ref[...] = pk[:, :LANES] * cosk_ref[...] + pk[:, LANES:] * sink_ref[...]


def _inproj(x, lw, tabs, tm):
    t = x.shape[0]
    nt = tabs["cosq"].shape[0] // tm
    tok = lambda i: (i, 0)
    tab = lambda i: (i % nt, 0)
    consts = [lw["g_mix"], lw["w_hg"], lw["w_g"], lw["w_qa"], lw["w_kv"], lw["w_kr"], lw["g_qa"], lw["w_q"],
              lw["g_kva"]]
    in_specs = ([pl.BlockSpec((tm, D_MODEL), tok)] + [_const_spec(c.shape) for c in consts]
                + [pl.BlockSpec((tm, LANES), tab)] * 4)
    out_shape = (jax.ShapeDtypeStruct((t, 4 * HG_WIDTH), F32), jax.ShapeDtypeStruct((t, 2 * D_MODEL), BF16),
                 jax.ShapeDtypeStruct((t, MLA_HEADS * HEAD_PAD), BF16), jax.ShapeDtypeStruct((t, KV_LORA), F32),
                 jax.ShapeDtypeStruct((t, LANES), F32))
    out_specs = tuple(pl.BlockSpec((tm, s.shape[1]), tok) for s in out_shape)
    return pl.pallas_call(
        _inproj_kernel, grid=(t // tm,), in_specs=in_specs, out_specs=out_specs, out_shape=out_shape,
        compiler_params=_params(1), name="inproj",
    )(x, *consts, tabs["cosq"], tabs["sinq"], tabs["cosk"], tabs["sink"])


def _kvbuild_kernel(ckv_ref, kpe_ref, wkn_ref, wv_ref, e_ref, k_ref, v_ref):
    c = ckv_ref[...].astype(BF16)
    k_ref[...] = (_dot(c, wkn_ref[...]) + _dot(kpe_ref[...].astype(BF16), e_ref[...])).astype(BF16)
    v_ref[...] = _dot(c, wv_ref[...]).astype(BF16)


def _kvbuild(ckv, kpe, lw, tm):
    n = ckv.shape[0]
    tok = lambda i: (i, 0)
    consts = [lw["w_kn"], lw["w_v"], lw["e_kpe"]]
    out_shape = (jax.ShapeDtypeStruct((n, MLA_HEADS * HEAD_PAD), BF16), jax.ShapeDtypeStruct((n, MLA_WIDTH), BF16))
    return pl.pallas_call(
        _kvbuild_kernel, grid=(n // tm,),
        in_specs=[pl.BlockSpec((tm, KV_LORA), tok), pl.BlockSpec((tm, LANES), tok)] + [_const_spec(c.shape) for c in consts],
        out_specs=tuple(pl.BlockSpec((tm, s.shape[1]), tok) for s in out_shape), out_shape=out_shape,
        compiler_params=_params(1), name="kvbuild",
    )(ckv, kpe, *consts)


def _attn_tile(q_ref, k_ref, v_ref, o_ref, *, q0, tq, tk, lk_true):
    lim_first = min((q0 // CHUNK + 1) * CHUNK, lk_true)
    lim_last = min(((q0 + tq - 1) // CHUNK + 1) * CHUNK, lk_true)
    n_keys = -(-lim_last // tk) * tk
    n_open = lim_first // tk * tk
    outs = []
    for h in range(2):
        hs = slice(h * HEAD_PAD, (h + 1) * HEAD_PAD)
        s = _dot_nt(q_ref[:, hs], k_ref[0:n_keys, hs])
        if n_open < n_keys:
            edge = s[:, n_open:]
            kpos = n_open + lax.broadcasted_iota(I32, edge.shape, 1)
            qpos = q0 + lax.broadcasted_iota(I32, edge.shape, 0)
            edge = jnp.where(kpos < jnp.minimum((qpos // CHUNK + 1) * CHUNK, lk_true), edge, NEG)
            s = edge if n_open == 0 else jnp.concatenate([s[:, :n_open], edge], axis=-1)
        p = jnp.exp(s - jnp.max(s, axis=-1, keepdims=True))
        outs.append(_dot(p.astype(BF16), v_ref[0:n_keys, :]) / jnp.sum(p, axis=-1, keepdims=True))
    lane = lax.broadcasted_iota(I32, outs[0].shape, 1)
    o_ref[...] = jnp.where(lane < V_HEAD, outs[0], outs[1]).astype(BF16)


def _attn_kernel(q_ref, k_ref, v_ref, o_ref, *, n_q, tq, tk, lk_true, q_off):
    i = pl.program_id(2)
    for qi in range(n_q):
        @pl.when(i == qi)
        def _(qi=qi):
            _attn_tile(q_ref, k_ref, v_ref, o_ref, q0=q_off + qi * tq, tq=tq, tk=tk, lk_true=lk_true)


def _attention(q, k, v, lk_true, q_off, tq, tk):
    b, lq, _ = q.shape
    lk = k.shape[1]
    kern = functools.partial(_attn_kernel, n_q=lq // tq, tq=tq, tk=tk, lk_true=lk_true, q_off=q_off)
    return pl.pallas_call(
        kern, grid=(b, MLA_HEADS // 2, lq // tq),
        in_specs=[pl.BlockSpec((None, tq, 2 * HEAD_PAD), lambda bi, hp, i: (bi, i, hp)),
                  pl.BlockSpec((None, lk, 2 * HEAD_PAD), lambda bi, hp, i: (bi, 0, hp)),
                  pl.BlockSpec((None, lk, 2 * V_HEAD), lambda bi, hp, i: (bi, 0, hp))],
        out_specs=pl.BlockSpec((None, tq, 2 * V_HEAD), lambda bi, hp, i: (bi, i, hp)),
        out_shape=jax.ShapeDtypeStruct((b, lq, MLA_WIDTH), BF16),
        compiler_params=_params(3), name="attn",
    )(q, k, v)


def _hgrn_kernel(q_ref, f_ref, i_ref, g_ref, lb_ref, norm_ref, s0_ref, a_ref, sout_ref, st_sc, *, chunk, n_chunks):
    t = pl.program_id(1)

    @pl.when(t == 0)
    def _():
        for h in range(HG_HEADS):
            st_sc[h] = s0_ref[h].T

    row = lax.broadcasted_iota(I32, (chunk, chunk), 0)
    col = lax.broadcasted_iota(I32, (chunk, chunk), 1)
    tri = jnp.where(row >= col, 1.0, 0.0).astype(BF16)
    nsub = chunk // HG_SUB
    sub_row = lax.broadcasted_iota(I32, (nsub, HG_SUB, 1), 1)
    same_block = {}
    m = chunk // 2
    while m >= HG_SUB:
        same_block[2 * m] = jnp.where(row // (2 * m) == col // (2 * m), 1.0, 0.0)
        m //= 2

    def head_chunk(r0, h):
        hs = slice(h * HG_DK, (h + 1) * HG_DK)
        lb = lb_ref[:, hs]
        q = q_ref[pl.ds(r0, chunk), hs]
        v = i_ref[pl.ds(r0, chunk), hs]
        f = lb + (1.0 - lb) * _sigmoid(f_ref[pl.ds(r0, chunk), hs])
        lf = jnp.log(f)
        k = 1.0 - f
        hi = lf.astype(BF16)
        r1 = lf - hi.astype(F32)
        mid = r1.astype(BF16)
        lo = (r1 - mid.astype(F32)).astype(BF16)
        b = _dot(tri, hi) + _dot(tri, mid) + _dot(tri, lo)
        st = st_sc[h]
        o = _dot_nt((q * jnp.exp(b)).astype(BF16), st.astype(BF16))
        a_cross = jnp.zeros((chunk, chunk), F32)
        m = chunk // 2
        while m >= HG_SUB:
            blk = 2 * m
            shp = (chunk // blk, blk, HG_DK)
            b3 = b.reshape(shp)
            ref = b3[:, m - 1:m, :]
            first = lax.broadcasted_iota(I32, (chunk // blk, blk, 1), 1) < m
            decay = jnp.exp(-jnp.abs(b3 - ref))
            ql = jnp.where(first, 0.0, q.reshape(shp) * decay)
            kl = jnp.where(first, k.reshape(shp) * decay, 0.0)
            a_l = _dot_nt(ql.reshape(chunk, HG_DK).astype(BF16), kl.reshape(chunk, HG_DK).astype(BF16))
            a_cross = a_cross + a_l * same_block[blk]
            m //= 2
        o = o + _dot(a_cross.astype(BF16), v.astype(BF16))
        shp = (nsub, HG_SUB, HG_DK)
        q3, k3, b3, v3 = q.reshape(shp), k.reshape(shp), b.reshape(shp), v.reshape(shp)
        od = jnp.zeros(shp, F32)
        for s in range(HG_SUB):
            w = q3 * jnp.exp(b3 - b3[:, s:s + 1, :]) * k3[:, s:s + 1, :]
            a_col = jnp.where(sub_row >= s, jnp.sum(w, axis=-1, keepdims=True), 0.0)
            od = od + a_col * v3[:, s:s + 1, :]
        o = o + od.reshape(chunk, HG_DV)
        b_last = b[chunk - 1:chunk, :]
        kd = k * jnp.exp(b_last - b)
        st_sc[h] = st * jnp.exp(b_last) + _dot_tn(v.astype(BF16), kd.astype(BF16))
        g = g_ref[pl.ds(r0, chunk), hs]
        a_ref[pl.ds(r0, chunk), hs] = (_rms(o, norm_ref[:, hs]) * (g * _sigmoid(g))).astype(BF16)

    def body(c, carry):
        r0 = pl.multiple_of(c * chunk, chunk)
        for h in range(HG_HEADS):
            head_chunk(r0, h)
        return carry

    lax.fori_loop(0, n_chunks, body, 0)

    @pl.when(t == pl.num_programs(1) - 1)
    def _():
        for h in range(HG_HEADS):
            sout_ref[h] = st_sc[h].T


def _hgrn(hg, lb, norm, s0, lt, chunk):
    b, l, _ = hg.shape
    kern = functools.partial(_hgrn_kernel, chunk=chunk, n_chunks=lt // chunk)
    seg = lambda j: pl.BlockSpec((None, lt, HG_WIDTH), lambda bi, t: (bi, t, j))
    head_vec = _const_spec((1, HG_WIDTH))
    state = pl.BlockSpec((None, HG_HEADS, HG_DK, HG_DV), lambda bi, t: (bi, 0, 0, 0))
    return pl.pallas_call(
        kern, grid=(b, l // lt),
        in_specs=[seg(0), seg(1), seg(2), seg(3), head_vec, head_vec, state],
        out_specs=(pl.BlockSpec((None, lt, HG_WIDTH), lambda bi, t: (bi, t, 0)), state),
        out_shape=(jax.ShapeDtypeStruct((b, l, HG_WIDTH), BF16), jax.ShapeDtypeStruct(s0.shape, F32)),
        scratch_shapes=[pltpu.VMEM((HG_HEADS, HG_DV, HG_DK), F32)],
        compiler_params=_params(2), name="hgrn",
    )(hg, hg, hg, hg, lb, norm, s0)


def _memkv_kernel(m_ref, g_ref, wk_ref, wv_ref, k_ref, v_ref):
    m = _rms(m_ref[...], g_ref[...]).astype(BF16)
    k_ref[...] = _dot(m, wk_ref[...])
    v_ref[...] = _dot(m, wv_ref[...])


def _memkv(mem, lw, tm):
    n = mem.shape[0]
    tok = lambda i: (i, 0)
    consts = [lw["g_mem"], lw["w_mk"], lw["w_mv"]]
    out_shape = (jax.ShapeDtypeStruct((n, X_WIDTH), F32),) * 2
    return pl.pallas_call(
        _memkv_kernel, grid=(n // tm,),
        in_specs=[pl.BlockSpec((tm, D_MODEL), tok)] + [_const_spec(c.shape) for c in consts],
        out_specs=(pl.BlockSpec((tm, X_WIDTH), tok),) * 2, out_shape=out_shape,
        compiler_params=_params(1), name="memkv",
    )(mem, *consts)


def _merge_kernel(x_ref, a_ref, bo_ref, sg_ref, mk_ref, mv_ref, wpa_ref, wpb_ref, wout_ref, gx_ref, wxq_ref, wxo_ref,
                  gffn_ref, wr_ref, br_ref, *rest, n_seg, seg, n_tiles):
    h_ref, u_ref, lg_ref = rest[-3:]

    @pl.when(pl.program_id(0) >= n_tiles)
    def _():
        u_ref[...] = jnp.zeros(u_ref.shape, F32)
        lg_ref[...] = jnp.zeros(lg_ref.shape, F32)

    @pl.when(pl.program_id(0) < n_tiles)
    def _():
        _merge_tile(x_ref, a_ref, bo_ref, sg_ref, mk_ref, mv_ref, wpa_ref, wpb_ref, wout_ref, gx_ref, wxq_ref, wxo_ref,
                    gffn_ref, wr_ref, br_ref, h_ref, u_ref, lg_ref, n_seg=n_seg, seg=seg)


def _merge_tile(x_ref, a_ref, bo_ref, sg_ref, mk_ref, mv_ref, wpa_ref, wpb_ref, wout_ref, gx_ref, wxq_ref, wxo_ref,
                gffn_ref, wr_ref, br_ref, h_ref, u_ref, lg_ref, *, n_seg, seg):
    sg = sg_ref[...]
    mix = (sg[:, :D_MODEL].astype(F32) * _dot(a_ref[...], wpa_ref[...])
           + sg[:, D_MODEL:].astype(F32) * _dot(bo_ref[...], wpb_ref[...]))
    h1 = x_ref[...] + _dot(mix.astype(BF16), wout_ref[...])
    u2 = _rms(h1, gx_ref[...]).astype(BF16)
    qx = (_dot(u2, wxq_ref[...]) * X_HEAD_DIM ** -0.5).astype(BF16)
    segs = []
    for j in range(n_seg):
        kj = mk_ref[j].astype(BF16)
        vj = mv_ref[j].astype(BF16)
        heads = []
        for h in range(X_HEADS):
            hs = slice(h * X_HEAD_DIM, (h + 1) * X_HEAD_DIM)
            s = _dot_nt(qx[j * seg:(j + 1) * seg, hs], kj[:, hs])
            p = jnp.exp(s - jnp.max(s, axis=-1, keepdims=True))
            heads.append(_dot(p.astype(BF16), vj[:, hs]) / jnp.sum(p, axis=-1, keepdims=True))
        segs.append(jnp.concatenate(heads, axis=-1))
    ox = segs[0] if n_seg == 1 else jnp.concatenate(segs, axis=0)
    h2 = h1 + _dot(ox.astype(BF16), wxo_ref[...])
    h_ref[...] = h2
    u3 = _rms(h2, gffn_ref[...])
    _to_row_tiles(u_ref, u3)
    lg_ref[...] = _dot(u3.astype(BF16), wr_ref[...]) + br_ref[...]


def _merge(x, a, bo, sg, mk, mv, lw, tm, seq_len, t_all, shared=None, tok_off=0):
    t = x.shape[0]
    n_mem = mk.shape[1]
    n_seg = max(1, tm // seq_len)
    seg = tm // n_seg
    tiles_per_seq = max(1, seq_len // tm)
    n_tiles = t // tm
    assert t_all % tm == 0 and tok_off % tm == 0
    n_steps = n_tiles if shared is not None else t_all // tm
    assert shared is not None or tok_off == 0
    tok = lambda i: (jnp.minimum(i, n_tiles - 1), 0)
    out_tok = lambda i: (tok_off // tm + i, 0)
    mem = pl.BlockSpec((n_seg, n_mem, X_WIDTH), lambda i: (jnp.minimum(i, n_tiles - 1) // tiles_per_seq, 0, 0))
    consts = [lw["w_pa"], lw["w_pb"], lw["w_out"], lw["g_x"], lw["w_xq"], lw["w_xo"], lw["g_ffn"], lw["w_router"],
              lw["b_router"]]
    out_shape = (jax.ShapeDtypeStruct((t, D_MODEL), F32), jax.ShapeDtypeStruct((t_all * ROW_TILE, LANES), F32),
                 jax.ShapeDtypeStruct((t_all, LANES), F32))
    out_specs = (pl.BlockSpec((tm, D_MODEL), tok), pl.BlockSpec((tm * ROW_TILE, LANES), out_tok),
                 pl.BlockSpec((tm, LANES), out_tok))
    in_specs = ([pl.BlockSpec((tm, D_MODEL), tok), pl.BlockSpec((tm, HG_WIDTH), tok), pl.BlockSpec((tm, MLA_WIDTH), tok),
                 pl.BlockSpec((tm, 2 * D_MODEL), tok), mem, mem] + [_const_spec(c.shape) for c in consts])
    args = [x, a, bo, sg, mk, mv, *consts]
    aliases = {}
    if shared is not None:
        aliases = {len(args): 1, len(args) + 1: 2}
        in_specs += [pl.BlockSpec(memory_space=pl.ANY)] * 2
        args += list(shared)
    kern = functools.partial(_merge_kernel, n_seg=n_seg, seg=seg, n_tiles=n_tiles)
    return pl.pallas_call(
        kern, grid=(n_steps,), in_specs=in_specs, out_specs=out_specs, out_shape=out_shape, input_output_aliases=aliases,
        compiler_params=_params(1), name="merge",
    )(*args)


def _route_kernel(lg_ref, gate_ref, ir_ref, cum_ref, cnt_ref, carry_sc, *, tr):
    i = pl.program_id(0)

    @pl.when(i == 0)
    def _():
        carry_sc[...] = jnp.zeros(carry_sc.shape, F32)

    lane_i = lax.broadcasted_iota(I32, (tr, LANES), 1)
    lane = lane_i.astype(F32)
    logit = jnp.where(lane < N_EXPERTS, lg_ref[...], -jnp.inf)
    vals, idxs, hots = [], [], []
    for _ in range(TOP_K):
        top = jnp.max(logit, axis=-1, keepdims=True)
        first = jnp.min(jnp.where(logit == top, lane, float(LANES)), axis=-1, keepdims=True)
        hot = lane == first
        vals.append(top)
        idxs.append(first)
        hots.append(hot)
        logit = jnp.where(hot, -jnp.inf, logit)
    exps = [jnp.exp(v - vals[0]) for v in vals]
    denom = exps[0] + exps[1] + exps[2] + exps[3]
    hot_all = sum(jnp.where(h, 1.0, 0.0) for h in hots)
    row = lax.broadcasted_iota(I32, (tr, tr), 0)
    col = lax.broadcasted_iota(I32, (tr, tr), 1)
    before = jnp.where(row > col, 1.0, 0.0).astype(BF16)
    carry = carry_sc[...]
    prefix = _dot(before, hot_all.astype(BF16)) + carry
    lane4 = lax.broadcasted_iota(I32, (tr, TOP_K), 1)
    gate = jnp.zeros((tr, TOP_K), F32)
    cols = jnp.zeros((tr, LANES), F32)
    for k in range(TOP_K):
        rk = jnp.sum(jnp.where(hots[k], prefix, 0.0), axis=-1, keepdims=True)
        gate = jnp.where(lane4 == k, exps[k] / denom, gate)
        cols = jnp.where(lane_i == k, idxs[k], jnp.where(lane_i == TOP_K + k, rk, cols))
    gate_ref[...] = gate
    ir_ref[i] = cols.T[0:2 * TOP_K, :]
    cum_ref[pl.ds(i, 1), :] = carry
    carry = carry + jnp.sum(hot_all, axis=0, keepdims=True)
    carry_sc[...] = carry
    cnt_ref[...] = carry


def _route(logits, tr):
    t = logits.shape[0]
    n_tiles = t // tr
    tok = lambda i: (i, 0)
    out_shape = (jax.ShapeDtypeStruct((t, TOP_K), F32), jax.ShapeDtypeStruct((n_tiles, 2 * TOP_K, tr), F32),
                 jax.ShapeDtypeStruct((n_tiles, LANES), F32), jax.ShapeDtypeStruct((1, LANES), F32))
    return pl.pallas_call(
        functools.partial(_route_kernel, tr=tr), grid=(n_tiles,),
        in_specs=[pl.BlockSpec((tr, LANES), tok)],
        out_specs=(pl.BlockSpec((tr, TOP_K), tok), pl.BlockSpec((n_tiles, 2 * TOP_K, tr), lambda i: (0, 0, 0)),
                   pl.BlockSpec((n_tiles, LANES), lambda i: (0, 0)), pl.BlockSpec((1, LANES), lambda i: (0, 0))),
        out_shape=out_shape, scratch_shapes=[pltpu.VMEM((1, LANES), F32)],
        compiler_params=_params(1), name="route",
    )(logits)


def _rowlist_kernel(be_ref, j0_ref, tlo_ref, thi_ref, cum_ref, ir_ref, out_ref, mark_sc, *, tr, bm, group):
    g = pl.program_id(0)
    n_fold = tr // LANES
    win = min(bm, ROWLIST_WINDOW)
    tl = lax.broadcasted_iota(I32, (1, tr), 1)
    lo_code = (tl % 256 + 1).astype(F32).astype(BF16)
    hi_code = (tl // 256 * TOP_K + 1).astype(F32)
    jrow = lax.broadcasted_iota(I32, (win, tr), 0).astype(F32).astype(BF16)
    kcol = lax.broadcasted_iota(I32, (TOP_K, tr), 0).astype(F32)
    lane = lax.broadcasted_iota(I32, (bm, LANES), 1)
    zero = jnp.zeros((), BF16)

    def fold(x):
        parts = [x[:, c * LANES:(c + 1) * LANES] for c in range(n_fold)]
        while len(parts) > 1:
            parts = [jnp.maximum(parts[c], parts[c + 1]) for c in range(0, len(parts), 2)]
        return parts[0]

    for q in range(group):
        b = g * group + q
        ei = be_ref[b]
        e = ei.astype(F32)
        j0i = j0_ref[b]
        j0 = j0i.astype(F32)
        mark_sc[...] = jnp.zeros(mark_sc.shape, BF16)

        def tile(t, c, ei=ei, e=e, j0i=j0i, j0=j0):
            ir = ir_ref[t]
            mine = ir[0:TOP_K, :] == e
            rel = jnp.sum(jnp.where(mine, ir[TOP_K:2 * TOP_K, :] - j0, 0.0), axis=0, keepdims=True)
            hit = jnp.sum(jnp.where(mine, 1.0, 0.0), axis=0, keepdims=True) > 0.0
            slot = jnp.sum(jnp.where(mine, kcol, 0.0), axis=0, keepdims=True)
            rel = jnp.where(hit, rel, -1e6)
            hk_code = jnp.broadcast_to((hi_code + slot).astype(BF16), (win, tr))
            tile_no = jnp.broadcast_to(jnp.full((1, LANES), t + 1, I32).astype(F32).astype(BF16), (win, LANES))
            first = jnp.maximum(cum_ref[t * N_EXPERTS + ei] - j0i, 0)
            last = jnp.minimum(cum_ref[(t + 1) * N_EXPERTS + ei] - j0i, bm)
            start = first // ROWLIST_ALIGN * ROWLIST_ALIGN
            n_win = jnp.maximum(last - start + win - 1, 0) // win

            def window(w, c2):
                ws = pl.multiple_of(jnp.minimum(start + w * win, bm - win), ROWLIST_ALIGN)
                relw = jnp.clip(rel - ws.astype(F32), -1.0, float(win)).astype(BF16)
                match = jnp.broadcast_to(relw, (win, tr)) == jrow
                f_lo = fold(jnp.where(match, jnp.broadcast_to(lo_code, (win, tr)), zero))
                f_hk = fold(jnp.where(match, hk_code, zero))
                rows = pl.ds(ws, win)
                mark_sc[0, rows, :] = jnp.maximum(mark_sc[0, rows, :], f_lo)
                mark_sc[1, rows, :] = jnp.maximum(mark_sc[1, rows, :], f_hk)
                mark_sc[2, rows, :] = jnp.where(f_lo > zero, tile_no, mark_sc[2, rows, :])
                return c2

            lax.fori_loop(0, n_win, window, 0)
            return c

        lax.fori_loop(tlo_ref[b], thi_ref[b], tile, 0)
        lo = jnp.max(mark_sc[0].astype(F32), axis=-1, keepdims=True)
        hk = jnp.max(mark_sc[1].astype(F32), axis=-1, keepdims=True) - 1.0
        tile_no = jnp.max(mark_sc[2].astype(F32), axis=-1, keepdims=True) - 1.0
        found = lo > 0.0
        hi = jnp.floor(hk * (1.0 / TOP_K))
        tok = jnp.where(found, tile_no * float(tr) + hi * 256.0 + lo - 1.0, 0.0)
        k = jnp.where(found, hk - hi * TOP_K, 0.0)
        cols = jnp.where(lane == 0, tok, jnp.where(lane == 1, k, jnp.where(lane == 2, jnp.where(found, 1.0, 0.0), 0.0)))
        out_ref[q] = cols.T[0:ROW_TILE, :].astype(I32)


def _rowlist(ir, block_expert, block_j0, tile_lo, tile_hi, cum_flat, bm):
    n_tiles, _, tr = ir.shape
    n_blocks = block_expert.shape[0]
    assert n_tiles < 256 and bm <= 256 and tr <= 8192
    group = 8
    while n_blocks % group:
        group //= 2
    grid_spec = pltpu.PrefetchScalarGridSpec(
        num_scalar_prefetch=5, grid=(n_blocks // group,),
        in_specs=[pl.BlockSpec((n_tiles, 2 * TOP_K, tr), lambda g, *_: (0, 0, 0))],
        out_specs=pl.BlockSpec((group, ROW_TILE, bm), lambda g, *_: (g, 0, 0)),
        scratch_shapes=[pltpu.VMEM((3, bm, LANES), BF16)])
    return pl.pallas_call(
        functools.partial(_rowlist_kernel, tr=tr, bm=bm, group=group), grid_spec=grid_spec,
        out_shape=jax.ShapeDtypeStruct((n_blocks, ROW_TILE, bm), I32),
        compiler_params=_params(1), name="rowlist",
    )(block_expert, block_j0, tile_lo, tile_hi, cum_flat, ir)


def _to_row_tiles(ref, x):
    n = x.shape[0]
    for c in range(ROW_TILE):
        ref[pl.ds(c, n, stride=ROW_TILE), :] = x[:, c * LANES:(c + 1) * LANES]


def _row_tile_chunk(ref, n, c):
    return ref[pl.ds(c, n, stride=ROW_TILE), :]


def _moe_kernel(be_ref, *refs, bm, n_blocks):
    rt_first = refs[:MOE_AHEAD]
    rtn_ref, rdp_ref, x_hbm, wgu_ref, bgu_ref, wd_ref, bd_ref, y_hbm = refs[MOE_AHEAD:MOE_AHEAD + 8]
    scratch = refs[MOE_AHEAD + 8:]
    xs, ys = scratch[:MOE_BUFFERS], scratch[MOE_BUFFERS:2 * MOE_BUFFERS]
    wgu_bf, wd_bf, gsem, ssem = scratch[2 * MOE_BUFFERS:]
    b = pl.program_id(0)
    rows = bm * ROW_TILE

    def gather(idx_ref, r, slot):
        src = pl.multiple_of(idx_ref[0, 0, r], ROW_TILE)
        return pltpu.make_async_copy(x_hbm.at[pl.ds(src, ROW_TILE), :], xs[slot].at[pl.ds(r * ROW_TILE, ROW_TILE), :],
                                     gsem.at[slot])

    def scatter(r, slot):
        dst = pl.multiple_of(rdp_ref[0, 0, r], ROW_TILE)
        return pltpu.make_async_copy(ys[slot].at[pl.ds(r * ROW_TILE, ROW_TILE), :], y_hbm.at[pl.ds(dst, ROW_TILE), :],
                                     ssem.at[slot])

    def wait_gathers(slot):
        pltpu.make_async_copy(x_hbm.at[pl.ds(0, rows), :], xs[slot], gsem.at[slot]).wait()

    def wait_scatters(slot):
        pltpu.make_async_copy(ys[slot], y_hbm.at[pl.ds(0, rows), :], ssem.at[slot]).wait()

    @pl.when(b == 0)
    def _():
        ys[MOE_BUFFERS - 1][...] = jnp.zeros(ys[MOE_BUFFERS - 1].shape, F32)
        for j in range(MOE_AHEAD):
            def start(r, c, j=j):
                gather(rt_first[j], r, j).start()
                return c

            lax.fori_loop(0, bm, start, 0)

    def step(slot):
        prev = (slot - 1) % MOE_BUFFERS
        wait_gathers(slot)

        @pl.when(b >= MOE_BUFFERS - 1)
        def _():
            wait_scatters(slot)

        for r in range(bm):
            gather(rtn_ref, r, prev).start()
        for r in range(bm):
            scatter(r, prev).start(priority=1)
        x = jnp.concatenate([_row_tile_chunk(xs[slot], bm, c) for c in range(ROW_TILE)], axis=-1).astype(BF16)
        gu = _dot(x, wgu_bf[...]) + bgu_ref[...]
        g = jnp.minimum(gu[:, :D_FF], SWIGLU_LIMIT)
        up = jnp.clip(gu[:, D_FF:], -SWIGLU_LIMIT, SWIGLU_LIMIT)
        act = (up + 1.0) * g * _sigmoid(SWIGLU_ALPHA * g)
        _to_row_tiles(ys[slot], _dot(act.astype(BF16), wd_bf[...]) + bd_ref[...])

    expert = be_ref[jnp.minimum(b, n_blocks - 1)]
    new_expert = jnp.logical_or(b == 0, expert != be_ref[jnp.maximum(b - 1, 0)])

    @pl.when(jnp.logical_and(b < n_blocks, new_expert))
    def _():
        wgu_bf[...] = wgu_ref[...].astype(BF16)
        wd_bf[...] = wd_ref[...].astype(BF16)

    for v in range(MOE_BUFFERS):
        @pl.when(jnp.logical_and(b < n_blocks, b % MOE_BUFFERS == v))
        def _(v=v):
            step(v)

    @pl.when(b == n_blocks)
    def _():
        for j in range(MOE_AHEAD):
            wait_gathers((n_blocks + j) % MOE_BUFFERS)
        for j in range(2, MOE_BUFFERS + 1):
            wait_scatters((n_blocks - j) % MOE_BUFFERS)
        last = (n_blocks - 1) % MOE_BUFFERS

        def start(r, c):
            scatter(r, last).start()
            return c

        lax.fori_loop(0, bm, start, 0)
        wait_scatters(last)


def _moe(x, block_expert, row_src, row_dst, lw, bm, n_out_rows):
    n_blocks = block_expert.shape[0]
    assert n_blocks >= MOE_BUFFERS
    spare = (n_out_rows // ROW_TILE - bm + jnp.arange(bm, dtype=I32)) * ROW_TILE
    src = row_src.reshape(n_blocks, 1, bm)
    dst_prev = jnp.concatenate([spare.reshape(1, 1, bm), row_dst.reshape(n_blocks, 1, bm)], axis=0)
    smem = lambda f: pl.BlockSpec((1, 1, bm), f, memory_space=pltpu.SMEM)
    blk = lambda b: jnp.minimum(b, n_blocks - 1)
    first = [smem(lambda b, be, j=j: (j, 0, 0)) for j in range(MOE_AHEAD)]
    grid_spec = pltpu.PrefetchScalarGridSpec(
        num_scalar_prefetch=1, grid=(n_blocks + 1,),
        in_specs=first + [smem(lambda b, be: (blk(b + MOE_AHEAD), 0, 0)), smem(lambda b, be: (b, 0, 0)),
                  pl.BlockSpec(memory_space=pl.ANY),
                  pl.BlockSpec((None, D_MODEL, 2 * D_FF), lambda b, be: (be[blk(b)], 0, 0)),
                  pl.BlockSpec((None, 1, 2 * D_FF), lambda b, be: (be[blk(b)], 0, 0)),
                  pl.BlockSpec((None, D_FF, D_MODEL), lambda b, be: (be[blk(b)], 0, 0)),
                  pl.BlockSpec((None, 1, D_MODEL), lambda b, be: (be[blk(b)], 0, 0))],
        out_specs=pl.BlockSpec(memory_space=pl.ANY),
        scratch_shapes=[pltpu.VMEM((bm * ROW_TILE, LANES), F32)] * (2 * MOE_BUFFERS)
        + [pltpu.VMEM((D_MODEL, 2 * D_FF), BF16), pltpu.VMEM((D_FF, D_MODEL), BF16)]
        + [pltpu.SemaphoreType.DMA((MOE_BUFFERS,))] * 2)
    return pl.pallas_call(
        functools.partial(_moe_kernel, bm=bm, n_blocks=n_blocks), grid_spec=grid_spec,
        out_shape=jax.ShapeDtypeStruct((n_out_rows, LANES), F32),
        compiler_params=_params(1), name="moe",
    )(block_expert, *([src] * (MOE_AHEAD + 1)), dst_prev, x, lw["w_gu"], lw["b_gu"], lw["w_down"], lw["b_down"])


def _final_kernel(h_ref, gate_ref, y0_ref, y1_ref, y2_ref, y3_ref, g_ref, o_ref, *, tm):
    gate = gate_ref[...]
    gates = [jnp.broadcast_to(gate[:, k:k + 1], (tm, LANES)) for k in range(TOP_K)]
    ssq = jnp.zeros((tm, 1), F32)
    for c in range(ROW_TILE):
        cs = slice(c * LANES, (c + 1) * LANES)
        h = h_ref[:, cs]
        for k, y_ref in enumerate((y0_ref, y1_ref, y2_ref, y3_ref)):
            h = h + gates[k] * _row_tile_chunk(y_ref, tm, c)
        o_ref[:, cs] = h
        ssq = ssq + jnp.sum(h * h, axis=-1, keepdims=True)
    o_ref[...] = o_ref[...] * lax.rsqrt(ssq * (1.0 / D_MODEL) + EPS) * g_ref[...]


def _final(h, gate, y4, g_final, tm, tok_off, t_all):
    t = h.shape[0]
    tok = lambda i: (i, 0)
    slot = lambda k: pl.BlockSpec((tm * ROW_TILE, LANES), lambda i: ((k * t_all + tok_off) // tm + i, 0))
    return pl.pallas_call(
        functools.partial(_final_kernel, tm=tm), grid=(t // tm,),
        in_specs=[pl.BlockSpec((tm, D_MODEL), tok), pl.BlockSpec((tm, TOP_K), lambda i: (tok_off // tm + i, 0)),
                  slot(0), slot(1), slot(2), slot(3), _const_spec(g_final.shape)],
        out_specs=pl.BlockSpec((tm, D_MODEL), tok), out_shape=jax.ShapeDtypeStruct((t, D_MODEL), F32),
        compiler_params=_params(1), name="final",
    )(h, gate, y4, y4, y4, y4, g_final)


def _rope_tables(pos, reps):
    half = QK_ROPE // 2
    inv_freq = jnp.exp(-math.log(ROPE_THETA) * jnp.arange(half, dtype=F32) / half)
    ang = pos.astype(F32)[:, None] * inv_freq[None, :]
    cos = jnp.concatenate([jnp.cos(ang)] * 2, axis=-1)
    sin = jnp.concatenate([jnp.sin(ang)] * 2, axis=-1)
    n = pos.shape[0]
    ones = jnp.ones((n, QK_NOPE), F32)
    z = lambda w: jnp.zeros((n, w), F32)
    tabs = {"cosq": jnp.concatenate([ones, cos, z(HEAD_PAD - QK_NOPE - QK_ROPE)], axis=-1),
            "sinq": jnp.concatenate([z(QK_NOPE), sin, z(HEAD_PAD - QK_NOPE - QK_ROPE)], axis=-1),
            "cosk": jnp.concatenate([cos, z(LANES - QK_ROPE)], axis=-1),
            "sink": jnp.concatenate([sin, z(LANES - QK_ROPE)], axis=-1)}
    return {k: jnp.tile(v, (reps, 1)) for k, v in tabs.items()}


def _rot_cols(w):
    half = QK_ROPE // 2
    return jnp.concatenate([-w[..., half:], w[..., :half]], axis=-1)


def _layer_weights(l, lb, g_mix, w_in, g_qa, w_uq, g_kva, w_ukv, hg_norm, w_pa, w_pb, w_out, g_mem, w_mk, w_mv, g_x, w_xq,
                   w_xo, g_ffn, w_router, b_router, w_gu, b_gu, w_down, b_down):
    row = lambda v: v.reshape(1, -1).astype(F32)
    wi = w_in[l]
    o_qa = 4 * HG_WIDTH
    o_kv = o_qa + Q_LORA
    o_kr = o_kv + KV_LORA
    o_g = o_kr + QK_ROPE
    w_kr = wi[:, o_kr:o_g]
    zk = jnp.zeros((D_MODEL, LANES - QK_ROPE), F32)
    uq = w_uq[l].reshape(Q_LORA, MLA_HEADS, QK_NOPE + QK_ROPE)
    pad = HEAD_PAD - QK_NOPE - QK_ROPE
    assert pad == QK_ROPE
    w_q = jnp.concatenate([uq, _rot_cols(uq[..., QK_NOPE:])], axis=-1).reshape(Q_LORA, MLA_HEADS * HEAD_PAD)
    ukv = w_ukv[l].reshape(KV_LORA, MLA_HEADS, QK_NOPE + V_HEAD)
    w_kn = jnp.concatenate([ukv[..., :QK_NOPE], jnp.zeros((KV_LORA, MLA_HEADS, HEAD_PAD - QK_NOPE), F32)], axis=-1)
    e_head = jnp.concatenate([jnp.zeros((QK_ROPE, QK_NOPE), F32), jnp.eye(QK_ROPE, dtype=F32),
                              jnp.zeros((QK_ROPE, pad), F32)], axis=-1)
    e_kpe = jnp.concatenate([jnp.tile(e_head, (1, MLA_HEADS)), jnp.zeros((LANES - QK_ROPE, MLA_HEADS * HEAD_PAD), F32)], axis=0)
    bf = lambda w: w.astype(BF16)
    return {
        "lb": row(lb[l]), "g_mix": row(g_mix[l]), "w_hg": bf(wi[:, :o_qa]), "w_g": bf(wi[:, o_g:]),
        "w_qa": bf(wi[:, o_qa:o_kv]), "w_kv": bf(wi[:, o_kv:o_kr]),
        "w_kr": bf(jnp.concatenate([w_kr, zk, _rot_cols(w_kr), zk], axis=-1)),
        "g_qa": row(g_qa[l]), "w_q": bf(w_q), "g_kva": row(g_kva[l]),
        "w_kn": bf(w_kn.reshape(KV_LORA, MLA_HEADS * HEAD_PAD)), "w_v": bf(ukv[..., QK_NOPE:].reshape(KV_LORA, MLA_WIDTH)),
        "e_kpe": bf(e_kpe), "hg_norm": row(hg_norm[l]), "w_pa": bf(w_pa[l]), "w_pb": bf(w_pb[l]), "w_out": bf(w_out[l]),
        "g_mem": row(g_mem[l]), "w_mk": bf(w_mk[l]), "w_mv": bf(w_mv[l]), "g_x": row(g_x[l]), "w_xq": bf(w_xq[l]),
        "w_xo": bf(w_xo[l]), "g_ffn": row(g_ffn[l]),
        "w_router": bf(jnp.pad(w_router[l], ((0, 0), (0, LANES - N_EXPERTS)))),
        "b_router": jnp.pad(row(b_router[l]), ((0, 0), (0, LANES - N_EXPERTS))),
        "w_gu": w_gu[l], "b_gu": b_gu[l].reshape(N_EXPERTS, 1, 2 * D_FF).astype(F32),
        "w_down": w_down[l], "b_down": b_down[l].reshape(N_EXPERTS, 1, D_MODEL).astype(F32),
    }


def _tile(n, want):
    t = min(n, want)
    while n % t:
        t -= 8
    assert t > 0 and n % t == 0, (n, want)
    return t


def _group_front(x, lw, tabs, seq_len, hg_state, past):
    b, l, _ = x.shape
    t = b * l
    tm = _tile(t, TOKEN_TILE)
    hg, sg, q, ckv, kpe = _inproj(x.reshape(t, D_MODEL), lw, tabs, tm)
    lt = _tile(l, 512)
    a, s_new = _hgrn(hg.reshape(b, l, 4 * HG_WIDTH), lw["lb"], lw["hg_norm"], hg_state, lt, _tile(lt, HG_CHUNK))
    tq = _tile(l, ATTN_Q_TILE)
    if past is None:
        lk_true, q_off = l, 0
        ckv_all, kpe_all = ckv, kpe
        lk = l
    else:
        past_ckv, past_kpe = past
        q_off = past_ckv.shape[1]
        lk_true = q_off + l
        lk = -(-lk_true // ATTN_TILE) * ATTN_TILE
        padr = ((0, 0), (0, lk - lk_true), (0, 0))
        ckv_all = jnp.pad(jnp.concatenate([past_ckv, ckv.reshape(b, l, KV_LORA)], axis=1), padr).reshape(b * lk, KV_LORA)
        past_kpe = jnp.pad(past_kpe, ((0, 0), (0, 0), (0, LANES - QK_ROPE)))
        kpe_all = jnp.pad(jnp.concatenate([past_kpe, kpe.reshape(b, l, LANES)], axis=1), padr).reshape(b * lk, LANES)
    k_arr, v_arr = _kvbuild(ckv_all, kpe_all, lw, _tile(b * lk, KV_TILE))
    bo = _attention(q.reshape(b, l, MLA_HEADS * HEAD_PAD), k_arr.reshape(b, lk, MLA_HEADS * HEAD_PAD),
                    v_arr.reshape(b, lk, MLA_WIDTH), lk_true, q_off, tq, _tile(lk, ATTN_TILE))
    return a.reshape(t, HG_WIDTH), bo.reshape(t, MLA_WIDTH), sg, ckv.reshape(b, l, KV_LORA), kpe[:, :QK_ROPE].reshape(b, l, QK_ROPE), s_new


def kernel(x_prompt, x_sample, cache_mla_ckv, cache_mla_kpe, state_hgrn, cache_mem_k, cache_mem_v, mem_prompt, hg_lb_logits, g_mix, w_in, g_qa, w_uq, g_kva, w_ukv, hg_norm, w_pa, w_pb, w_out, g_mem, w_mk, w_mv, g_x, w_xq, w_xo, g_ffn, w_router, b_router, w_gu, b_gu, w_down, b_down, g_final):
    bp, lp, _ = x_prompt.shape
    bs, ls, _ = x_sample.shape
    depth = w_in.shape[0]
    past_len = cache_mla_ckv.shape[2]
    n_mem = mem_prompt.shape[1]
    tp, ts = bp * lp, bs * ls
    t_all = tp + ts
    tm_p, tm_s = _tile(tp, TOKEN_TILE), _tile(ts, TOKEN_TILE)
    assert tp % tm_s == 0
    lb_all = jnp.cumsum(jax.nn.softmax(hg_lb_logits.astype(F32), axis=0), axis=0)
    tabs_p = _rope_tables(jnp.arange(lp, dtype=I32), max(1, tm_p // lp))
    tabs_s = _rope_tables(past_len + jnp.arange(ls, dtype=I32), max(1, tm_s // ls))
    g_fin = g_final.reshape(1, D_MODEL).astype(F32)

    n_asg = t_all * TOP_K
    bm = MOE_ROWS
    n_blocks = (n_asg + N_EXPERTS * (bm - 1) + bm - 1) // bm
    n_rows = n_blocks * bm

    hp, hs = x_prompt.reshape(tp, D_MODEL), x_sample.reshape(ts, D_MODEL)
    outs = [[] for _ in range(8)]
    for l in range(depth):
        lw = _layer_weights(l, lb_all, g_mix, w_in, g_qa, w_uq, g_kva, w_ukv, hg_norm, w_pa, w_pb, w_out, g_mem, w_mk, w_mv,
                            g_x, w_xq, w_xo, g_ffn, w_router, b_router, w_gu, b_gu, w_down, b_down)
        mk, mv = _memkv(mem_prompt.reshape(bp * n_mem, D_MODEL), lw, _tile(bp * n_mem, TOKEN_TILE))
        mk, mv = mk.reshape(bp, n_mem, X_WIDTH), mv.reshape(bp, n_mem, X_WIDTH)
        zero_state = jnp.zeros((bp, HG_HEADS, HG_DK, HG_DV), F32)
        a_p, bo_p, sg_p, ckv_p, kpe_p, st_p = _group_front(hp.reshape(bp, lp, D_MODEL), lw, tabs_p, lp, zero_state, None)
        a_s, bo_s, sg_s, ckv_s, kpe_s, st_s = _group_front(hs.reshape(bs, ls, D_MODEL), lw, tabs_s, ls, state_hgrn[l],
                                                           (cache_mla_ckv[l], cache_mla_kpe[l]))
        h2_p, u3, lg = _merge(hp, a_p, bo_p, sg_p, mk, mv, lw, tm_p, lp, t_all)
        h2_s, u3, lg = _merge(hs, a_s, bo_s, sg_s, cache_mem_k[l].reshape(bs, n_mem, X_WIDTH),
                              cache_mem_v[l].reshape(bs, n_mem, X_WIDTH), lw, tm_s, ls, t_all, shared=(u3, lg), tok_off=tp)
        gate, ir, cum, cnt = _route(lg, _tile(t_all, TOKEN_TILE))
        counts = cnt[0, :N_EXPERTS].astype(I32)
        padded = (counts + bm - 1) // bm * bm
        pend = jnp.cumsum(padded)

        def owner(start):
            return jnp.minimum(jnp.sum((pend[None, :] <= start[:, None]).astype(I32), axis=1), N_EXPERTS - 1)

        block_expert = owner(jnp.arange(n_blocks, dtype=I32) * bm)
        sub_start = jnp.arange(n_rows // ROWLIST_ROWS, dtype=I32) * ROWLIST_ROWS
        sub_expert = owner(sub_start)
        sub_j0 = sub_start - (pend - padded)[sub_expert]
        cum_be = jnp.take(cum[:, :N_EXPERTS].astype(I32), sub_expert, axis=1)
        tile_lo = jnp.maximum(jnp.sum((cum_be <= sub_j0[None, :]).astype(I32), axis=0) - 1, 0)
        tile_hi = jnp.sum((cum_be < (sub_j0 + ROWLIST_ROWS)[None, :]).astype(I32), axis=0)
        cum_flat = jnp.concatenate([cum[:, :N_EXPERTS], cnt[:, :N_EXPERTS]], axis=0).astype(I32).reshape(-1)
        rows = _rowlist(ir, sub_expert, sub_j0, tile_lo, tile_hi, cum_flat, ROWLIST_ROWS)
        row_t, row_k, pad_row = rows[:, 0, :].reshape(-1), rows[:, 1, :].reshape(-1), rows[:, 2, :].reshape(-1) == 0
        row_src = jnp.where(pad_row, 0, row_t) * ROW_TILE
        r = jnp.arange(n_rows, dtype=I32)
        spare = n_asg + (r // bm % MOE_BUFFERS) * bm + r % bm
        row_dst = jnp.where(pad_row, spare, row_k * t_all + row_t) * ROW_TILE
        y4 = _moe(u3, block_expert, row_src, row_dst, lw, bm, (n_asg + MOE_BUFFERS * bm) * ROW_TILE)
        last = l == depth - 1
        gf = g_fin if last else jnp.ones_like(g_fin)
        yp = _final(h2_p, gate, y4, gf, tm_p, 0, t_all)
        ys = _final(h2_s, gate, y4, gf, tm_s, tp, t_all)
        assert last, "multi-layer stacking needs the un-normalised residual stream"
        hp, hs = yp, ys
        for lst, v in zip(outs, (ckv_p, kpe_p, st_p, mk.reshape(bp, n_mem, X_HEADS, X_HEAD_DIM),
                                 mv.reshape(bp, n_mem, X_HEADS, X_HEAD_DIM), ckv_s, kpe_s, st_s)):
            lst.append(v)
    stk = [jnp.stack(o) for o in outs]
    return (hp.reshape(bp, lp, D_MODEL), hs.reshape(bs, ls, D_MODEL), stk[0], stk[1], stk[2], stk[3], stk[4], stk[5], stk[6], stk[7])
```
